```python
import jax, jax.numpy as jnp
from jax import lax
import numpy as np

D_MODEL = 1024
BATCH = 8
SEQ = 2048
DEPTH = 2

GRID_W = 64
CTX_LEN = 256
Q_BLOCK = 128
ROPE_THETA = 10000.0
EPS = 1e-6

GQA_HEADS = 8
GQA_KV_HEADS = 2
GQA_GROUP = GQA_HEADS // GQA_KV_HEADS
GQA_HEAD_DIM = 64
GQA_SCALE = GQA_HEAD_DIM ** -0.5
MLA_HEADS = 8
MLA_Q_RANK = 384
MLA_KV_RANK = 256
MLA_NOPE_DIM = 64
MLA_ROPE_DIM = 32
MLA_V_DIM = 64
MLA_SCALE = (MLA_NOPE_DIM + MLA_ROPE_DIM) ** -0.5
MLA_UQ_WIDTH = MLA_HEADS * (MLA_NOPE_DIM + MLA_ROPE_DIM)
MLA_UKV_WIDTH = MLA_HEADS * (MLA_NOPE_DIM + MLA_V_DIM)
FOURIER_GROUPS = 4
FOURIER_GROUP_DIM = 128
FOURIER_WIDTH = FOURIER_GROUPS * FOURIER_GROUP_DIM
N_BRANCH = 3
FFN_HIDDEN = -(-8 * D_MODEL // (3 * 256)) * 256

IN_WIDTHS = (
    GQA_HEADS * GQA_HEAD_DIM,
    GQA_KV_HEADS * GQA_HEAD_DIM,
    GQA_KV_HEADS * GQA_HEAD_DIM,
    MLA_Q_RANK,
    MLA_KV_RANK,
    MLA_ROPE_DIM,
    FOURIER_WIDTH,
    N_BRANCH * D_MODEL,
)
IN_WIDTH = sum(IN_WIDTHS)

kernel_name = "hybrid_gqa_mla_fnet_prefix_dit"


def rms_norm(x, g):
    xf = x.astype(jnp.float32)
    y = xf * lax.rsqrt(jnp.mean(xf * xf, axis=-1, keepdims=True) + EPS)
    return (y * g.astype(jnp.float32)).astype(x.dtype)


def modulate(x, g, shift, scale):
    return rms_norm(x, g) * (1 + scale) + shift


def split_in(p):
    idx, acc = [], 0
    for w in IN_WIDTHS[:-1]:
        acc += w
        idx.append(acc)
    return jnp.split(p, idx, axis=-1)


def axial_rope_tables(rows, dim):
    row_id = jnp.repeat(jnp.arange(rows), GRID_W)
    col_id = jnp.tile(jnp.arange(GRID_W), rows)
    half = dim // 2
    freqs = ROPE_THETA ** (-jnp.arange(0, half, 2, dtype=jnp.float32) / half)

    def axis_angles(pos):
        ang = pos.astype(jnp.float32)[:, None] * freqs[None, :]
        return jnp.concatenate([ang, ang], axis=-1)

    ang = jnp.concatenate([axis_angles(row_id), axis_angles(col_id)], axis=-1)
    return jnp.cos(ang), jnp.sin(ang)


def apply_rope(x, cos, sin):
    dim = x.shape[-1]
    q = dim // 4
    xr = x.reshape(*x.shape[:-1], 2, 2, q)
    rot = jnp.concatenate([-xr[..., 1:2, :], xr[..., 0:1, :]], axis=-2).reshape(x.shape)
    out = x.astype(jnp.float32) * cos[None, :, None, :] + rot.astype(jnp.float32) * sin[None, :, None, :]
    return out.astype(x.dtype)


def attend(q, k, v, scale):
    B, N = q.shape[:2]
    nb = N // Q_BLOCK
    qb = jnp.moveaxis(q.reshape(B, nb, Q_BLOCK, *q.shape[2:]), 1, 0)

    def one_block(qblk):
        s = jnp.einsum('bqkgd,bmkd->bkgqm', qblk, k, preferred_element_type=jnp.float32) * scale
        p = jax.nn.softmax(s, axis=-1).astype(v.dtype)
        return jnp.einsum('bkgqm,bmke->bqkge', p, v)

    o = lax.map(one_block, qb)
    o = jnp.moveaxis(o, 0, 1)
    return o.reshape(B, N, -1)


def gqa_queries(pq, g_q, rope):
    B, N, _ = pq.shape
    q = rms_norm(pq.reshape(B, N, GQA_HEADS, GQA_HEAD_DIM), g_q)
    if rope is not None:
        q = apply_rope(q, *rope)
    return q.reshape(B, N, GQA_KV_HEADS, GQA_GROUP, GQA_HEAD_DIM)


def gqa_keys_values(pk, pv, g_k, rope):
    B, N, _ = pk.shape
    k = rms_norm(pk.reshape(B, N, GQA_KV_HEADS, GQA_HEAD_DIM), g_k)
    if rope is not None:
        k = apply_rope(k, *rope)
    v = pv.reshape(B, N, GQA_KV_HEADS, GQA_HEAD_DIM)
    return k, v


def mla_queries(pcq, g_cq, w_uq, g_qn, g_qr, rope):
    B, N, _ = pcq.shape
    q = (rms_norm(pcq, g_cq) @ w_uq).reshape(B, N, MLA_HEADS, MLA_NOPE_DIM + MLA_ROPE_DIM)
    qn = rms_norm(q[..., :MLA_NOPE_DIM], g_qn)
    qr = rms_norm(q[..., MLA_NOPE_DIM:], g_qr)
    if rope is not None:
        qr = apply_rope(qr, *rope)
    return jnp.concatenate([qn, qr], axis=-1)[:, :, :, None, :]


def mla_keys_values(pckv, pkr, g_ckv, w_ukv, g_kn, g_kr, rope):
    B, N, _ = pckv.shape
    kv = (rms_norm(pckv, g_ckv) @ w_ukv).reshape(B, N, MLA_HEADS, MLA_NOPE_DIM + MLA_V_DIM)
    kn = rms_norm(kv[..., :MLA_NOPE_DIM], g_kn)
    v = kv[..., MLA_NOPE_DIM:]
    kr = rms_norm(pkr, g_kr)[:, :, None, :]
    if rope is not None:
        kr = apply_rope(kr, *rope)
    k = jnp.concatenate([kn, jnp.broadcast_to(kr, (B, N, MLA_HEADS, MLA_ROPE_DIM))], axis=-1)
    return k, v


def fourier_mix(pf):
    B, N, _ = pf.shape
    f = pf.astype(jnp.float32).reshape(B, N, FOURIER_GROUPS, FOURIER_GROUP_DIM)
    f = jnp.fft.fft2(f, axes=(1, 3), norm="ortho").real
    return f.reshape(B, N, FOURIER_WIDTH).astype(pf.dtype)


def merge_branches(ya, yb, yc, pg, b_gate, w_br_a, w_br_b, w_br_c, w_out):
    B, N, _ = pg.shape
    g = jax.nn.sigmoid((pg + b_gate).astype(jnp.float32)).astype(ya.dtype)
    g = g.reshape(B, N, N_BRANCH, D_MODEL)
    m = g[:, :, 0] * (ya @ w_br_a) + g[:, :, 1] * (yb @ w_br_b) + g[:, :, 2] * (yc @ w_br_c)
    return m @ w_out


def swiglu(h, w_in, w_out):
    gate, up = jnp.split(h @ w_in, 2, axis=-1)
    return (jax.nn.silu(gate) * up) @ w_out


def setup_inputs(seed: int = 0) -> dict:
    key = jax.random.key(seed)
    ks = jax.random.split(key, 32)
    L, D = DEPTH, D_MODEL
    f32 = jnp.float32

    def w(k, shape, fan_in, gain=1.0):
        return jax.random.normal(k, shape, f32) * (gain * fan_in ** -0.5)

    def gain(k, shape):
        return 1.0 + 0.1 * jax.random.normal(k, shape, f32)

    def bias(k, shape):
        return 0.01 * jax.random.normal(k, shape, f32)

    return {
        "x": jax.random.normal(ks[0], (BATCH, SEQ, D), f32),
        "c": jax.random.normal(ks[1], (BATCH, D), f32),
        "ctx": jax.random.normal(ks[2], (BATCH, CTX_LEN, D), f32),
        "c_ctx": jax.random.normal(ks[3], (D,), f32),
        "w_mod": w(ks[4], (L, D, 6 * D), D, 0.5),
        "b_mod": bias(ks[5], (L, 6 * D)),
        "g_norm1": gain(ks[6], (L, D)),
        "g_norm2": gain(ks[7], (L, D)),
        "w_in": w(ks[8], (L, D, IN_WIDTH), D),
        "g_q_gqa": gain(ks[9], (L, GQA_HEAD_DIM)),
        "g_k_gqa": gain(ks[10], (L, GQA_HEAD_DIM)),
        "g_cq": gain(ks[11], (L, MLA_Q_RANK)),
        "g_ckv": gain(ks[12], (L, MLA_KV_RANK)),
        "w_uq": w(ks[13], (L, MLA_Q_RANK, MLA_UQ_WIDTH), MLA_Q_RANK),
        "w_ukv": w(ks[14], (L, MLA_KV_RANK, MLA_UKV_WIDTH), MLA_KV_RANK),
        "g_q_nope": gain(ks[15], (L, MLA_NOPE_DIM)),
        "g_k_nope": gain(ks[16], (L, MLA_NOPE_DIM)),
        "g_q_rope": gain(ks[17], (L, MLA_ROPE_DIM)),
        "g_k_rope": gain(ks[18], (L, MLA_ROPE_DIM)),
        "b_gate": bias(ks[19], (L, N_BRANCH * D)),
        "w_br_a": w(ks[20], (L, GQA_HEADS * GQA_HEAD_DIM, D), GQA_HEADS * GQA_HEAD_DIM),
        "w_br_b": w(ks[21], (L, MLA_HEADS * MLA_V_DIM, D), MLA_HEADS * MLA_V_DIM),
        "w_br_c": w(ks[22], (L, FOURIER_WIDTH, D), FOURIER_WIDTH),
        "w_out": w(ks[23], (L, D, D), D),
        "w_ffn_in": w(ks[24], (L, D, 2 * FFN_HIDDEN), D),
        "w_ffn_out": w(ks[25], (L, FFN_HIDDEN, D), FFN_HIDDEN),
    }


def reference(x, c, ctx, c_ctx, w_mod, b_mod, g_norm1, g_norm2, w_in, g_q_gqa, g_k_gqa,
              g_cq, g_ckv, w_uq, w_ukv, g_q_nope, g_k_nope, g_q_rope, g_k_rope, b_gate,
              w_br_a, w_br_b, w_br_c, w_out, w_ffn_in, w_ffn_out):
    B, N, D = x.shape
    ROWS = N // GRID_W
    rope_a = axial_rope_tables(ROWS, GQA_HEAD_DIM)
    rope_b = axial_rope_tables(ROWS, MLA_ROPE_DIM)
    silu_c = jax.nn.silu(c)
    silu_cc = jax.nn.silu(c_ctx)
    xc = ctx

    for l in range(DEPTH):
        last = l == DEPTH - 1
        mod = (silu_c @ w_mod[l] + b_mod[l])[:, None, :]
        mod_c = (silu_cc @ w_mod[l] + b_mod[l])[None, None, :]
        sh1, sc1, gt1, sh2, sc2, gt2 = jnp.split(mod, 6, axis=-1)
        csh1, csc1, cgt1, csh2, csc2, cgt2 = jnp.split(mod_c, 6, axis=-1)

        h = modulate(x, g_norm1[l], sh1, sc1)
        hc = modulate(xc, g_norm1[l], csh1, csc1)
        p_q, p_k, p_v, p_cq, p_ckv, p_kr, p_f, p_g = split_in(h @ w_in[l])
        c_q, c_k, c_v, c_cq, c_ckv, c_kr, c_f, c_g = split_in(hc @ w_in[l])

        kc_a, vc_a = gqa_keys_values(c_k, c_v, g_k_gqa[l], None)
        kc_b, vc_b = mla_keys_values(c_ckv, c_kr, g_ckv[l], w_ukv[l], g_k_nope[l], g_k_rope[l], None)

        q_a = gqa_queries(p_q, g_q_gqa[l], rope_a)
        k_a, v_a = gqa_keys_values(p_k, p_v, g_k_gqa[l], rope_a)
        ya = attend(q_a, jnp.concatenate([kc_a, k_a], axis=1),
                    jnp.concatenate([vc_a, v_a], axis=1), GQA_SCALE)
        q_b = mla_queries(p_cq, g_cq[l], w_uq[l], g_q_nope[l], g_q_rope[l], rope_b)
        k_b, v_b = mla_keys_values(p_ckv, p_kr, g_ckv[l], w_ukv[l], g_k_nope[l], g_k_rope[l], rope_b)
        yb = attend(q_b, jnp.concatenate([kc_b, k_b], axis=1),
                    jnp.concatenate([vc_b, v_b], axis=1), MLA_SCALE)
        yc = fourier_mix(p_f)

        x_new = x + gt1 * merge_branches(ya, yb, yc, p_g, b_gate[l], w_br_a[l], w_br_b[l],
                                         w_br_c[l], w_out[l])
        x_new = x_new + gt2 * swiglu(modulate(x_new, g_norm2[l], sh2, sc2), w_ffn_in[l], w_ffn_out[l])

        if not last:
            qc_a = gqa_queries(c_q, g_q_gqa[l], None)
            yca = attend(qc_a, kc_a, vc_a, GQA_SCALE)
            qc_b = mla_queries(c_cq, g_cq[l], w_uq[l], g_q_nope[l], g_q_rope[l], None)
            ycb = attend(qc_b, kc_b, vc_b, MLA_SCALE)
            ycc = fourier_mix(c_f)
            xc = xc + cgt1 * merge_branches(yca, ycb, ycc, c_g, b_gate[l], w_br_a[l], w_br_b[l],
                                            w_br_c[l], w_out[l])
            xc = xc + cgt2 * swiglu(modulate(xc, g_norm2[l], csh2, csc2), w_ffn_in[l], w_ffn_out[l])

        x = x_new

    return x
```

```python
import functools

import numpy as np
import jax
import jax.numpy as jnp
from jax import lax
from jax.experimental import pallas as pl
from jax.experimental.pallas import tpu as pltpu

GRID_W = 64
ROPE_THETA = 10000.0
EPS = 1e-6
GQA_HEADS = 8
GQA_KV_HEADS = 2
GQA_HEAD_DIM = 64
GQA_SCALE = GQA_HEAD_DIM ** -0.5
MLA_HEADS = 8
MLA_Q_RANK = 384
MLA_KV_RANK = 256
MLA_NOPE_DIM = 64
MLA_ROPE_DIM = 32
MLA_V_DIM = 64
MLA_SCALE = (MLA_NOPE_DIM + MLA_ROPE_DIM) ** -0.5
FOURIER_GROUPS = 4
FOURIER_GROUP_DIM = 128
FOURIER_WIDTH = FOURIER_GROUPS * FOURIER_GROUP_DIM
N_BRANCH = 3

LANES = 128
TILE = 256
MOD_ROWS = 16
VMEM_LIMIT = 56 * 1024 * 1024

BF16 = jnp.bfloat16
F32 = jnp.float32

_Q0, _K0, _V0, _CQ0, _CKV0, _F0, _KR0, _WA = 0, 512, 640, 768, 1152, 1408, 1920, 2048


def _cparams(n_axes):
    return pltpu.CompilerParams(dimension_semantics=("arbitrary",) * n_axes, vmem_limit_bytes=VMEM_LIMIT)


def _dot(a, b):
    return jnp.dot(a, b, preferred_element_type=F32)


def _rms_rows(xv, g):
    return xv * lax.rsqrt(jnp.mean(xv * xv, axis=-1, keepdims=True) + EPS) * g


def _mod_kernel(cc_ref, w_ref, b_ref, o_ref):
    s = cc_ref[...]
    s = s * jax.nn.sigmoid(s)
    o_ref[0] = _dot(s.astype(BF16), w_ref[0].astype(BF16)) + b_ref[0]


def _modulation(cc, w_mod, b_mod):
    depth, d, d6 = w_mod.shape
    bn = 1536
    return pl.pallas_call(
        _mod_kernel,
        grid=(depth, d6 // bn),
        in_specs=[pl.BlockSpec((MOD_ROWS, d), lambda l, j: (0, 0)),
                  pl.BlockSpec((1, d, bn), lambda l, j: (l, 0, j)),
                  pl.BlockSpec((1, 1, bn), lambda l, j: (l, 0, j))],
        out_specs=pl.BlockSpec((1, MOD_ROWS, bn), lambda l, j: (l, 0, j)),
        out_shape=jax.ShapeDtypeStruct((depth, MOD_ROWS, d6), F32),
        compiler_params=_cparams(2),
        name="modulation",
    )(cc, w_mod, b_mod.reshape(depth, 1, d6))


def _seg_masks():
    lane = lax.broadcasted_iota(jnp.int32, (1, LANES), 1)
    lo = (lane < 64).astype(F32)
    mid = ((lane >= 64) & (lane < 96)).astype(F32)
    return lane, lo, mid


def _rope(xv, cos, sin_signed, lane, seg):
    first = (lane & (2 * seg - 1)) < seg
    rot = jnp.where(first, pltpu.roll(xv, LANES - seg, 1), pltpu.roll(xv, seg, 1))
    return xv * cos + rot * sin_signed


def _inproj_kernel(x_ref, mod_ref, g1_ref, wa_ref, wuq_ref, wkn_ref, wv_ref, gq_ref, gk_ref, gcq_ref, gckv_ref,
                   gqb_ref, gkn_ref, gkr_ref, cosa_ref, sina_ref, cosb_ref, sinb_ref,
                   qa_ref, ka_ref, va_ref, qb_ref, kb_ref, vb_ref, pf_ref):
    x = x_ref[0]
    m = mod_ref[0]
    h = _rms_rows(x, g1_ref[...]) * (1.0 + m[1:2]) + m[0:1]
    p = _dot(h.astype(BF16), wa_ref[...])
    lane, lo, mid = _seg_masks()
    hi = 1.0 - lo
    cosa, sina, cosb, sinb = cosa_ref[...], sina_ref[...], cosb_ref[...], sinb_ref[...]

    def two_seg_scale(xb, m0, m1, n0, n1):
        x2 = xb * xb
        s0 = jnp.sum(x2 * m0, axis=-1, keepdims=True) * (1.0 / n0)
        s1 = jnp.sum(x2 * m1, axis=-1, keepdims=True) * (1.0 / n1)
        r0 = lax.rsqrt(s0 + EPS)
        r1 = lax.rsqrt(s1 + EPS)
        return r0 * m0 + r1 * m1

    for j in range(GQA_HEADS * GQA_HEAD_DIM // LANES):
        xb = p[:, _Q0 + j * LANES:_Q0 + (j + 1) * LANES]
        xn = xb * two_seg_scale(xb, lo, hi, 64, 64) * gq_ref[...]
        qa_ref[0, :, j * LANES:(j + 1) * LANES] = (_rope(xn, cosa, sina, lane, 16) * GQA_SCALE).astype(BF16)

    xb = p[:, _K0:_K0 + LANES]
    kn = _rope(xb * two_seg_scale(xb, lo, hi, 64, 64) * gk_ref[...], cosa, sina, lane, 16)
    for val, ref in ((kn, ka_ref), (p[:, _V0:_V0 + LANES], va_ref)):
        sw = pltpu.roll(val, 64, 1)
        forms = (val * lo, sw * hi, sw * lo, val * hi)
        for j, f in enumerate(forms):
            ref[0, :, j * LANES:(j + 1) * LANES] = f.astype(BF16)

    cq = _rms_rows(p[:, _CQ0:_CQ0 + MLA_Q_RANK], gcq_ref[...]).astype(BF16)
    qb = _dot(cq, wuq_ref[...])
    for j in range(MLA_HEADS):
        xb = qb[:, j * LANES:(j + 1) * LANES]
        xn = xb * two_seg_scale(xb, lo, mid, MLA_NOPE_DIM, MLA_ROPE_DIM) * gqb_ref[...]
        qb_ref[0, :, j * LANES:(j + 1) * LANES] = (_rope(xn, cosb, sinb, lane, 8) * MLA_SCALE).astype(BF16)

    ckv = _rms_rows(p[:, _CKV0:_CKV0 + MLA_KV_RANK], gckv_ref[...]).astype(BF16)
    xb = p[:, _KR0:_KR0 + LANES]
    x2 = xb * xb
    kr = xb * lax.rsqrt(jnp.sum(x2, axis=-1, keepdims=True) * (1.0 / MLA_ROPE_DIM) + EPS) * gkr_ref[...]
    kr = _rope(kr, cosb, sinb, lane, 8)
    kn_all = _dot(ckv, wkn_ref[...])
    for j in range(MLA_HEADS):
        xb = kn_all[:, j * LANES:(j + 1) * LANES]
        r = lax.rsqrt(jnp.sum(xb * xb, axis=-1, keepdims=True) * (1.0 / MLA_NOPE_DIM) + EPS)
        kb_ref[0, :, j * LANES:(j + 1) * LANES] = (xb * r * gkn_ref[...] + kr).astype(BF16)
    vb_ref[0] = _dot(ckv, wv_ref[...]).astype(BF16)

    pf_ref[0] = p[:, _F0:_F0 + FOURIER_WIDTH].astype(BF16)


def _inproj(xall, mod_l, n_batch, n_lat_tiles, g1, wts, tabs):
    b, t, d = xall.shape
    nt = t // TILE
    row = lambda bi, ti: (bi, ti, 0)
    const2 = lambda bi, ti: (0, 0)
    modmap = lambda bi, ti: (jnp.where(ti >= n_lat_tiles, n_batch, bi), 0, 0)
    tabmap = lambda bi, ti: (ti, 0)
    full = lambda a: pl.BlockSpec(a.shape, const2)
    widths = (512, 512, 512, 1024, 1024, 1024, 512)
    return pl.pallas_call(
        _inproj_kernel,
        grid=(b, nt),
        in_specs=[pl.BlockSpec((1, TILE, d), row), pl.BlockSpec((1, 6, d), modmap), full(g1)]
        + [full(w) for w in wts]
        + [pl.BlockSpec((TILE, LANES), tabmap) for _ in tabs],
        out_specs=[pl.BlockSpec((1, TILE, w), row) for w in widths],
        out_shape=[jax.ShapeDtypeStruct((b, t, w), BF16) for w in widths],
        compiler_params=_cparams(2),
        name="inproj",
    )(xall, mod_l, g1, *wts, *tabs)


def _attn_heads(q_ref, k_ref, v_ref, o_ref, key0, nkeys, qoff, koff):
    for pair in range(4):
        acc = None
        for e in range(2):
            i = 2 * pair + e
            q = q_ref[0, :, qoff[i]:qoff[i] + LANES]
            k = k_ref[0, key0:key0 + nkeys, koff[i]:koff[i] + LANES]
            v = v_ref[0, key0:key0 + nkeys, koff[i]:koff[i] + LANES]
            s = lax.dot_general(q, k, (((1,), (1,)), ((), ())), preferred_element_type=F32)
            mx = jnp.max(s, axis=-1, keepdims=True)
            pr = jnp.exp(s - mx)
            den = jnp.sum(pr, axis=-1, keepdims=True)
            o = _dot(pr.astype(BF16), v) / den
            acc = o if acc is None else acc + o
        o_ref[0, :, pair * LANES:(pair + 1) * LANES] = acc.astype(BF16)


def _attn_kernel(q_ref, k_ref, v_ref, o_ref, *, n_lat_tiles, n_keys, qoff, koff):
    ti = pl.program_id(1)

    @pl.when(ti < n_lat_tiles)
    def _():
        _attn_heads(q_ref, k_ref, v_ref, o_ref, 0, n_keys, qoff, koff)

    @pl.when(ti >= n_lat_tiles)
    def _():
        lat = n_lat_tiles * TILE
        _attn_heads(q_ref, k_ref, v_ref, o_ref, lat, n_keys - lat, qoff, koff)


def _attention(q, k, v, n_lat_tiles, n_q_tiles, qoff, koff, name):
    b, t, wq = q.shape
    wk = k.shape[2]
    kern = functools.partial(_attn_kernel, n_lat_tiles=n_lat_tiles, n_keys=t, qoff=qoff, koff=koff)
    return pl.pallas_call(
        kern,
        grid=(b, n_q_tiles),
        in_specs=[pl.BlockSpec((1, TILE, wq), lambda bi, ti: (bi, ti, 0)),
                  pl.BlockSpec((1, t, wk), lambda bi, ti: (bi, 0, 0)),
                  pl.BlockSpec((1, t, wk), lambda bi, ti: (bi, 0, 0))],
        out_specs=pl.BlockSpec((1, TILE, 512), lambda bi, ti: (bi, ti, 0)),
        out_shape=jax.ShapeDtypeStruct((b, n_q_tiles * TILE, 512), BF16),
        compiler_params=_cparams(2),
        name=name,
    )(q, k, v)


def _fourier_kernel(pf_ref, dn_ref, dc_ref, cs_ref, o_ref, ab_ref, abc_ref, *, n_lat_tiles):
    ti = pl.program_id(1)
    n_lat = n_lat_tiles * TILE
    w = FOURIER_WIDTH

    @pl.when(ti == 0)
    def _():
        for c in range(n_lat_tiles):
            ab = _dot(pf_ref[0, c * TILE:(c + 1) * TILE, :], cs_ref[...])
            ab_ref[c * TILE:(c + 1) * TILE, :] = ab[:, :w].astype(BF16)
            ab_ref[n_lat + c * TILE:n_lat + (c + 1) * TILE, :] = ab[:, w:].astype(BF16)

    @pl.when(ti < n_lat_tiles)
    def _():
        o_ref[0] = _dot(dn_ref[...], ab_ref[...]).astype(BF16)

    @pl.when(ti >= n_lat_tiles)
    def _():
        ab = _dot(pf_ref[0, n_lat:n_lat + TILE, :], cs_ref[...])
        abc_ref[0:TILE, :] = ab[:, :w].astype(BF16)
        abc_ref[TILE:2 * TILE, :] = ab[:, w:].astype(BF16)
        o_ref[0] = _dot(dc_ref[...], abc_ref[...]).astype(BF16)


def _fourier(pf, dn, dc, cs, n_lat_tiles, n_tiles):
    b, t, w = pf.shape
    n_lat = n_lat_tiles * TILE
    kern = functools.partial(_fourier_kernel, n_lat_tiles=n_lat_tiles)
    return pl.pallas_call(
        kern,
        grid=(b, n_tiles),
        in_specs=[pl.BlockSpec((1, t, w), lambda bi, ti: (bi, 0, 0)),
                  pl.BlockSpec((TILE, 2 * n_lat), lambda bi, ti: (jnp.minimum(ti, n_lat_tiles - 1), 0)),
                  pl.BlockSpec(dc.shape, lambda bi, ti: (0, 0)),
                  pl.BlockSpec(cs.shape, lambda bi, ti: (0, 0))],
        out_specs=pl.BlockSpec((1, TILE, w), lambda bi, ti: (bi, ti, 0)),
        out_shape=jax.ShapeDtypeStruct((b, n_tiles * TILE, w), BF16),
        scratch_shapes=[pltpu.VMEM((2 * n_lat, w), BF16), pltpu.VMEM((2 * TILE, w), BF16)],
        compiler_params=_cparams(2),
        name="fourier",
    )(pf, dn, dc, cs)


def _merge_kernel(x_ref, mod_ref, g1_ref, ya_ref, yb_ref, yc_ref, wg_ref, bg_ref, wa_ref, wb_ref, wc_ref, wo_ref,
                  o_ref):
    x = x_ref[0]
    m = mod_ref[0]
    d = x.shape[-1]
    hb = (_rms_rows(x, g1_ref[...]) * (1.0 + m[1:2]) + m[0:1]).astype(BF16)
    acc = None
    for i, (y_ref, w_ref) in enumerate(((ya_ref, wa_ref), (yb_ref, wb_ref), (yc_ref, wc_ref))):
        gate = jax.nn.sigmoid(_dot(hb, wg_ref[:, i * d:(i + 1) * d]) + bg_ref[:, i * d:(i + 1) * d])
        term = gate * _dot(y_ref[0], w_ref[...])
        acc = term if acc is None else acc + term
    o_ref[0] = x + m[2:3] * _dot(acc.astype(BF16), wo_ref[...])


def _merge(xall, mod_l, n_batch, n_lat_tiles, n_tiles, g1, ya, yb, yc, wts):
    b, t, d = xall.shape
    row = lambda bi, ti: (bi, ti, 0)
    modmap = lambda bi, ti: (jnp.where(ti >= n_lat_tiles, n_batch, bi), 0, 0)
    full = lambda a: pl.BlockSpec(a.shape, lambda bi, ti: (0, 0))
    return pl.pallas_call(
        _merge_kernel,
        grid=(b, n_tiles),
        in_specs=[pl.BlockSpec((1, TILE, d), row), pl.BlockSpec((1, 6, d), modmap), full(g1)]
        + [pl.BlockSpec((1, TILE, 512), row)] * 3 + [full(w) for w in wts],
        out_specs=pl.BlockSpec((1, TILE, d), row),
        out_shape=jax.ShapeDtypeStruct((b, n_tiles * TILE, d), F32),
        compiler_params=_cparams(2),
        name="merge",
    )(xall, mod_l, g1, ya, yb, yc, *wts)


def _ffn_kernel(x_ref, mod_ref, g2_ref, wi_ref, wo_ref, o_ref):
    x = x_ref[0]
    m = mod_ref[0]
    hb = (_rms_rows(x, g2_ref[...]) * (1.0 + m[4:5]) + m[3:4]).astype(BF16)
    gu = _dot(hb, wi_ref[...])
    hid = gu.shape[-1] // 2
    gate, up = gu[:, :hid], gu[:, hid:]
    act = (gate * jax.nn.sigmoid(gate) * up).astype(BF16)
    o_ref[0] = x + m[5:6] * _dot(act, wo_ref[...])


def _ffn(xm, mod_l, n_batch, n_lat_tiles, n_tiles, g2, wi, wo):
    b, t, d = xm.shape
    row = lambda bi, ti: (bi, ti, 0)
    modmap = lambda bi, ti: (jnp.where(ti >= n_lat_tiles, n_batch, bi), 0, 0)
    full = lambda a: pl.BlockSpec(a.shape, lambda bi, ti: (0, 0))
    return pl.pallas_call(
        _ffn_kernel,
        grid=(b, n_tiles),
        in_specs=[pl.BlockSpec((1, TILE, d), row), pl.BlockSpec((1, 6, d), modmap), full(g2), full(wi), full(wo)],
        out_specs=pl.BlockSpec((1, TILE, d), row),
        out_shape=jax.ShapeDtypeStruct((b, n_tiles * TILE, d), F32),
        compiler_params=_cparams(2),
        name="ffn",
    )(xm, mod_l, g2, wi, wo)


def _rope_tables(n_lat, n_ctx):
    rows = n_lat // GRID_W
    row_id = np.repeat(np.arange(rows), GRID_W).astype(np.float64)
    col_id = np.tile(np.arange(GRID_W), rows).astype(np.float64)

    def angles(dim):
        half = dim // 2
        freqs = ROPE_THETA ** (-np.arange(0, half, 2, dtype=np.float64) / half)
        ax = lambda pos: np.concatenate([pos[:, None] * freqs[None, :]] * 2, axis=-1)
        return np.concatenate([ax(row_id), ax(col_id)], axis=-1)

    def signed(sin, dim):
        sign = np.where((np.arange(dim) % (dim // 2)) < dim // 4, -1.0, 1.0)
        return sin * sign[None, :]

    a64, a32 = angles(GQA_HEAD_DIM), angles(MLA_ROPE_DIM)
    t = n_lat + n_ctx
    cosa, sina = np.ones((t, LANES)), np.zeros((t, LANES))
    cosa[:n_lat] = np.tile(np.cos(a64), (1, 2))
    sina[:n_lat] = np.tile(signed(np.sin(a64), GQA_HEAD_DIM), (1, 2))
    cosb, sinb = np.ones((t, LANES)), np.zeros((t, LANES))
    cosb[:n_lat, 64:96] = np.cos(a32)
    sinb[:n_lat, 64:96] = signed(np.sin(a32), MLA_ROPE_DIM)
    return [jnp.asarray(a, F32) for a in (cosa, sina, cosb, sinb)]


def _dft_tables(n_lat, n_ctx):
    def cs(n):
        j = np.arange(n)
        ang = 2.0 * np.pi * ((j[:, None] * j[None, :]) % n) / n
        return np.cos(ang) / np.sqrt(n), np.sin(ang) / np.sqrt(n)

    cn, sn = cs(n_lat)
    cc, sc = cs(n_ctx)
    cg, sg = cs(FOURIER_GROUP_DIM)
    eye = np.eye(FOURIER_GROUPS)
    dn = np.concatenate([cn, -sn], axis=1)
    dc = np.concatenate([cc, -sc], axis=1)
    chan = np.concatenate([np.kron(eye, cg), np.kron(eye, sg)], axis=1)
    return [jnp.asarray(a, F32).astype(BF16) for a in (dn, dc, chan)]


def _layer_weights(l, w_in, g_q_gqa, g_k_gqa, g_cq, g_ckv, w_uq, w_ukv, g_q_nope, g_k_nope, g_q_rope, g_k_rope):
    d = w_in.shape[1]
    w = w_in[l]
    q, k, v, cq, ckv, kr, f, gates = (w[:, 0:512], w[:, 512:640], w[:, 640:768], w[:, 768:1152], w[:, 1152:1408],
                                      w[:, 1408:1440], w[:, 1440:1952], w[:, 1952:])
    zeros = lambda *s: jnp.zeros(s, F32)
    krblk = jnp.concatenate([zeros(d, 64), kr, zeros(d, 32)], axis=1)
    wa = jnp.concatenate([q, k, v, cq, ckv, f, krblk], axis=1).astype(BF16)
    uq = w_uq[l].reshape(MLA_Q_RANK, MLA_HEADS, MLA_NOPE_DIM + MLA_ROPE_DIM)
    wuq = jnp.concatenate([uq, zeros(MLA_Q_RANK, MLA_HEADS, 32)], axis=-1).reshape(MLA_Q_RANK, -1).astype(BF16)
    ukv = w_ukv[l].reshape(MLA_KV_RANK, MLA_HEADS, MLA_NOPE_DIM + MLA_V_DIM)
    kn, vv = ukv[..., :MLA_NOPE_DIM], ukv[..., MLA_NOPE_DIM:]
    z64 = zeros(MLA_KV_RANK, MLA_HEADS, 64)
    wkn = jnp.concatenate([kn, z64], axis=-1).reshape(MLA_KV_RANK, -1).astype(BF16)
    even = (jnp.arange(MLA_HEADS) % 2 == 0)[None, :, None]
    wv = jnp.where(even, jnp.concatenate([vv, z64], axis=-1), jnp.concatenate([z64, vv], axis=-1))
    wv = wv.reshape(MLA_KV_RANK, -1).astype(BF16)
    gq = jnp.tile(g_q_gqa[l], 2)[None]
    gk = jnp.tile(g_k_gqa[l], 2)[None]
    gqb = jnp.concatenate([g_q_nope[l], g_q_rope[l], jnp.zeros((32,), F32)])[None]
    gkn = jnp.concatenate([g_k_nope[l], jnp.zeros((64,), F32)])[None]
    gkr = jnp.concatenate([jnp.zeros((64,), F32), g_k_rope[l], jnp.zeros((32,), F32)])[None]
    wts = [wa, wuq, wkn, wv, gq, gk, g_cq[l][None], g_ckv[l][None], gqb, gkn, gkr]
    return wts, gates.astype(BF16)


def kernel(x, c, ctx, c_ctx, w_mod, b_mod, g_norm1, g_norm2, w_in, g_q_gqa, g_k_gqa, g_cq, g_ckv, w_uq, w_ukv,
           g_q_nope, g_k_nope, g_q_rope, g_k_rope, b_gate, w_br_a, w_br_b, w_br_c, w_out, w_ffn_in, w_ffn_out):
    b, n, d = x.shape
    nc = ctx.shape[1]
    depth = w_mod.shape[0]
    assert n % TILE == 0 and nc == TILE and b + 1 <= MOD_ROWS and n % GRID_W == 0
    n_lat_tiles, n_tiles = n // TILE, (n + nc) // TILE

    cc = jnp.concatenate([c, c_ctx[None], jnp.zeros((MOD_ROWS - b - 1, d), F32)], axis=0)
    mod = _modulation(cc, w_mod, b_mod).reshape(depth, MOD_ROWS, 6, d)
    tabs = _rope_tables(n, nc)
    dn, dc, chan = _dft_tables(n, nc)
    gqa_q = (0, 0, 128, 128, 256, 256, 384, 384)
    gqa_k = (0, 128, 0, 128, 256, 384, 256, 384)
    mla_o = tuple(range(0, 1024, 128))

    xall = jnp.concatenate([x, ctx], axis=1)
    for l in range(depth):
        last = l == depth - 1
        n_out = n_lat_tiles if last else n_tiles
        wts, wg = _layer_weights(l, w_in, g_q_gqa, g_k_gqa, g_cq, g_ckv, w_uq, w_ukv, g_q_nope, g_k_nope, g_q_rope,
                                 g_k_rope)
        qa, ka, va, qb, kb, vb, pf = _inproj(xall, mod[l], b, n_lat_tiles, g_norm1[l][None], wts, tabs)
        ya = _attention(qa, ka, va, n_lat_tiles, n_out, gqa_q, gqa_k, "attn_gqa")
        yb = _attention(qb, kb, vb, n_lat_tiles, n_out, mla_o, mla_o, "attn_mla")
        yc = _fourier(pf, dn, dc, chan, n_lat_tiles, n_out)
        mw = [wg, b_gate[l][None], w_br_a[l].astype(BF16), w_br_b[l].astype(BF16), w_br_c[l].astype(BF16),
              w_out[l].astype(BF16)]
        xm = _merge(xall, mod[l], b, n_lat_tiles, n_out, g_norm1[l][None], ya, yb, yc, mw)
        xall = _ffn(xm, mod[l], b, n_lat_tiles, n_out, g_norm2[l][None], w_ffn_in[l].astype(BF16),
                    w_ffn_out[l].astype(BF16))
    return xall
```

```python
import functools
import math

import numpy as np
import jax
import jax.numpy as jnp
from jax import lax
from jax.experimental import pallas as pl
from jax.experimental.pallas import tpu as pltpu

GRID_W = 64
ROPE_THETA = 10000.0
EPS = 1e-6
GQA_HEADS = 8
GQA_KV_HEADS = 2
GQA_HEAD_DIM = 64
GQA_SCALE = GQA_HEAD_DIM ** -0.5
MLA_HEADS = 8
MLA_Q_RANK = 384
MLA_KV_RANK = 256
MLA_NOPE_DIM = 64
MLA_ROPE_DIM = 32
MLA_V_DIM = 64
MLA_SCALE = (MLA_NOPE_DIM + MLA_ROPE_DIM) ** -0.5
FOURIER_GROUPS = 4
FOURIER_GROUP_DIM = 128
FOURIER_WIDTH = FOURIER_GROUPS * FOURIER_GROUP_DIM
N_BRANCH = 3
LOG2E = math.log2(math.e)

LANES = 128
TILE = 256
MOD_ROWS = 16
VMEM_LIMIT = 56 * 1024 * 1024

BF16 = jnp.bfloat16
F32 = jnp.float32

_Q0, _K0, _V0, _CQ0, _KR0, _CKV0, _F0, _WA = 0, 512, 640, 768, 1152, 1280, 1536, 2048


def _cparams(n_axes):
    return pltpu.CompilerParams(dimension_semantics=("arbitrary",) * n_axes, vmem_limit_bytes=VMEM_LIMIT)


def _dot(a, b):
    return jnp.dot(a, b, preferred_element_type=F32)


def _rms_rows(xv, g):
    return xv * lax.rsqrt(jnp.mean(xv * xv, axis=-1, keepdims=True) + EPS) * g


def _modulated(x, g_ref, shift, scale):
    r = lax.rsqrt(jnp.mean(x * x, axis=-1, keepdims=True) + EPS)
    return (x * r * (g_ref[...] * (1.0 + scale)) + shift).astype(BF16)


def _mod_kernel(cc_ref, w_ref, b_ref, o_ref):
    s = cc_ref[...]
    s = s * jax.nn.sigmoid(s)
    o_ref[0] = _dot(s.astype(BF16), w_ref[0].astype(BF16)) + b_ref[0]


def _modulation(cc, w_mod, b_mod):
    depth, d, d6 = w_mod.shape
    bn = 1536
    return pl.pallas_call(
        _mod_kernel,
        grid=(depth, d6 // bn),
        in_specs=[pl.BlockSpec((MOD_ROWS, d), lambda l, j: (0, 0)),
                  pl.BlockSpec((1, d, bn), lambda l, j: (l, 0, j)),
                  pl.BlockSpec((1, 1, bn), lambda l, j: (l, 0, j))],
        out_specs=pl.BlockSpec((1, MOD_ROWS, bn), lambda l, j: (l, 0, j)),
        out_shape=jax.ShapeDtypeStruct((depth, MOD_ROWS, d6), F32),
        compiler_params=_cparams(2),
        name="modulation",
    )(cc, w_mod, b_mod.reshape(depth, 1, d6))


def _rot_half(xv, lane, seg):
    first = (lane & (2 * seg - 1)) < seg
    return jnp.where(first, pltpu.roll(xv, LANES - seg, 1), pltpu.roll(xv, seg, 1))


def _seg_rsqrt(xw, seg_ref):
    return lax.rsqrt(_dot((xw * xw).astype(BF16), seg_ref[...]) + EPS)


def _inproj_kernel(x_ref, mod_ref, g1_ref, wa_ref, wuq_ref, wkn_ref, wvt_ref, gcq_ref, gckv_ref, gkn_ref,
                   sgqa_ref, sqb_ref, skn_ref, skk_ref,
                   cgq_ref, sgq_ref, cgk_ref, sgk_ref, cgqb_ref, sgqb_ref, cgkr_ref, sgkr_ref,
                   qa_ref, ka_ref, vat_ref, qb_ref, kb_ref, vbt_ref, pf_ref):
    m = mod_ref[0]
    hb = _modulated(x_ref[0], g1_ref, m[0:1], m[1:2])
    lane = lax.broadcasted_iota(jnp.int32, (1, LANES), 1)
    lo = lane < 64
    hi = lane >= 64
    blk = lambda a, j: a[:, j * LANES:(j + 1) * LANES]
    n_qblk = GQA_HEADS * GQA_HEAD_DIM // LANES

    p_c = _dot(hb, wa_ref[:, _CQ0:_CKV0])
    p_ckv = _dot(hb, wa_ref[:, _CKV0:_F0])
    pq = _dot(hb, wa_ref[:, _Q0:_K0])
    pkv = _dot(hb, wa_ref[:, _K0:_CQ0])
    pf_ref[0] = _dot(hb, wa_ref[:, _F0:_WA]).astype(BF16)
    cq = _rms_rows(p_c[:, :MLA_Q_RANK], gcq_ref[...]).astype(BF16)
    ckv = _rms_rows(p_ckv, gckv_ref[...])
    qb = _dot(cq, wuq_ref[...])
    kn_all = _dot(ckv.astype(BF16), wkn_ref[...])
    vbt = _dot(wvt_ref[...], ckv.T.astype(BF16))
    for j in range(MLA_HEADS):
        vbt_ref[0, j] = vbt[j * LANES:(j + 1) * LANES].astype(BF16)
    vt = pkv[:, LANES:].T.astype(BF16)
    zero = jnp.zeros((64, TILE), BF16)
    for kvh in range(GQA_KV_HEADS):
        rows = vt[kvh * 64:(kvh + 1) * 64]
        vat_ref[0, 2 * kvh, 0:64] = rows
        vat_ref[0, 2 * kvh, 64:128] = zero
        vat_ref[0, 2 * kvh + 1, 0:64] = zero
        vat_ref[0, 2 * kvh + 1, 64:128] = rows

    w2 = 2 * LANES
    r_qb = jnp.concatenate([_seg_rsqrt(qb[:, c * w2:(c + 1) * w2], sqb_ref) for c in range(MLA_HEADS // 2)], axis=1)
    r_kn = jnp.concatenate([_seg_rsqrt(kn_all[:, c * w2:(c + 1) * w2], skn_ref) for c in range(MLA_HEADS // 2)],
                           axis=1)
    r_q = jnp.concatenate([_seg_rsqrt(pq[:, c * w2:(c + 1) * w2], sgqa_ref) for c in range(n_qblk // 2)], axis=1)
    xkr, xk = p_c[:, MLA_Q_RANK:], pkv[:, :LANES]
    r_kk = _seg_rsqrt(jnp.concatenate([xk, xkr], axis=1), skk_ref)
    r_k, r_kr = r_kk[:, :LANES], r_kk[:, LANES:]

    roped = lambda xb, c_ref, s_ref, seg, r: (xb * c_ref[...] + _rot_half(xb, lane, seg) * s_ref[...]) * r
    for j in range(MLA_HEADS):
        qb_ref[0, :, j * LANES:(j + 1) * LANES] = roped(blk(qb, j), cgqb_ref, sgqb_ref, 8, blk(r_qb, j)).astype(BF16)
    kr = roped(xkr, cgkr_ref, sgkr_ref, 8, r_kr)
    for j in range(MLA_HEADS):
        kb_ref[0, :, j * LANES:(j + 1) * LANES] = (blk(kn_all, j) * blk(r_kn, j) * gkn_ref[...] + kr).astype(BF16)
    for j in range(n_qblk):
        qa_ref[0, :, j * LANES:(j + 1) * LANES] = roped(blk(pq, j), cgq_ref, sgq_ref, 16, blk(r_q, j)).astype(BF16)
    kn = roped(xk, cgk_ref, sgk_ref, 16, r_k)
    sw = pltpu.roll(kn, 64, 1)
    for j, f in enumerate((jnp.where(lo, kn, 0.0), jnp.where(hi, sw, 0.0), jnp.where(lo, sw, 0.0),
                           jnp.where(hi, kn, 0.0))):
        ka_ref[0, :, j * LANES:(j + 1) * LANES] = f.astype(BF16)


def _inproj(xall, mod_l, n_batch, n_lat_tiles, g1, wts, tabs):
    b, t, d = xall.shape
    nt = t // TILE
    row = lambda bi, ti: (bi, ti, 0)
    rowt = lambda bi, ti: (bi, 0, 0, ti)
    modmap = lambda bi, ti: (jnp.where(ti >= n_lat_tiles, n_batch, bi), 0, 0)
    full = lambda a: pl.BlockSpec(a.shape, lambda bi, ti: (0, 0))
    tok = lambda w: (pl.BlockSpec((1, TILE, w), row), jax.ShapeDtypeStruct((b, t, w), BF16))
    tra = lambda h: (pl.BlockSpec((1, h, LANES, TILE), rowt), jax.ShapeDtypeStruct((b, h, LANES, t), BF16))
    outs = [tok(512), tok(512), tra(4), tok(1024), tok(1024), tra(MLA_HEADS), tok(512)]
    return pl.pallas_call(
        _inproj_kernel,
        grid=(b, nt),
        in_specs=[pl.BlockSpec((1, TILE, d), row), pl.BlockSpec((1, 6, d), modmap), full(g1)]
        + [full(w) for w in wts]
        + [pl.BlockSpec((TILE, LANES), lambda bi, ti: (ti, 0)) for _ in tabs],
        out_specs=[o[0] for o in outs],
        out_shape=[o[1] for o in outs],
        compiler_params=_cparams(2),
        name="inproj",
    )(xall, mod_l, g1, *wts, *tabs)


def _reduce_keys(a, op):
    nk, nq = a.shape
    part = op(a.reshape(nk // TILE, TILE, nq), axis=0)
    return op(part, axis=0, keepdims=True)


def _attn_heads(q_ref, k_ref, vt_ref, o_ref, key0, nkeys, qoff, koff, vidx):
    zero = jnp.zeros((TILE, LANES), BF16)

    def scores(pair):
        e, o = 2 * pair, 2 * pair + 1
        assert koff[o] == koff[e] + LANES
        q_e = q_ref[0, :, qoff[e]:qoff[e] + LANES]
        q_o = q_ref[0, :, qoff[o]:qoff[o] + LANES]
        qd = jnp.concatenate([jnp.concatenate([q_e, zero], axis=1), jnp.concatenate([zero, q_o], axis=1)], axis=0)
        k2 = k_ref[0, key0:key0 + nkeys, koff[e]:koff[e] + 2 * LANES]
        return lax.dot_general(k2, qd, (((1,), (1,)), ((), ())), preferred_element_type=F32)

    st_next = scores(0)
    for pair in range(4):
        e, o = 2 * pair, 2 * pair + 1
        st = st_next
        if pair + 1 < 4:
            st_next = scores(pair + 1)
        mx = _reduce_keys(st, jnp.max)
        pt = jnp.exp2(st - mx)
        rden = 1.0 / _reduce_keys(pt, jnp.sum)
        pb = pt.astype(BF16)
        ot = (_dot(vt_ref[0, vidx[e], :, key0:key0 + nkeys], pb[:, :TILE]) * rden[:, :TILE]
              + _dot(vt_ref[0, vidx[o], :, key0:key0 + nkeys], pb[:, TILE:]) * rden[:, TILE:])
        o_ref[0, :, pair * LANES:(pair + 1) * LANES] = ot.T.astype(BF16)


def _attn_kernel(q_ref, k_ref, vt_ref, o_ref, *, n_lat_tiles, n_keys, with_ctx, qoff, koff, vidx):
    if not with_ctx:
        _attn_heads(q_ref, k_ref, vt_ref, o_ref, 0, n_keys, qoff, koff, vidx)
        return
    ti = pl.program_id(1)

    @pl.when(ti < n_lat_tiles)
    def _():
        _attn_heads(q_ref, k_ref, vt_ref, o_ref, 0, n_keys, qoff, koff, vidx)

    @pl.when(ti >= n_lat_tiles)
    def _():
        lat = n_lat_tiles * TILE
        _attn_heads(q_ref, k_ref, vt_ref, o_ref, lat, n_keys - lat, qoff, koff, vidx)


def _attention(q, k, vt, n_lat_tiles, n_q_tiles, qoff, koff, vidx, name):
    b, t, wq = q.shape
    wk = k.shape[2]
    nh = vt.shape[1]
    kern = functools.partial(_attn_kernel, n_lat_tiles=n_lat_tiles, n_keys=t, with_ctx=n_q_tiles > n_lat_tiles,
                             qoff=qoff, koff=koff, vidx=vidx)
    return pl.pallas_call(
        kern,
        grid=(b, n_q_tiles),
        in_specs=[pl.BlockSpec((1, TILE, wq), lambda bi, ti: (bi, ti, 0)),
                  pl.BlockSpec((1, t, wk), lambda bi, ti: (bi, 0, 0)),
                  pl.BlockSpec((1, nh, LANES, t), lambda bi, ti: (bi, 0, 0, 0))],
        out_specs=pl.BlockSpec((1, TILE, 512), lambda bi, ti: (bi, ti, 0)),
        out_shape=jax.ShapeDtypeStruct((b, n_q_tiles * TILE, 512), BF16),
        compiler_params=_cparams(2),
        name=name,
    )(q, k, vt)


def _fourier_kernel(pf_ref, dn_ref, dc_ref, cs_ref, o_ref, ab_ref, abc_ref, *, n_lat_tiles):
    ti = pl.program_id(1)
    n_lat = n_lat_tiles * TILE
    w = FOURIER_WIDTH

    @pl.when(ti == 0)
    def _():
        for c in range(n_lat_tiles):
            ab = _dot(pf_ref[0, c * TILE:(c + 1) * TILE, :], cs_ref[...])
            ab_ref[c * TILE:(c + 1) * TILE, :] = ab[:, :w].astype(BF16)
            ab_ref[n_lat + c * TILE:n_lat + (c + 1) * TILE, :] = ab[:, w:].astype(BF16)

    @pl.when(ti < n_lat_tiles)
    def _():
        o_ref[0] = _dot(dn_ref[...], ab_ref[...]).astype(BF16)

    @pl.when(ti >= n_lat_tiles)
    def _():
        ab = _dot(pf_ref[0, n_lat:n_lat + TILE, :], cs_ref[...])
        abc_ref[0:TILE, :] = ab[:, :w].astype(BF16)
        abc_ref[TILE:2 * TILE, :] = ab[:, w:].astype(BF16)
        o_ref[0] = _dot(dc_ref[...], abc_ref[...]).astype(BF16)


def _fourier(pf, dn, dc, cs, n_lat_tiles, n_tiles):
    b, t, w = pf.shape
    n_lat = n_lat_tiles * TILE
    kern = functools.partial(_fourier_kernel, n_lat_tiles=n_lat_tiles)
    return pl.pallas_call(
        kern,
        grid=(b, n_tiles),
        in_specs=[pl.BlockSpec((1, t, w), lambda bi, ti: (bi, 0, 0)),
                  pl.BlockSpec((TILE, 2 * n_lat), lambda bi, ti: (jnp.minimum(ti, n_lat_tiles - 1), 0)),
                  pl.BlockSpec(dc.shape, lambda bi, ti: (0, 0)),
                  pl.BlockSpec(cs.shape, lambda bi, ti: (0, 0))],
        out_specs=pl.BlockSpec((1, TILE, w), lambda bi, ti: (bi, ti, 0)),
        out_shape=jax.ShapeDtypeStruct((b, n_tiles * TILE, w), BF16),
        scratch_shapes=[pltpu.VMEM((2 * n_lat, w), BF16), pltpu.VMEM((2 * TILE, w), BF16)],
        compiler_params=_cparams(2),
        name="fourier",
    )(pf, dn, dc, cs)


def _merge_kernel(x_ref, mod_ref, g1_ref, ya_ref, yb_ref, yc_ref, wg_ref, bg_ref, wa_ref, wb_ref, wc_ref, wo_ref,
                  o_ref):
    x = x_ref[0]
    m = mod_ref[0]
    d = x.shape[-1]
    hb = _modulated(x, g1_ref, m[0:1], m[1:2])
    acc = None
    for i, (y_ref, w_ref) in enumerate(((ya_ref, wa_ref), (yb_ref, wb_ref), (yc_ref, wc_ref))):
        gate = jax.nn.sigmoid(_dot(hb, wg_ref[:, i * d:(i + 1) * d]) + bg_ref[:, i * d:(i + 1) * d])
        term = gate * _dot(y_ref[0], w_ref[...])
        acc = term if acc is None else acc + term
    o_ref[0] = x + m[2:3] * _dot(acc.astype(BF16), wo_ref[...])


def _merge(xall, mod_l, n_batch, n_lat_tiles, n_tiles, g1, ya, yb, yc, wts):
    b, t, d = xall.shape
    row = lambda bi, ti: (bi, ti, 0)
    modmap = lambda bi, ti: (jnp.where(ti >= n_lat_tiles, n_batch, bi), 0, 0)
    full = lambda a: pl.BlockSpec(a.shape, lambda bi, ti: (0, 0))
    return pl.pallas_call(
        _merge_kernel,
        grid=(b, n_tiles),
        in_specs=[pl.BlockSpec((1, TILE, d), row), pl.BlockSpec((1, 6, d), modmap), full(g1)]
        + [pl.BlockSpec((1, TILE, 512), row)] * 3 + [full(w) for w in wts],
        out_specs=pl.BlockSpec((1, TILE, d), row),
        out_shape=jax.ShapeDtypeStruct((b, n_tiles * TILE, d), F32),
        compiler_params=_cparams(2),
        name="merge",
    )(xall, mod_l, g1, ya, yb, yc, *wts)


def _ffn_kernel(x_ref, mod_ref, g2_ref, wi_ref, wo_ref, o_ref):
    x = x_ref[0]
    m = mod_ref[0]
    hb = _modulated(x, g2_ref, m[3:4], m[4:5])
    gu = _dot(hb, wi_ref[...])
    hid = gu.shape[-1] // 2
    gate, up = gu[:, :hid], gu[:, hid:]
    act = (gate * jax.nn.sigmoid(gate) * up).astype(BF16)
    o_ref[0] = x + m[5:6] * _dot(act, wo_ref[...])


def _ffn(xm, mod_l, n_batch, n_lat_tiles, n_tiles, g2, wi, wo):
    b, t, d = xm.shape
    row = lambda bi, ti: (bi, ti, 0)
    modmap = lambda bi, ti: (jnp.where(ti >= n_lat_tiles, n_batch, bi), 0, 0)
    full = lambda a: pl.BlockSpec(a.shape, lambda bi, ti: (0, 0))
    return pl.pallas_call(
        _ffn_kernel,
        grid=(b, n_tiles),
        in_specs=[pl.BlockSpec((1, TILE, d), row), pl.BlockSpec((1, 6, d), modmap), full(g2), full(wi), full(wo)],
        out_specs=pl.BlockSpec((1, TILE, d), row),
        out_shape=jax.ShapeDtypeStruct((b, n_tiles * TILE, d), F32),
        compiler_params=_cparams(2),
        name="ffn",
    )(xm, mod_l, g2, wi, wo)


def _partner(seg):
    lane = np.arange(LANES)
    return np.where((lane % (2 * seg)) < seg, lane + seg, lane - seg)


def _rope_tables(n_lat, n_ctx):
    rows = n_lat // GRID_W
    row_id = np.repeat(np.arange(rows), GRID_W).astype(np.float64)
    col_id = np.tile(np.arange(GRID_W), rows).astype(np.float64)

    def angles(dim):
        half = dim // 2
        freqs = ROPE_THETA ** (-np.arange(0, half, 2, dtype=np.float64) / half)
        ax = lambda pos: np.concatenate([pos[:, None] * freqs[None, :]] * 2, axis=-1)
        return np.concatenate([ax(row_id), ax(col_id)], axis=-1)

    def signed(sin, dim):
        sign = np.where((np.arange(dim) % (dim // 2)) < dim // 4, -1.0, 1.0)
        return sin * sign[None, :]

    a64, a32 = angles(GQA_HEAD_DIM), angles(MLA_ROPE_DIM)
    t = n_lat + n_ctx
    cosa, sina = np.ones((t, LANES)), np.zeros((t, LANES))
    cosa[:n_lat] = np.tile(np.cos(a64), (1, 2))
    sina[:n_lat] = np.tile(signed(np.sin(a64), GQA_HEAD_DIM), (1, 2))
    cosb, sinb = np.ones((t, LANES)), np.zeros((t, LANES))
    cosb[:n_lat, 64:96] = np.cos(a32)
    sinb[:n_lat, 64:96] = signed(np.sin(a32), MLA_ROPE_DIM)
    return [jnp.asarray(a, F32) for a in (cosa, sina, cosb, sinb)]


def _gained_tables(cos, sin, gain, seg, scale):
    g = gain * scale
    return cos * g[None, :], sin * g[_partner(seg)][None, :]


def _seg_matrices():
    def blockdiag(segs):
        m = np.zeros((2 * LANES, 2 * LANES))
        for start, n in segs:
            m[start:start + n, start:start + n] = 1.0 / n
        return m

    gqa = blockdiag([(s0, 64) for s0 in range(0, 256, 64)])
    qb = blockdiag([(0, 64), (64, 32), (128, 64), (192, 32)])
    kn = blockdiag([(0, 64), (128, 64)])
    kk = blockdiag([(0, 64), (64, 64), (192, 32)])
    return [jnp.asarray(a, F32).astype(BF16) for a in (gqa, qb, kn, kk)]


def _dft_tables(n_lat, n_ctx):
    def cs(n):
        j = np.arange(n)
        ang = 2.0 * np.pi * ((j[:, None] * j[None, :]) % n) / n
        return np.cos(ang) / np.sqrt(n), np.sin(ang) / np.sqrt(n)

    cn, sn = cs(n_lat)
    cc, sc = cs(n_ctx)
    cg, sg = cs(FOURIER_GROUP_DIM)
    eye = np.eye(FOURIER_GROUPS)
    dn = np.concatenate([cn, -sn], axis=1)
    dc = np.concatenate([cc, -sc], axis=1)
    chan = np.concatenate([np.kron(eye, cg), np.kron(eye, sg)], axis=1)
    return [jnp.asarray(a, F32).astype(BF16) for a in (dn, dc, chan)]


def _layer_weights(l, rope, w_in, g_q_gqa, g_k_gqa, g_cq, g_ckv, w_uq, w_ukv, g_q_nope, g_k_nope, g_q_rope,
                   g_k_rope):
    d = w_in.shape[1]
    w = w_in[l]
    q, k, v, cq, ckv, kr, f, gates = (w[:, 0:512], w[:, 512:640], w[:, 640:768], w[:, 768:1152], w[:, 1152:1408],
                                      w[:, 1408:1440], w[:, 1440:1952], w[:, 1952:])
    zeros = lambda *s: jnp.zeros(s, F32)
    krblk = jnp.concatenate([zeros(d, 64), kr, zeros(d, 32)], axis=1)
    wa = jnp.concatenate([q, k, v, cq, krblk, ckv, f], axis=1).astype(BF16)
    uq = w_uq[l].reshape(MLA_Q_RANK, MLA_HEADS, MLA_NOPE_DIM + MLA_ROPE_DIM)
    wuq = jnp.concatenate([uq, zeros(MLA_Q_RANK, MLA_HEADS, 32)], axis=-1).reshape(MLA_Q_RANK, -1).astype(BF16)
    ukv = w_ukv[l].reshape(MLA_KV_RANK, MLA_HEADS, MLA_NOPE_DIM + MLA_V_DIM)
    kn, vv = ukv[..., :MLA_NOPE_DIM], ukv[..., MLA_NOPE_DIM:]
    z64 = zeros(MLA_KV_RANK, MLA_HEADS, 64)
    wkn = jnp.concatenate([kn, z64], axis=-1).reshape(MLA_KV_RANK, -1).astype(BF16)
    even = (jnp.arange(MLA_HEADS) % 2 == 0)[None, :, None]
    wv = jnp.where(even, jnp.concatenate([vv, z64], axis=-1), jnp.concatenate([z64, vv], axis=-1))
    wvt = wv.reshape(MLA_KV_RANK, -1).T.astype(BF16)
    gkn = jnp.concatenate([g_k_nope[l], jnp.zeros((64,), F32)])[None]
    cosa, sina, cosb, sinb = rope
    z32 = jnp.zeros((32,), F32)
    z64v = jnp.zeros((64,), F32)
    tabs = (_gained_tables(cosa, sina, jnp.tile(g_q_gqa[l], 2), 16, GQA_SCALE * LOG2E)
            + _gained_tables(cosa, sina, jnp.tile(g_k_gqa[l], 2), 16, 1.0)
            + _gained_tables(cosb, sinb, jnp.concatenate([g_q_nope[l], g_q_rope[l], z32]), 8, MLA_SCALE * LOG2E)
            + _gained_tables(cosb, sinb, jnp.concatenate([z64v, g_k_rope[l], z32]), 8, 1.0))
    wts = [wa, wuq, wkn, wvt, g_cq[l][None], g_ckv[l][None], gkn] + _seg_matrices()
    return wts, list(tabs), gates.astype(BF16)


def kernel(x, c, ctx, c_ctx, w_mod, b_mod, g_norm1, g_norm2, w_in, g_q_gqa, g_k_gqa, g_cq, g_ckv, w_uq, w_ukv,
           g_q_nope, g_k_nope, g_q_rope, g_k_rope, b_gate, w_br_a, w_br_b, w_br_c, w_out, w_ffn_in, w_ffn_out):
    b, n, d = x.shape
    nc = ctx.shape[1]
    depth = w_mod.shape[0]
    assert n % TILE == 0 and nc == TILE and b + 1 <= MOD_ROWS and n % GRID_W == 0
    n_lat_tiles, n_tiles = n // TILE, (n + nc) // TILE

    cc = jnp.concatenate([c, c_ctx[None], jnp.zeros((MOD_ROWS - b - 1, d), F32)], axis=0)
    mod = _modulation(cc, w_mod, b_mod).reshape(depth, MOD_ROWS, 6, d)
    rope = _rope_tables(n, nc)
    dn, dc, chan = _dft_tables(n, nc)
    gqa_q = (0, 0, 128, 128, 256, 256, 384, 384)
    gqa_k = (0, 128, 0, 128, 256, 384, 256, 384)
    gqa_v = (0, 1, 0, 1, 2, 3, 2, 3)
    mla_o = tuple(range(0, 1024, 128))

    xall = jnp.concatenate([x, ctx], axis=1)
    for l in range(depth):
        last = l == depth - 1
        n_out = n_lat_tiles if last else n_tiles
        wts, tabs, wg = _layer_weights(l, rope, w_in, g_q_gqa, g_k_gqa, g_cq, g_ckv, w_uq, w_ukv, g_q_nope,
                                       g_k_nope, g_q_rope, g_k_rope)
        qa, ka, vat, qb, kb, vbt, pf = _inproj(xall, mod[l], b, n_lat_tiles, g_norm1[l][None], wts, tabs)
        ya = _attention(qa, ka, vat, n_lat_tiles, n_out, gqa_q, gqa_k, gqa_v, "attn_gqa")
        yb = _attention(qb, kb, vbt, n_lat_tiles, n_out, mla_o, mla_o, tuple(range(MLA_HEADS)), "attn_mla")
        yc = _fourier(pf, dn, dc, chan, n_lat_tiles, n_out)
        mw = [wg, b_gate[l][None], w_br_a[l].astype(BF16), w_br_b[l].astype(BF16), w_br_c[l].astype(BF16),
              w_out[l].astype(BF16)]
        xm = _merge(xall, mod[l], b, n_lat_tiles, n_out, g_norm1[l][None], ya, yb, yc, mw)
        xall = _ffn(xm, mod[l], b, n_lat_tiles, n_out, g_norm2[l][None], w_ffn_in[l].astype(BF16),
                    w_ffn_out[l].astype(BF16))
    return xall
```

```python
import functools
import math

import numpy as np
import jax
import jax.numpy as jnp
from jax import lax
from jax.experimental import pallas as pl
from jax.experimental.pallas import tpu as pltpu

GRID_W = 64
ROPE_THETA = 10000.0
EPS = 1e-6
GQA_HEADS = 8
GQA_KV_HEADS = 2
GQA_HEAD_DIM = 64
GQA_SCALE = GQA_HEAD_DIM ** -0.5
MLA_HEADS = 8
MLA_Q_RANK = 384
MLA_KV_RANK = 256
MLA_NOPE_DIM = 64
MLA_ROPE_DIM = 32
MLA_V_DIM = 64
MLA_SCALE = (MLA_NOPE_DIM + MLA_ROPE_DIM) ** -0.5
FOURIER_GROUPS = 4
FOURIER_GROUP_DIM = 128
FOURIER_WIDTH = FOURIER_GROUPS * FOURIER_GROUP_DIM
N_BRANCH = 3
LOG2E = math.log2(math.e)

LANES = 128
TILE = 256
MOD_ROWS = 16
VMEM_LIMIT = 56 * 1024 * 1024

BF16 = jnp.bfloat16
F32 = jnp.float32

_Q0, _K0, _V0, _CQ0, _KR0, _CKV0, _F0, _WA = 0, 512, 640, 768, 1152, 1280, 1536, 2048


def _cparams(n_axes):
    return pltpu.CompilerParams(dimension_semantics=("arbitrary",) * n_axes, vmem_limit_bytes=VMEM_LIMIT)


def _dot(a, b):
    return jnp.dot(a, b, preferred_element_type=F32)


def _rms_rows(xv, g):
    return xv * lax.rsqrt(jnp.mean(xv * xv, axis=-1, keepdims=True) + EPS) * g


def _modulated(x, g, shift, scale):
    r = lax.rsqrt(jnp.mean(x * x, axis=-1, keepdims=True) + EPS)
    return (x * r * (g * (1.0 + scale)) + shift).astype(BF16)


def _layer_spec(a, l):
    return pl.BlockSpec((1,) + a.shape[1:], lambda *_: (l,) + (0,) * (a.ndim - 1))


def _const_spec(a):
    return pl.BlockSpec(a.shape, lambda *_: (0,) * a.ndim)


def _row_specs(x_lat, x_ctx, ctx_tile, n_lat_tiles):
    d = x_lat.shape[-1]
    return [pl.BlockSpec((1, TILE, d), lambda bi, ti: (bi, jnp.minimum(ti, n_lat_tiles - 1), 0)),
            pl.BlockSpec((1, TILE, d), lambda bi, ti: (bi, ctx_tile, 0))]


def _mod_spec(mod, l, n_batch, n_lat_tiles):
    return pl.BlockSpec((1, 1) + mod.shape[2:], lambda bi, ti: (l, jnp.where(ti >= n_lat_tiles, n_batch, bi), 0, 0))


def _tile_rows(xl_ref, xc_ref, n_lat_tiles):
    return jnp.where(pl.program_id(1) < n_lat_tiles, xl_ref[0], xc_ref[0])


def _mod_kernel(cc_ref, w_ref, b_ref, o_ref):
    s = cc_ref[...]
    s = s * jax.nn.sigmoid(s)
    o_ref[0] = _dot(s.astype(BF16), w_ref[0].astype(BF16)) + b_ref[0]


def _modulation(cc, w_mod, b_mod):
    depth, d, d6 = w_mod.shape
    bn = 1536
    return pl.pallas_call(
        _mod_kernel,
        grid=(depth, d6 // bn),
        in_specs=[pl.BlockSpec((MOD_ROWS, d), lambda l, j: (0, 0)),
                  pl.BlockSpec((1, d, bn), lambda l, j: (l, 0, j)),
                  pl.BlockSpec((1, 1, bn), lambda l, j: (l, 0, j))],
        out_specs=pl.BlockSpec((1, MOD_ROWS, bn), lambda l, j: (l, 0, j)),
        out_shape=jax.ShapeDtypeStruct((depth, MOD_ROWS, d6), F32),
        compiler_params=_cparams(2),
        name="modulation",
    )(cc, w_mod, b_mod.reshape(depth, 1, d6))


def _rot_half(xv, lane, seg):
    first = (lane & (2 * seg - 1)) < seg
    return jnp.where(first, pltpu.roll(xv, LANES - seg, 1), pltpu.roll(xv, seg, 1))


def _seg_rsqrt(xw, seg_ref):
    return lax.rsqrt(_dot((xw * xw).astype(BF16), seg_ref[...]) + EPS)


def _inproj_kernel(xl_ref, xc_ref, mod_ref, g1_ref, wa_ref, wuq_ref, wkn_ref, wvt_ref, gcq_ref, gckv_ref, gkn_ref,
                   cgq_ref, sgq_ref, cgk_ref, sgk_ref, cgqb_ref, sgqb_ref, cgkr_ref, sgkr_ref,
                   sgqa_ref, sqb_ref, skn_ref, skk_ref,
                   qa_ref, ka_ref, vat_ref, qb_ref, kb_ref, vbt_ref, pf_ref, *, n_lat_tiles):
    m = mod_ref[0, 0]
    hb = _modulated(_tile_rows(xl_ref, xc_ref, n_lat_tiles), g1_ref[0], m[0:1], m[1:2])
    lane = lax.broadcasted_iota(jnp.int32, (1, LANES), 1)
    lo = lane < 64
    hi = lane >= 64
    blk = lambda a, j: a[:, j * LANES:(j + 1) * LANES]
    n_qblk = GQA_HEADS * GQA_HEAD_DIM // LANES

    p_c = _dot(hb, wa_ref[0, :, _CQ0:_CKV0])
    p_ckv = _dot(hb, wa_ref[0, :, _CKV0:_F0])
    pq = _dot(hb, wa_ref[0, :, _Q0:_K0])
    pkv = _dot(hb, wa_ref[0, :, _K0:_CQ0])
    pf_ref[0] = _dot(hb, wa_ref[0, :, _F0:_WA]).astype(BF16)
    cq = _rms_rows(p_c[:, :MLA_Q_RANK], gcq_ref[0]).astype(BF16)
    ckv = _rms_rows(p_ckv, gckv_ref[0])
    qb = _dot(cq, wuq_ref[0])
    kn_all = _dot(ckv.astype(BF16), wkn_ref[0])
    vbt = _dot(wvt_ref[0], ckv.T.astype(BF16))
    for j in range(MLA_HEADS):
        vbt_ref[0, j] = vbt[j * LANES:(j + 1) * LANES].astype(BF16)
    vt = pkv[:, LANES:].T.astype(BF16)
    zero = jnp.zeros((64, TILE), BF16)
    for kvh in range(GQA_KV_HEADS):
        rows = vt[kvh * 64:(kvh + 1) * 64]
        vat_ref[0, 2 * kvh, 0:64] = rows
        vat_ref[0, 2 * kvh, 64:128] = zero
        vat_ref[0, 2 * kvh + 1, 0:64] = zero
        vat_ref[0, 2 * kvh + 1, 64:128] = rows

    w2 = 2 * LANES
    r_qb = jnp.concatenate([_seg_rsqrt(qb[:, c * w2:(c + 1) * w2], sqb_ref) for c in range(MLA_HEADS // 2)], axis=1)
    r_kn = jnp.concatenate([_seg_rsqrt(kn_all[:, c * w2:(c + 1) * w2], skn_ref) for c in range(MLA_HEADS // 2)],
                           axis=1)
    r_q = jnp.concatenate([_seg_rsqrt(pq[:, c * w2:(c + 1) * w2], sgqa_ref) for c in range(n_qblk // 2)], axis=1)
    xkr, xk = p_c[:, MLA_Q_RANK:], pkv[:, :LANES]
    r_kk = _seg_rsqrt(jnp.concatenate([xk, xkr], axis=1), skk_ref)
    r_k, r_kr = r_kk[:, :LANES], r_kk[:, LANES:]

    roped = lambda xb, c_ref, s_ref, seg, r: (xb * c_ref[0] + _rot_half(xb, lane, seg) * s_ref[0]) * r
    for j in range(MLA_HEADS):
        qb_ref[0, :, j * LANES:(j + 1) * LANES] = roped(blk(qb, j), cgqb_ref, sgqb_ref, 8, blk(r_qb, j)).astype(BF16)
    kr = roped(xkr, cgkr_ref, sgkr_ref, 8, r_kr)
    for j in range(MLA_HEADS):
        kb_ref[0, :, j * LANES:(j + 1) * LANES] = (blk(kn_all, j) * blk(r_kn, j) * gkn_ref[0] + kr).astype(BF16)
    for j in range(n_qblk):
        qa_ref[0, :, j * LANES:(j + 1) * LANES] = roped(blk(pq, j), cgq_ref, sgq_ref, 16, blk(r_q, j)).astype(BF16)
    kn = roped(xk, cgk_ref, sgk_ref, 16, r_k)
    sw = pltpu.roll(kn, 64, 1)
    for j, f in enumerate((jnp.where(lo, kn, 0.0), jnp.where(hi, sw, 0.0), jnp.where(lo, sw, 0.0),
                           jnp.where(hi, kn, 0.0))):
        ka_ref[0, :, j * LANES:(j + 1) * LANES] = f.astype(BF16)


def _inproj(x_lat, x_ctx, ctx_tile, mod, l, n_lat_tiles, g1, wts, tabs, segs):
    b = x_lat.shape[0]
    nt = n_lat_tiles + 1
    t = nt * TILE
    row = lambda bi, ti: (bi, ti, 0)
    rowt = lambda bi, ti: (bi, 0, 0, ti)
    tok = lambda w: (pl.BlockSpec((1, TILE, w), row), jax.ShapeDtypeStruct((b, t, w), BF16))
    tra = lambda h: (pl.BlockSpec((1, h, LANES, TILE), rowt), jax.ShapeDtypeStruct((b, h, LANES, t), BF16))
    outs = [tok(512), tok(512), tra(4), tok(1024), tok(1024), tra(MLA_HEADS), tok(512)]
    return pl.pallas_call(
        functools.partial(_inproj_kernel, n_lat_tiles=n_lat_tiles),
        grid=(b, nt),
        in_specs=_row_specs(x_lat, x_ctx, ctx_tile, n_lat_tiles)
        + [_mod_spec(mod, l, b, n_lat_tiles), _layer_spec(g1, l)]
        + [_layer_spec(w, l) for w in wts]
        + [pl.BlockSpec((1, TILE, LANES), lambda bi, ti: (l, ti, 0)) for _ in tabs]
        + [_const_spec(s) for s in segs],
        out_specs=[o[0] for o in outs],
        out_shape=[o[1] for o in outs],
        compiler_params=_cparams(2),
        name="inproj",
    )(x_lat, x_ctx, mod, g1, *wts, *tabs, *segs)


def _reduce_keys(a, op):
    nk, nq = a.shape
    part = op(a.reshape(nk // TILE, TILE, nq), axis=0)
    return op(part, axis=0, keepdims=True)


def _attn_kernel(q_ref, k_ref, vt_ref, o_ref, st_ref, *, n_lat_tiles, with_ctx, qoff, koff, vidx):
    n_keys = k_ref.shape[1]
    n_pairs = len(qoff) // 2
    zero = jnp.zeros((TILE, LANES), BF16)

    def scores(row0, pair, key0, nk):
        e, o = 2 * pair, 2 * pair + 1
        assert koff[o] == koff[e] + LANES
        q_e = q_ref[0, pl.ds(row0, TILE), qoff[e]:qoff[e] + LANES]
        q_o = q_ref[0, pl.ds(row0, TILE), qoff[o]:qoff[o] + LANES]
        qd = jnp.concatenate([jnp.concatenate([q_e, zero], axis=1), jnp.concatenate([zero, q_o], axis=1)], axis=0)
        k2 = k_ref[0, key0:key0 + nk, koff[e]:koff[e] + 2 * LANES]
        return lax.dot_general(k2, qd, (((1,), (1,)), ((), ())), preferred_element_type=F32)

    def finish(st, row0, pair, key0, nk):
        e, o = 2 * pair, 2 * pair + 1
        pt = jnp.exp2(st - _reduce_keys(st, jnp.max))
        rden = 1.0 / _reduce_keys(pt, jnp.sum)
        pb = pt.astype(BF16)
        ot = (_dot(vt_ref[0, vidx[e], :, key0:key0 + nk], pb[:, :TILE]) * rden[:, :TILE]
              + _dot(vt_ref[0, vidx[o], :, key0:key0 + nk], pb[:, TILE:]) * rden[:, TILE:])
        o_ref[0, pl.ds(row0, TILE), pair * LANES:(pair + 1) * LANES] = ot.T.astype(BF16)

    st_ref[...] = scores(0, 0, 0, n_keys)

    def tile_body(t, carry):
        row0 = pl.multiple_of(t * TILE, TILE)
        nxt = pl.multiple_of(jnp.minimum(t + 1, n_lat_tiles - 1) * TILE, TILE)
        st = st_ref[...]
        for pair in range(n_pairs):
            st_next = scores(row0, pair + 1, 0, n_keys) if pair + 1 < n_pairs else scores(nxt, 0, 0, n_keys)
            finish(st, row0, pair, 0, n_keys)
            st = st_next
        st_ref[...] = st
        return carry

    lax.fori_loop(0, n_lat_tiles, tile_body, 0)

    if with_ctx:
        lat = n_lat_tiles * TILE
        nk = n_keys - lat
        st = scores(lat, 0, lat, nk)
        for pair in range(n_pairs):
            st_next = scores(lat, pair + 1, lat, nk) if pair + 1 < n_pairs else None
            finish(st, lat, pair, lat, nk)
            st = st_next


def _attention(q, k, vt, n_lat_tiles, n_q_tiles, qoff, koff, vidx, name):
    b, t, wq = q.shape
    wk = k.shape[2]
    nh = vt.shape[1]
    kern = functools.partial(_attn_kernel, n_lat_tiles=n_lat_tiles, with_ctx=n_q_tiles > n_lat_tiles,
                             qoff=qoff, koff=koff, vidx=vidx)
    return pl.pallas_call(
        kern,
        grid=(b,),
        in_specs=[pl.BlockSpec((1, t, wq), lambda bi: (bi, 0, 0)),
                  pl.BlockSpec((1, t, wk), lambda bi: (bi, 0, 0)),
                  pl.BlockSpec((1, nh, LANES, t), lambda bi: (bi, 0, 0, 0))],
        out_specs=pl.BlockSpec((1, n_q_tiles * TILE, 512), lambda bi: (bi, 0, 0)),
        out_shape=jax.ShapeDtypeStruct((b, n_q_tiles * TILE, 512), BF16),
        scratch_shapes=[pltpu.VMEM((t, 2 * TILE), F32)],
        compiler_params=_cparams(1),
        name=name,
    )(q, k, vt)


def _fourier_kernel(pf_ref, dn_ref, dc_ref, cs_ref, o_ref, ab_ref, abc_ref, *, n_lat_tiles):
    ti = pl.program_id(1)
    n_lat = n_lat_tiles * TILE
    w = FOURIER_WIDTH

    @pl.when(ti == 0)
    def _():
        for c in range(n_lat_tiles):
            ab = _dot(pf_ref[0, c * TILE:(c + 1) * TILE, :], cs_ref[...])
            ab_ref[c * TILE:(c + 1) * TILE, :] = ab[:, :w].astype(BF16)
            ab_ref[n_lat + c * TILE:n_lat + (c + 1) * TILE, :] = ab[:, w:].astype(BF16)

    @pl.when(ti < n_lat_tiles)
    def _():
        o_ref[0] = _dot(dn_ref[...], ab_ref[...]).astype(BF16)

    @pl.when(ti >= n_lat_tiles)
    def _():
        ab = _dot(pf_ref[0, n_lat:n_lat + TILE, :], cs_ref[...])
        abc_ref[0:TILE, :] = ab[:, :w].astype(BF16)
        abc_ref[TILE:2 * TILE, :] = ab[:, w:].astype(BF16)
        o_ref[0] = _dot(dc_ref[...], abc_ref[...]).astype(BF16)


def _fourier(pf, dn, dc, cs, n_lat_tiles, n_tiles):
    b, t, w = pf.shape
    n_lat = n_lat_tiles * TILE
    kern = functools.partial(_fourier_kernel, n_lat_tiles=n_lat_tiles)
    return pl.pallas_call(
        kern,
        grid=(b, n_tiles),
        in_specs=[pl.BlockSpec((1, t, w), lambda bi, ti: (bi, 0, 0)),
                  pl.BlockSpec((TILE, 2 * n_lat), lambda bi, ti: (jnp.minimum(ti, n_lat_tiles - 1), 0)),
                  _const_spec(dc), _const_spec(cs)],
        out_specs=pl.BlockSpec((1, TILE, w), lambda bi, ti: (bi, ti, 0)),
        out_shape=jax.ShapeDtypeStruct((b, n_tiles * TILE, w), BF16),
        scratch_shapes=[pltpu.VMEM((2 * n_lat, w), BF16), pltpu.VMEM((2 * TILE, w), BF16)],
        compiler_params=_cparams(2),
        name="fourier",
    )(pf, dn, dc, cs)


def _merge_kernel(xl_ref, xc_ref, mod_ref, g1_ref, ya_ref, yb_ref, yc_ref, wg_ref, bg_ref, wa_ref, wb_ref, wc_ref,
                  wo_ref, o_ref, *, n_lat_tiles):
    x = _tile_rows(xl_ref, xc_ref, n_lat_tiles)
    m = mod_ref[0, 0]
    d = x.shape[-1]
    hb = _modulated(x, g1_ref[0], m[0:1], m[1:2])
    acc = None
    for i, (y_ref, w_ref) in enumerate(((ya_ref, wa_ref), (yb_ref, wb_ref), (yc_ref, wc_ref))):
        gate = jax.nn.sigmoid(_dot(hb, wg_ref[0, :, i * d:(i + 1) * d]) + bg_ref[0, :, i * d:(i + 1) * d])
        term = gate * _dot(y_ref[0], w_ref[0])
        acc = term if acc is None else acc + term
    o_ref[0] = x + m[2:3] * _dot(acc.astype(BF16), wo_ref[0])


def _merge(x_lat, x_ctx, ctx_tile, mod, l, n_lat_tiles, n_tiles, g1, ya, yb, yc, wts):
    b, _, d = x_lat.shape
    row = lambda bi, ti: (bi, ti, 0)
    return pl.pallas_call(
        functools.partial(_merge_kernel, n_lat_tiles=n_lat_tiles),
        grid=(b, n_tiles),
        in_specs=_row_specs(x_lat, x_ctx, ctx_tile, n_lat_tiles)
        + [_mod_spec(mod, l, b, n_lat_tiles), _layer_spec(g1, l)]
        + [pl.BlockSpec((1, TILE, 512), row)] * 3 + [_layer_spec(w, l) for w in wts],
        out_specs=pl.BlockSpec((1, TILE, d), row),
        out_shape=jax.ShapeDtypeStruct((b, n_tiles * TILE, d), F32),
        compiler_params=_cparams(2),
        name="merge",
    )(x_lat, x_ctx, mod, g1, ya, yb, yc, *wts)


def _ffn_kernel(x_ref, mod_ref, g2_ref, wi_ref, wo_ref, o_ref):
    x = x_ref[0]
    m = mod_ref[0, 0]
    hb = _modulated(x, g2_ref[0], m[3:4], m[4:5])
    gu = _dot(hb, wi_ref[0])
    hid = gu.shape[-1] // 2
    gate, up = gu[:, :hid], gu[:, hid:]
    act = (gate * jax.nn.sigmoid(gate) * up).astype(BF16)
    o_ref[0] = x + m[5:6] * _dot(act, wo_ref[0])


def _ffn(xm, mod, l, n_lat_tiles, n_tiles, g2, wi, wo):
    b, _, d = xm.shape
    row = lambda bi, ti: (bi, ti, 0)
    return pl.pallas_call(
        _ffn_kernel,
        grid=(b, n_tiles),
        in_specs=[pl.BlockSpec((1, TILE, d), row), _mod_spec(mod, l, b, n_lat_tiles), _layer_spec(g2, l),
                  _layer_spec(wi, l), _layer_spec(wo, l)],
        out_specs=pl.BlockSpec((1, TILE, d), row),
        out_shape=jax.ShapeDtypeStruct((b, n_tiles * TILE, d), F32),
        compiler_params=_cparams(2),
        name="ffn",
    )(xm, mod, g2, wi, wo)


def _partner(seg):
    lane = np.arange(LANES)
    return np.where((lane % (2 * seg)) < seg, lane + seg, lane - seg)


def _rope_tables(n_lat, n_ctx):
    rows = n_lat // GRID_W
    row_id = np.repeat(np.arange(rows), GRID_W).astype(np.float64)
    col_id = np.tile(np.arange(GRID_W), rows).astype(np.float64)

    def angles(dim):
        half = dim // 2
        freqs = ROPE_THETA ** (-np.arange(0, half, 2, dtype=np.float64) / half)
        ax = lambda pos: np.concatenate([pos[:, None] * freqs[None, :]] * 2, axis=-1)
        return np.concatenate([ax(row_id), ax(col_id)], axis=-1)

    def signed(sin, dim):
        sign = np.where((np.arange(dim) % (dim // 2)) < dim // 4, -1.0, 1.0)
        return sin * sign[None, :]

    a64, a32 = angles(GQA_HEAD_DIM), angles(MLA_ROPE_DIM)
    t = n_lat + n_ctx
    cosa, sina = np.ones((t, LANES)), np.zeros((t, LANES))
    cosa[:n_lat] = np.tile(np.cos(a64), (1, 2))
    sina[:n_lat] = np.tile(signed(np.sin(a64), GQA_HEAD_DIM), (1, 2))
    cosb, sinb = np.ones((t, LANES)), np.zeros((t, LANES))
    cosb[:n_lat, 64:96] = np.cos(a32)
    sinb[:n_lat, 64:96] = signed(np.sin(a32), MLA_ROPE_DIM)
    return [jnp.asarray(a, F32) for a in (cosa, sina, cosb, sinb)]


def _gained_tables(cos, sin, gain, seg, scale):
    g = gain * scale
    return [cos[None] * g[:, None, :], sin[None] * g[:, _partner(seg)][:, None, :]]


def _seg_matrices():
    def blockdiag(segs):
        m = np.zeros((2 * LANES, 2 * LANES))
        for start, n in segs:
            m[start:start + n, start:start + n] = 1.0 / n
        return m

    gqa = blockdiag([(s0, 64) for s0 in range(0, 256, 64)])
    qb = blockdiag([(0, 64), (64, 32), (128, 64), (192, 32)])
    kn = blockdiag([(0, 64), (128, 64)])
    kk = blockdiag([(0, 64), (64, 64), (192, 32)])
    return [jnp.asarray(a, F32).astype(BF16) for a in (gqa, qb, kn, kk)]


def _dft_tables(n_lat, n_ctx):
    def cs(n):
        j = np.arange(n)
        ang = 2.0 * np.pi * ((j[:, None] * j[None, :]) % n) / n
        return np.cos(ang) / np.sqrt(n), np.sin(ang) / np.sqrt(n)

    cn, sn = cs(n_lat)
    cc, sc = cs(n_ctx)
    cg, sg = cs(FOURIER_GROUP_DIM)
    eye = np.eye(FOURIER_GROUPS)
    dn = np.concatenate([cn, -sn], axis=1)
    dc = np.concatenate([cc, -sc], axis=1)
    chan = np.concatenate([np.kron(eye, cg), np.kron(eye, sg)], axis=1)
    return [jnp.asarray(a, F32).astype(BF16) for a in (dn, dc, chan)]


def _inproj_weights(rope, w_in, g_q_gqa, g_k_gqa, g_cq, g_ckv, w_uq, w_ukv, g_q_nope, g_k_nope, g_q_rope, g_k_rope):
    depth, d, _ = w_in.shape
    q, k, v, cq, ckv, kr, f, gates = (w_in[..., 0:512], w_in[..., 512:640], w_in[..., 640:768], w_in[..., 768:1152],
                                      w_in[..., 1152:1408], w_in[..., 1408:1440], w_in[..., 1440:1952],
                                      w_in[..., 1952:])
    zeros = lambda *s: jnp.zeros((depth,) + s, F32)
    krblk = jnp.concatenate([zeros(d, 64), kr, zeros(d, 32)], axis=-1)
    wa = jnp.concatenate([q, k, v, cq, krblk, ckv, f], axis=-1).astype(BF16)
    uq = w_uq.reshape(depth, MLA_Q_RANK, MLA_HEADS, MLA_NOPE_DIM + MLA_ROPE_DIM)
    wuq = jnp.concatenate([uq, zeros(MLA_Q_RANK, MLA_HEADS, 32)], axis=-1).reshape(depth, MLA_Q_RANK, -1)
    ukv = w_ukv.reshape(depth, MLA_KV_RANK, MLA_HEADS, MLA_NOPE_DIM + MLA_V_DIM)
    kn, vv = ukv[..., :MLA_NOPE_DIM], ukv[..., MLA_NOPE_DIM:]
    z64 = zeros(MLA_KV_RANK, MLA_HEADS, 64)
    wkn = jnp.concatenate([kn, z64], axis=-1).reshape(depth, MLA_KV_RANK, -1)
    even = (jnp.arange(MLA_HEADS) % 2 == 0)[None, None, :, None]
    wv = jnp.where(even, jnp.concatenate([vv, z64], axis=-1), jnp.concatenate([z64, vv], axis=-1))
    wvt = jnp.swapaxes(wv.reshape(depth, MLA_KV_RANK, -1), 1, 2)
    gkn = jnp.concatenate([g_k_nope, zeros(64)], axis=-1)[:, None, :]
    cosa, sina, cosb, sinb = rope
    z32, z64v = zeros(32), zeros(64)
    tabs = (_gained_tables(cosa, sina, jnp.tile(g_q_gqa, (1, 2)), 16, GQA_SCALE * LOG2E)
            + _gained_tables(cosa, sina, jnp.tile(g_k_gqa, (1, 2)), 16, 1.0)
            + _gained_tables(cosb, sinb, jnp.concatenate([g_q_nope, g_q_rope, z32], axis=-1), 8, MLA_SCALE * LOG2E)
            + _gained_tables(cosb, sinb, jnp.concatenate([z64v, g_k_rope, z32], axis=-1), 8, 1.0))
    wts = [wa, wuq.astype(BF16), wkn.astype(BF16), wvt.astype(BF16), g_cq[:, None, :], g_ckv[:, None, :], gkn]
    return wts, tabs, gates.astype(BF16)


def kernel(x, c, ctx, c_ctx, w_mod, b_mod, g_norm1, g_norm2, w_in, g_q_gqa, g_k_gqa, g_cq, g_ckv, w_uq, w_ukv,
           g_q_nope, g_k_nope, g_q_rope, g_k_rope, b_gate, w_br_a, w_br_b, w_br_c, w_out, w_ffn_in, w_ffn_out):
    b, n, d = x.shape
    nc = ctx.shape[1]
    depth = w_mod.shape[0]
    assert n % TILE == 0 and nc == TILE and b + 1 <= MOD_ROWS and n % GRID_W == 0
    n_lat_tiles, n_tiles = n // TILE, (n + nc) // TILE

    cc = jnp.concatenate([c, c_ctx[None], jnp.zeros((MOD_ROWS - b - 1, d), F32)], axis=0)
    mod = _modulation(cc, w_mod, b_mod).reshape(depth, MOD_ROWS, 6, d)
    dn, dc, chan = _dft_tables(n, nc)
    segs = _seg_matrices()
    wts, tabs, wg = _inproj_weights(_rope_tables(n, nc), w_in, g_q_gqa, g_k_gqa, g_cq, g_ckv, w_uq, w_ukv,
                                    g_q_nope, g_k_nope, g_q_rope, g_k_rope)
    g1, g2 = g_norm1[:, None, :], g_norm2[:, None, :]
    mw = [wg, b_gate[:, None, :], w_br_a.astype(BF16), w_br_b.astype(BF16), w_br_c.astype(BF16), w_out.astype(BF16)]
    wi, wo = w_ffn_in.astype(BF16), w_ffn_out.astype(BF16)
    gqa_q = (0, 0, 128, 128, 256, 256, 384, 384)
    gqa_k = (0, 128, 0, 128, 256, 384, 256, 384)
    gqa_v = (0, 1, 0, 1, 2, 3, 2, 3)
    mla_o = tuple(range(0, 1024, 128))

    x_lat, x_ctx, ctx_tile = x, ctx, 0
    for l in range(depth):
        n_out = n_lat_tiles if l == depth - 1 else n_tiles
        qa, ka, vat, qb, kb, vbt, pf = _inproj(x_lat, x_ctx, ctx_tile, mod, l, n_lat_tiles, g1, wts, tabs, segs)
        ya = _attention(qa, ka, vat, n_lat_tiles, n_out, gqa_q, gqa_k, gqa_v, "attn_gqa")
        yb = _attention(qb, kb, vbt, n_lat_tiles, n_out, mla_o, mla_o, tuple(range(MLA_HEADS)), "attn_mla")
        yc = _fourier(pf, dn, dc, chan, n_lat_tiles, n_out)
        xm = _merge(x_lat, x_ctx, ctx_tile, mod, l, n_lat_tiles, n_out, g1, ya, yb, yc, mw)
        xall = _ffn(xm, mod, l, n_lat_tiles, n_out, g2, wi, wo)
        x_lat, x_ctx, ctx_tile = xall, xall, n_lat_tiles
    return xall
```

```python
import functools
import math

import numpy as np
import jax
import jax.numpy as jnp
from jax import lax
from jax.experimental import pallas as pl
from jax.experimental.pallas import tpu as pltpu

GRID_W = 64
ROPE_THETA = 10000.0
EPS = 1e-6
GQA_HEADS = 8
GQA_KV_HEADS = 2
GQA_HEAD_DIM = 64
GQA_SCALE = GQA_HEAD_DIM ** -0.5
MLA_HEADS = 8
MLA_Q_RANK = 384
MLA_KV_RANK = 256
MLA_NOPE_DIM = 64
MLA_ROPE_DIM = 32
MLA_V_DIM = 64
MLA_SCALE = (MLA_NOPE_DIM + MLA_ROPE_DIM) ** -0.5
FOURIER_GROUPS = 4
FOURIER_GROUP_DIM = 128
FOURIER_WIDTH = FOURIER_GROUPS * FOURIER_GROUP_DIM
N_BRANCH = 3
LOG2E = math.log2(math.e)

LANES = 128
TILE = 256
MOD_ROWS = 16
VMEM_LIMIT = 56 * 1024 * 1024

BF16 = jnp.bfloat16
F32 = jnp.float32

_Q0, _K0, _V0, _CQ0, _KR0, _CKV0, _F0, _WA = 0, 512, 640, 768, 1152, 1280, 1536, 2048


def _cparams(n_axes):
    return pltpu.CompilerParams(dimension_semantics=("arbitrary",) * n_axes, vmem_limit_bytes=VMEM_LIMIT)


def _dot(a, b):
    return jnp.dot(a, b, preferred_element_type=F32)


def _rms_rows(xv, g):
    return xv * lax.rsqrt(jnp.mean(xv * xv, axis=-1, keepdims=True) + EPS) * g


def _modulated(x, g, shift, scale):
    r = lax.rsqrt(jnp.mean(x * x, axis=-1, keepdims=True) + EPS)
    return (x * r * (g * (1.0 + scale)) + shift).astype(BF16)


def _layer_spec(a, l):
    return pl.BlockSpec((1,) + a.shape[1:], lambda *_: (l,) + (0,) * (a.ndim - 1))


def _const_spec(a):
    return pl.BlockSpec(a.shape, lambda *_: (0,) * a.ndim)


def _row_specs(x_lat, x_ctx, ctx_tile, n_lat_tiles):
    d = x_lat.shape[-1]
    return [pl.BlockSpec((1, TILE, d), lambda bi, ti: (bi, jnp.minimum(ti, n_lat_tiles - 1), 0)),
            pl.BlockSpec((1, TILE, d), lambda bi, ti: (bi, ctx_tile, 0))]


def _mod_spec(mod, l, n_batch, n_lat_tiles):
    return pl.BlockSpec((1, 1) + mod.shape[2:], lambda bi, ti: (l, jnp.where(ti >= n_lat_tiles, n_batch, bi), 0, 0))


def _tile_rows(xl_ref, xc_ref, n_lat_tiles):
    return jnp.where(pl.program_id(1) < n_lat_tiles, xl_ref[0], xc_ref[0])


def _mod_kernel(cc_ref, w_ref, b_ref, o_ref):
    s = cc_ref[...]
    s = s * jax.nn.sigmoid(s)
    o_ref[0] = _dot(s.astype(BF16), w_ref[0].astype(BF16)) + b_ref[0]


def _modulation(cc, w_mod, b_mod):
    depth, d, d6 = w_mod.shape
    bn = 1536
    return pl.pallas_call(
        _mod_kernel,
        grid=(depth, d6 // bn),
        in_specs=[pl.BlockSpec((MOD_ROWS, d), lambda l, j: (0, 0)),
                  pl.BlockSpec((1, d, bn), lambda l, j: (l, 0, j)),
                  pl.BlockSpec((1, 1, bn), lambda l, j: (l, 0, j))],
        out_specs=pl.BlockSpec((1, MOD_ROWS, bn), lambda l, j: (l, 0, j)),
        out_shape=jax.ShapeDtypeStruct((depth, MOD_ROWS, d6), F32),
        compiler_params=_cparams(2),
        name="modulation",
    )(cc, w_mod, b_mod.reshape(depth, 1, d6))


def _rot_half(xv, lane, seg):
    first = (lane & (2 * seg - 1)) < seg
    return jnp.where(first, pltpu.roll(xv, LANES - seg, 1), pltpu.roll(xv, seg, 1))


def _seg_rsqrt(xw, seg_ref):
    return lax.rsqrt(_dot((xw * xw).astype(BF16), seg_ref[...]) + EPS)


def _inproj_kernel(xl_ref, xc_ref, mod_ref, g1_ref, wa_ref, wuq_ref, wkn_ref, wvt_ref, gcq_ref, gckv_ref, gkn_ref,
                   cgq_ref, sgq_ref, cgk_ref, sgk_ref, cgqb_ref, sgqb_ref, cgkr_ref, sgkr_ref,
                   sgqa_ref, sqb_ref, skn_ref, skk_ref,
                   qa_ref, ka_ref, vat_ref, qb_ref, kb_ref, vbt_ref, pf_ref, kma_ref, kmb_ref, *, n_lat_tiles):
    m = mod_ref[0, 0]
    hb = _modulated(_tile_rows(xl_ref, xc_ref, n_lat_tiles), g1_ref[0], m[0:1], m[1:2])
    lane = lax.broadcasted_iota(jnp.int32, (1, LANES), 1)
    lo = lane < 64
    hi = lane >= 64
    blk = lambda a, j: a[:, j * LANES:(j + 1) * LANES]
    n_qblk = GQA_HEADS * GQA_HEAD_DIM // LANES

    p_c = _dot(hb, wa_ref[0, :, _CQ0:_CKV0])
    p_ckv = _dot(hb, wa_ref[0, :, _CKV0:_F0])
    pq = _dot(hb, wa_ref[0, :, _Q0:_K0])
    pkv = _dot(hb, wa_ref[0, :, _K0:_CQ0])
    pf_ref[0] = _dot(hb, wa_ref[0, :, _F0:_WA]).astype(BF16)
    cq = _rms_rows(p_c[:, :MLA_Q_RANK], gcq_ref[0]).astype(BF16)
    ckv = _rms_rows(p_ckv, gckv_ref[0])
    qb = _dot(cq, wuq_ref[0])
    kn_all = _dot(ckv.astype(BF16), wkn_ref[0])
    vbt = _dot(wvt_ref[0], ckv.T.astype(BF16))
    for j in range(MLA_HEADS):
        vbt_ref[0, j] = vbt[j * LANES:(j + 1) * LANES].astype(BF16)
    vt = pkv[:, LANES:].T.astype(BF16)
    zero = jnp.zeros((64, TILE), BF16)
    for kvh in range(GQA_KV_HEADS):
        rows = vt[kvh * 64:(kvh + 1) * 64]
        vat_ref[0, 2 * kvh, 0:64] = rows
        vat_ref[0, 2 * kvh, 64:128] = zero
        vat_ref[0, 2 * kvh + 1, 0:64] = zero
        vat_ref[0, 2 * kvh + 1, 64:128] = rows

    w2 = 2 * LANES
    r_qb = jnp.concatenate([_seg_rsqrt(qb[:, c * w2:(c + 1) * w2], sqb_ref) for c in range(MLA_HEADS // 2)], axis=1)
    r_kn = jnp.concatenate([_seg_rsqrt(kn_all[:, c * w2:(c + 1) * w2], skn_ref) for c in range(MLA_HEADS // 2)],
                           axis=1)
    r_q = jnp.concatenate([_seg_rsqrt(pq[:, c * w2:(c + 1) * w2], sgqa_ref) for c in range(n_qblk // 2)], axis=1)
    xkr, xk = p_c[:, MLA_Q_RANK:], pkv[:, :LANES]
    r_kk = _seg_rsqrt(jnp.concatenate([xk, xkr], axis=1), skk_ref)
    r_k, r_kr = r_kk[:, :LANES], r_kk[:, LANES:]

    roped = lambda xb, c_ref, s_ref, seg, r: (xb * c_ref[0] + _rot_half(xb, lane, seg) * s_ref[0]) * r
    for j in range(MLA_HEADS):
        qb_ref[0, :, j * LANES:(j + 1) * LANES] = roped(blk(qb, j), cgqb_ref, sgqb_ref, 8, blk(r_qb, j)).astype(BF16)
    kr = roped(xkr, cgkr_ref, sgkr_ref, 8, r_kr)
    tile_max = lambda sq: jnp.max(jnp.sum(sq, axis=-1, keepdims=True), axis=0, keepdims=True)
    row8 = lax.broadcasted_iota(jnp.int32, (8, LANES), 0)

    def head_rows(vals):
        out = jnp.zeros((8, LANES), F32)
        for h, v in enumerate(vals):
            out = jnp.where(row8 == h, v, out)
        return out

    kmax_b = []
    for j in range(MLA_HEADS):
        kbj = blk(kn_all, j) * blk(r_kn, j) * gkn_ref[0] + kr
        kb_ref[0, :, j * LANES:(j + 1) * LANES] = kbj.astype(BF16)
        kmax_b.append(tile_max(kbj * kbj))
    kmb_ref[0, 0] = head_rows(kmax_b)
    for j in range(n_qblk):
        qa_ref[0, :, j * LANES:(j + 1) * LANES] = roped(blk(pq, j), cgq_ref, sgq_ref, 16, blk(r_q, j)).astype(BF16)
    kn = roped(xk, cgk_ref, sgk_ref, 16, r_k)
    kn2 = kn * kn
    kma_ref[0, 0] = head_rows([tile_max(jnp.where(lo, kn2, 0.0)), tile_max(jnp.where(hi, kn2, 0.0))])
    sw = pltpu.roll(kn, 64, 1)
    for j, f in enumerate((jnp.where(lo, kn, 0.0), jnp.where(hi, sw, 0.0), jnp.where(lo, sw, 0.0),
                           jnp.where(hi, kn, 0.0))):
        ka_ref[0, :, j * LANES:(j + 1) * LANES] = f.astype(BF16)


def _inproj(x_lat, x_ctx, ctx_tile, mod, l, n_lat_tiles, g1, wts, tabs, segs):
    b = x_lat.shape[0]
    nt = n_lat_tiles + 1
    t = nt * TILE
    row = lambda bi, ti: (bi, ti, 0)
    rowt = lambda bi, ti: (bi, 0, 0, ti)
    tok = lambda w: (pl.BlockSpec((1, TILE, w), row), jax.ShapeDtypeStruct((b, t, w), BF16))
    tra = lambda h: (pl.BlockSpec((1, h, LANES, TILE), rowt), jax.ShapeDtypeStruct((b, h, LANES, t), BF16))
    kmx = (pl.BlockSpec((1, 1, 8, LANES), lambda bi, ti: (bi, ti, 0, 0)), jax.ShapeDtypeStruct((b, nt, 8, LANES), F32))
    outs = [tok(512), tok(512), tra(4), tok(1024), tok(1024), tra(MLA_HEADS), tok(512), kmx, kmx]
    return pl.pallas_call(
        functools.partial(_inproj_kernel, n_lat_tiles=n_lat_tiles),
        grid=(b, nt),
        in_specs=_row_specs(x_lat, x_ctx, ctx_tile, n_lat_tiles)
        + [_mod_spec(mod, l, b, n_lat_tiles), _layer_spec(g1, l)]
        + [_layer_spec(w, l) for w in wts]
        + [pl.BlockSpec((1, TILE, LANES), lambda bi, ti: (l, ti, 0)) for _ in tabs]
        + [_const_spec(s) for s in segs],
        out_specs=[o[0] for o in outs],
        out_shape=[o[1] for o in outs],
        compiler_params=_cparams(2),
        name="inproj",
    )(x_lat, x_ctx, mod, g1, *wts, *tabs, *segs)


def _reduce_keys(a, op):
    nk, nq = a.shape
    part = op(a.reshape(nk // TILE, TILE, nq), axis=0)
    return op(part, axis=0, keepdims=True)


SHIFT_SAFE = 60.0
BOUND_SLACK = 1.0 + 2.0 ** -6


def _attn_kernel(q_ref, k_ref, vt_ref, kmx_ref, o_ref, shift_ref, *, n_lat_tiles, with_ctx, qoff, koff, vidx,
                 kidx):
    n_keys = k_ref.shape[1]
    n_pairs = len(qoff) // 2
    zero = jnp.zeros((TILE, LANES), BF16)
    ones8 = jnp.ones((8, LANES), BF16)

    def q_blocks(row0, pair):
        e, o = 2 * pair, 2 * pair + 1
        return (q_ref[0, pl.ds(row0, TILE), qoff[e]:qoff[e] + LANES],
                q_ref[0, pl.ds(row0, TILE), qoff[o]:qoff[o] + LANES])

    def scores(row0, pair, key0, nk):
        e, o = 2 * pair, 2 * pair + 1
        assert koff[o] == koff[e] + LANES
        q_e, q_o = q_blocks(row0, pair)
        qd = jnp.concatenate([jnp.concatenate([q_e, zero], axis=1), jnp.concatenate([zero, q_o], axis=1)], axis=0)
        k2 = k_ref[0, key0:key0 + nk, koff[e]:koff[e] + 2 * LANES]
        return lax.dot_general(k2, qd, (((1,), (1,)), ((), ())), preferred_element_type=F32)

    def shift_bound(row0, pair, kmax2):
        halves = []
        for q, h in zip(q_blocks(row0, pair), (2 * pair, 2 * pair + 1)):
            q2 = lax.dot_general(ones8, q * q, (((1,), (1,)), ((), ())), preferred_element_type=F32)[0:1]
            kh = kmax2[kidx[h]:kidx[h] + 1]
            halves.append(jnp.sqrt(q2 * jnp.concatenate([kh] * (TILE // LANES), axis=1)) * BOUND_SLACK)
        return jnp.concatenate(halves, axis=1)

    def finish(st, shift, row0, pair, key0, nk):
        e, o = 2 * pair, 2 * pair + 1
        pt = jnp.exp2(st - (_reduce_keys(st, jnp.max) if shift is None else shift))
        rden = 1.0 / _reduce_keys(pt, jnp.sum)
        pb = pt.astype(BF16)
        ot = (_dot(vt_ref[0, vidx[e], :, key0:key0 + nk], pb[:, :TILE]) * rden[:, :TILE]
              + _dot(vt_ref[0, vidx[o], :, key0:key0 + nk], pb[:, TILE:]) * rden[:, TILE:])
        o_ref[0, pl.ds(row0, TILE), pair * LANES:(pair + 1) * LANES] = ot.T.astype(BF16)

    def all_shifts(tiles, kmax2):
        worst = jnp.zeros((1, 2 * TILE), F32)
        for t in tiles:
            for pair in range(n_pairs):
                sh = shift_bound(t * TILE, pair, kmax2)
                shift_ref[t * n_pairs + pair:t * n_pairs + pair + 1, :] = sh
                worst = jnp.maximum(worst, sh)
        return jnp.max(worst)

    def tile(t, row0, key0, nk, use_bound):
        for pair in range(n_pairs):
            shift = shift_ref[pl.ds(t * n_pairs + pair, 1), :] if use_bound else None
            finish(scores(row0, pair, key0, nk), shift, row0, pair, key0, nk)

    def latent_tiles(use_bound):
        def body(t, carry):
            tile(t, pl.multiple_of(t * TILE, TILE), 0, n_keys, use_bound)
            return carry
        lax.fori_loop(0, n_lat_tiles, body, 0)

    worst = all_shifts(range(n_lat_tiles), jnp.max(kmx_ref[0], axis=0))
    lax.cond(worst < SHIFT_SAFE, lambda: latent_tiles(True), lambda: latent_tiles(False))

    if with_ctx:
        lat = n_lat_tiles * TILE
        worst = all_shifts([n_lat_tiles], kmx_ref[0, n_lat_tiles])
        lax.cond(worst < SHIFT_SAFE, lambda: tile(n_lat_tiles, lat, lat, n_keys - lat, True),
                 lambda: tile(n_lat_tiles, lat, lat, n_keys - lat, False))


def _attention(q, k, vt, kmx, n_lat_tiles, n_q_tiles, qoff, koff, vidx, kidx, name):
    b, t, wq = q.shape
    wk = k.shape[2]
    nh = vt.shape[1]
    kern = functools.partial(_attn_kernel, n_lat_tiles=n_lat_tiles, with_ctx=n_q_tiles > n_lat_tiles,
                             qoff=qoff, koff=koff, vidx=vidx, kidx=kidx)
    return pl.pallas_call(
        kern,
        grid=(b,),
        in_specs=[pl.BlockSpec((1, t, wq), lambda bi: (bi, 0, 0)),
                  pl.BlockSpec((1, t, wk), lambda bi: (bi, 0, 0)),
                  pl.BlockSpec((1, nh, LANES, t), lambda bi: (bi, 0, 0, 0)),
                  pl.BlockSpec((1,) + kmx.shape[1:], lambda bi: (bi, 0, 0, 0))],
        out_specs=pl.BlockSpec((1, n_q_tiles * TILE, 512), lambda bi: (bi, 0, 0)),
        out_shape=jax.ShapeDtypeStruct((b, n_q_tiles * TILE, 512), BF16),
        scratch_shapes=[pltpu.VMEM(((t // TILE) * len(qoff) // 2, 2 * TILE), F32)],
        compiler_params=_cparams(1),
        name=name,
    )(q, k, vt, kmx)


def _fourier_kernel(pf_ref, dn_ref, dc_ref, cs_ref, o_ref, ab_ref, abc_ref, *, n_lat_tiles):
    ti = pl.program_id(1)
    n_lat = n_lat_tiles * TILE
    w = FOURIER_WIDTH

    @pl.when(ti == 0)
    def _():
        for c in range(n_lat_tiles):
            ab = _dot(pf_ref[0, c * TILE:(c + 1) * TILE, :], cs_ref[...])
            ab_ref[c * TILE:(c + 1) * TILE, :] = ab[:, :w].astype(BF16)
            ab_ref[n_lat + c * TILE:n_lat + (c + 1) * TILE, :] = ab[:, w:].astype(BF16)

    @pl.when(ti < n_lat_tiles)
    def _():
        o_ref[0] = _dot(dn_ref[...], ab_ref[...]).astype(BF16)

    @pl.when(ti >= n_lat_tiles)
    def _():
        ab = _dot(pf_ref[0, n_lat:n_lat + TILE, :], cs_ref[...])
        abc_ref[0:TILE, :] = ab[:, :w].astype(BF16)
        abc_ref[TILE:2 * TILE, :] = ab[:, w:].astype(BF16)
        o_ref[0] = _dot(dc_ref[...], abc_ref[...]).astype(BF16)


def _fourier(pf, dn, dc, cs, n_lat_tiles, n_tiles):
    b, t, w = pf.shape
    n_lat = n_lat_tiles * TILE
    kern = functools.partial(_fourier_kernel, n_lat_tiles=n_lat_tiles)
    return pl.pallas_call(
        kern,
        grid=(b, n_tiles),
        in_specs=[pl.BlockSpec((1, t, w), lambda bi, ti: (bi, 0, 0)),
                  pl.BlockSpec((TILE, 2 * n_lat), lambda bi, ti: (jnp.minimum(ti, n_lat_tiles - 1), 0)),
                  _const_spec(dc), _const_spec(cs)],
        out_specs=pl.BlockSpec((1, TILE, w), lambda bi, ti: (bi, ti, 0)),
        out_shape=jax.ShapeDtypeStruct((b, n_tiles * TILE, w), BF16),
        scratch_shapes=[pltpu.VMEM((2 * n_lat, w), BF16), pltpu.VMEM((2 * TILE, w), BF16)],
        compiler_params=_cparams(2),
        name="fourier",
    )(pf, dn, dc, cs)


def _merge_kernel(xl_ref, xc_ref, mod_ref, g1_ref, ya_ref, yb_ref, yc_ref, wg_ref, bg_ref, wa_ref, wb_ref, wc_ref,
                  wo_ref, o_ref, *, n_lat_tiles):
    x = _tile_rows(xl_ref, xc_ref, n_lat_tiles)
    m = mod_ref[0, 0]
    d = x.shape[-1]
    hb = _modulated(x, g1_ref[0], m[0:1], m[1:2])
    acc = None
    for i, (y_ref, w_ref) in enumerate(((ya_ref, wa_ref), (yb_ref, wb_ref), (yc_ref, wc_ref))):
        gate = jax.nn.sigmoid(_dot(hb, wg_ref[0, :, i * d:(i + 1) * d]) + bg_ref[0, :, i * d:(i + 1) * d])
        term = gate * _dot(y_ref[0], w_ref[0])
        acc = term if acc is None else acc + term
    o_ref[0] = x + m[2:3] * _dot(acc.astype(BF16), wo_ref[0])


def _merge(x_lat, x_ctx, ctx_tile, mod, l, n_lat_tiles, n_tiles, g1, ya, yb, yc, wts):
    b, _, d = x_lat.shape
    row = lambda bi, ti: (bi, ti, 0)
    return pl.pallas_call(
        functools.partial(_merge_kernel, n_lat_tiles=n_lat_tiles),
        grid=(b, n_tiles),
        in_specs=_row_specs(x_lat, x_ctx, ctx_tile, n_lat_tiles)
        + [_mod_spec(mod, l, b, n_lat_tiles), _layer_spec(g1, l)]
        + [pl.BlockSpec((1, TILE, 512), row)] * 3 + [_layer_spec(w, l) for w in wts],
        out_specs=pl.BlockSpec((1, TILE, d), row),
        out_shape=jax.ShapeDtypeStruct((b, n_tiles * TILE, d), F32),
        compiler_params=_cparams(2),
        name="merge",
    )(x_lat, x_ctx, mod, g1, ya, yb, yc, *wts)


def _ffn_kernel(x_ref, mod_ref, g2_ref, wi_ref, wo_ref, o_ref):
    x = x_ref[0]
    m = mod_ref[0, 0]
    hb = _modulated(x, g2_ref[0], m[3:4], m[4:5])
    gu = _dot(hb, wi_ref[0])
    hid = gu.shape[-1] // 2
    gate, up = gu[:, :hid], gu[:, hid:]
    act = (gate * jax.nn.sigmoid(gate) * up).astype(BF16)
    o_ref[0] = x + m[5:6] * _dot(act, wo_ref[0])


def _ffn(xm, mod, l, n_lat_tiles, n_tiles, g2, wi, wo):
    b, _, d = xm.shape
    row = lambda bi, ti: (bi, ti, 0)
    return pl.pallas_call(
        _ffn_kernel,
        grid=(b, n_tiles),
        in_specs=[pl.BlockSpec((1, TILE, d), row), _mod_spec(mod, l, b, n_lat_tiles), _layer_spec(g2, l),
                  _layer_spec(wi, l), _layer_spec(wo, l)],
        out_specs=pl.BlockSpec((1, TILE, d), row),
        out_shape=jax.ShapeDtypeStruct((b, n_tiles * TILE, d), F32),
        compiler_params=_cparams(2),
        name="ffn",
    )(xm, mod, g2, wi, wo)


def _partner(seg):
    lane = np.arange(LANES)
    return np.where((lane % (2 * seg)) < seg, lane + seg, lane - seg)


def _rope_tables(n_lat, n_ctx):
    rows = n_lat // GRID_W
    row_id = np.repeat(np.arange(rows), GRID_W).astype(np.float64)
    col_id = np.tile(np.arange(GRID_W), rows).astype(np.float64)

    def angles(dim):
        half = dim // 2
        freqs = ROPE_THETA ** (-np.arange(0, half, 2, dtype=np.float64) / half)
        ax = lambda pos: np.concatenate([pos[:, None] * freqs[None, :]] * 2, axis=-1)
        return np.concatenate([ax(row_id), ax(col_id)], axis=-1)

    def signed(sin, dim):
        sign = np.where((np.arange(dim) % (dim // 2)) < dim // 4, -1.0, 1.0)
        return sin * sign[None, :]

    a64, a32 = angles(GQA_HEAD_DIM), angles(MLA_ROPE_DIM)
    t = n_lat + n_ctx
    cosa, sina = np.ones((t, LANES)), np.zeros((t, LANES))
    cosa[:n_lat] = np.tile(np.cos(a64), (1, 2))
    sina[:n_lat] = np.tile(signed(np.sin(a64), GQA_HEAD_DIM), (1, 2))
    cosb, sinb = np.ones((t, LANES)), np.zeros((t, LANES))
    cosb[:n_lat, 64:96] = np.cos(a32)
    sinb[:n_lat, 64:96] = signed(np.sin(a32), MLA_ROPE_DIM)
    return [jnp.asarray(a, F32) for a in (cosa, sina, cosb, sinb)]


def _gained_tables(cos, sin, gain, seg, scale):
    g = gain * scale
    return [cos[None] * g[:, None, :], sin[None] * g[:, _partner(seg)][:, None, :]]


def _seg_matrices():
    def blockdiag(segs):
        m = np.zeros((2 * LANES, 2 * LANES))
        for start, n in segs:
            m[start:start + n, start:start + n] = 1.0 / n
        return m

    gqa = blockdiag([(s0, 64) for s0 in range(0, 256, 64)])
    qb = blockdiag([(0, 64), (64, 32), (128, 64), (192, 32)])
    kn = blockdiag([(0, 64), (128, 64)])
    kk = blockdiag([(0, 64), (64, 64), (192, 32)])
    return [jnp.asarray(a, F32).astype(BF16) for a in (gqa, qb, kn, kk)]


def _dft_tables(n_lat, n_ctx):
    def cs(n):
        j = np.arange(n)
        ang = 2.0 * np.pi * ((j[:, None] * j[None, :]) % n) / n
        return np.cos(ang) / np.sqrt(n), np.sin(ang) / np.sqrt(n)

    cn, sn = cs(n_lat)
    cc, sc = cs(n_ctx)
    cg, sg = cs(FOURIER_GROUP_DIM)
    eye = np.eye(FOURIER_GROUPS)
    dn = np.concatenate([cn, -sn], axis=1)
    dc = np.concatenate([cc, -sc], axis=1)
    chan = np.concatenate([np.kron(eye, cg), np.kron(eye, sg)], axis=1)
    return [jnp.asarray(a, F32).astype(BF16) for a in (dn, dc, chan)]


def _inproj_weights(rope, w_in, g_q_gqa, g_k_gqa, g_cq, g_ckv, w_uq, w_ukv, g_q_nope, g_k_nope, g_q_rope, g_k_rope):
    depth, d, _ = w_in.shape
    q, k, v, cq, ckv, kr, f, gates = (w_in[..., 0:512], w_in[..., 512:640], w_in[..., 640:768], w_in[..., 768:1152],
                                      w_in[..., 1152:1408], w_in[..., 1408:1440], w_in[..., 1440:1952],
                                      w_in[..., 1952:])
    zeros = lambda *s: jnp.zeros((depth,) + s, F32)
    krblk = jnp.concatenate([zeros(d, 64), kr, zeros(d, 32)], axis=-1)
    wa = jnp.concatenate([q, k, v, cq, krblk, ckv, f], axis=-1).astype(BF16)
    uq = w_uq.reshape(depth, MLA_Q_RANK, MLA_HEADS, MLA_NOPE_DIM + MLA_ROPE_DIM)
    wuq = jnp.concatenate([uq, zeros(MLA_Q_RANK, MLA_HEADS, 32)], axis=-1).reshape(depth, MLA_Q_RANK, -1)
    ukv = w_ukv.reshape(depth, MLA_KV_RANK, MLA_HEADS, MLA_NOPE_DIM + MLA_V_DIM)
    kn, vv = ukv[..., :MLA_NOPE_DIM], ukv[..., MLA_NOPE_DIM:]
    z64 = zeros(MLA_KV_RANK, MLA_HEADS, 64)
    wkn = jnp.concatenate([kn, z64], axis=-1).reshape(depth, MLA_KV_RANK, -1)
    even = (jnp.arange(MLA_HEADS) % 2 == 0)[None, None, :, None]
    wv = jnp.where(even, jnp.concatenate([vv, z64], axis=-1), jnp.concatenate([z64, vv], axis=-1))
    wvt = jnp.swapaxes(wv.reshape(depth, MLA_KV_RANK, -1), 1, 2)
    gkn = jnp.concatenate([g_k_nope, zeros(64)], axis=-1)[:, None, :]
    cosa, sina, cosb, sinb = rope
    z32, z64v = zeros(32), zeros(64)
    tabs = (_gained_tables(cosa, sina, jnp.tile(g_q_gqa, (1, 2)), 16, GQA_SCALE * LOG2E)
            + _gained_tables(cosa, sina, jnp.tile(g_k_gqa, (1, 2)), 16, 1.0)
            + _gained_tables(cosb, sinb, jnp.concatenate([g_q_nope, g_q_rope, z32], axis=-1), 8, MLA_SCALE * LOG2E)
            + _gained_tables(cosb, sinb, jnp.concatenate([z64v, g_k_rope, z32], axis=-1), 8, 1.0))
    wts = [wa, wuq.astype(BF16), wkn.astype(BF16), wvt.astype(BF16), g_cq[:, None, :], g_ckv[:, None, :], gkn]
    return wts, tabs, gates.astype(BF16)


def kernel(x, c, ctx, c_ctx, w_mod, b_mod, g_norm1, g_norm2, w_in, g_q_gqa, g_k_gqa, g_cq, g_ckv, w_uq, w_ukv,
           g_q_nope, g_k_nope, g_q_rope, g_k_rope, b_gate, w_br_a, w_br_b, w_br_c, w_out, w_ffn_in, w_ffn_out):
    b, n, d = x.shape
    nc = ctx.shape[1]
    depth = w_mod.shape[0]
    assert n % TILE == 0 and nc == TILE and b + 1 <= MOD_ROWS and n % GRID_W == 0
    n_lat_tiles, n_tiles = n // TILE, (n + nc) // TILE

    cc = jnp.concatenate([c, c_ctx[None], jnp.zeros((MOD_ROWS - b - 1, d), F32)], axis=0)
    mod = _modulation(cc, w_mod, b_mod).reshape(depth, MOD_ROWS, 6, d)
    dn, dc, chan = _dft_tables(n, nc)
    segs = _seg_matrices()
    wts, tabs, wg = _inproj_weights(_rope_tables(n, nc), w_in, g_q_gqa, g_k_gqa, g_cq, g_ckv, w_uq, w_ukv,
                                    g_q_nope, g_k_nope, g_q_rope, g_k_rope)
    g1, g2 = g_norm1[:, None, :], g_norm2[:, None, :]
    mw = [wg, b_gate[:, None, :], w_br_a.astype(BF16), w_br_b.astype(BF16), w_br_c.astype(BF16), w_out.astype(BF16)]
    wi, wo = w_ffn_in.astype(BF16), w_ffn_out.astype(BF16)
    gqa_q = (0, 0, 128, 128, 256, 256, 384, 384)
    gqa_k = (0, 128, 0, 128, 256, 384, 256, 384)
    gqa_v = (0, 1, 0, 1, 2, 3, 2, 3)
    gqa_n = (0, 0, 0, 0, 1, 1, 1, 1)
    mla_o = tuple(range(0, 1024, 128))
    mla_h = tuple(range(MLA_HEADS))

    x_lat, x_ctx, ctx_tile = x, ctx, 0
    for l in range(depth):
        n_out = n_lat_tiles if l == depth - 1 else n_tiles
        qa, ka, vat, qb, kb, vbt, pf, kma, kmb = _inproj(x_lat, x_ctx, ctx_tile, mod, l, n_lat_tiles, g1, wts, tabs,
                                                         segs)
        ya = _attention(qa, ka, vat, kma, n_lat_tiles, n_out, gqa_q, gqa_k, gqa_v, gqa_n, "attn_gqa")
        yb = _attention(qb, kb, vbt, kmb, n_lat_tiles, n_out, mla_o, mla_o, mla_h, mla_h, "attn_mla")
        yc = _fourier(pf, dn, dc, chan, n_lat_tiles, n_out)
        xm = _merge(x_lat, x_ctx, ctx_tile, mod, l, n_lat_tiles, n_out, g1, ya, yb, yc, mw)
        xall = _ffn(xm, mod, l, n_lat_tiles, n_out, g2, wi, wo)
        x_lat, x_ctx, ctx_tile = xall, xall, n_lat_tiles
    return xall
```

```python
import functools
import math

import numpy as np
import jax
import jax.numpy as jnp
from jax import lax
from jax.experimental import pallas as pl
from jax.experimental.pallas import tpu as pltpu

GRID_W = 64
ROPE_THETA = 10000.0
EPS = 1e-6
GQA_HEADS = 8
GQA_KV_HEADS = 2
GQA_HEAD_DIM = 64
GQA_SCALE = GQA_HEAD_DIM ** -0.5
MLA_HEADS = 8
MLA_Q_RANK = 384
MLA_KV_RANK = 256
MLA_NOPE_DIM = 64
MLA_ROPE_DIM = 32
MLA_V_DIM = 64
MLA_SCALE = (MLA_NOPE_DIM + MLA_ROPE_DIM) ** -0.5
FOURIER_GROUPS = 4
FOURIER_GROUP_DIM = 128
FOURIER_WIDTH = FOURIER_GROUPS * FOURIER_GROUP_DIM
N_BRANCH = 3
LOG2E = math.log2(math.e)

LANES = 128
TILE = 256
MOD_ROWS = 16
VMEM_LIMIT = 56 * 1024 * 1024

BF16 = jnp.bfloat16
F32 = jnp.float32

_Q0, _K0, _V0, _CQ0, _KR0, _CKV0, _F0, _WA = 0, 512, 640, 768, 1152, 1280, 1536, 2048


def _cparams(n_axes):
    return pltpu.CompilerParams(dimension_semantics=("arbitrary",) * n_axes, vmem_limit_bytes=VMEM_LIMIT)


def _dot(a, b):
    return jnp.dot(a, b, preferred_element_type=F32)


def _rms_rows(xv, g):
    return xv * lax.rsqrt(jnp.mean(xv * xv, axis=-1, keepdims=True) + EPS) * g


def _modulated(x, g, shift, scale):
    r = lax.rsqrt(jnp.mean(x * x, axis=-1, keepdims=True) + EPS)
    return (x * r * (g * (1.0 + scale)) + shift).astype(BF16)


def _layer_spec(a, l):
    return pl.BlockSpec((1,) + a.shape[1:], lambda *_: (l,) + (0,) * (a.ndim - 1))


def _const_spec(a):
    return pl.BlockSpec(a.shape, lambda *_: (0,) * a.ndim)


def _row_specs(x_lat, x_ctx, ctx_tile, n_lat_tiles):
    d = x_lat.shape[-1]
    return [pl.BlockSpec((1, TILE, d), lambda bi, ti: (bi, jnp.minimum(ti, n_lat_tiles - 1), 0)),
            pl.BlockSpec((1, TILE, d), lambda bi, ti: (bi, ctx_tile, 0))]


def _mod_spec(mod, l, n_batch, n_lat_tiles):
    return pl.BlockSpec((1, 1) + mod.shape[2:], lambda bi, ti: (l, jnp.where(ti >= n_lat_tiles, n_batch, bi), 0, 0))


def _tile_rows(xl_ref, xc_ref, n_lat_tiles):
    return jnp.where(pl.program_id(1) < n_lat_tiles, xl_ref[0], xc_ref[0])


def _mod_kernel(cc_ref, w_ref, b_ref, o_ref):
    s = cc_ref[...]
    s = s * jax.nn.sigmoid(s)
    o_ref[0] = _dot(s.astype(BF16), w_ref[0].astype(BF16)) + b_ref[0]


def _modulation(cc, w_mod, b_mod):
    depth, d, d6 = w_mod.shape
    bn = 1536
    return pl.pallas_call(
        _mod_kernel,
        grid=(depth, d6 // bn),
        in_specs=[pl.BlockSpec((MOD_ROWS, d), lambda l, j: (0, 0)),
                  pl.BlockSpec((1, d, bn), lambda l, j: (l, 0, j)),
                  pl.BlockSpec((1, 1, bn), lambda l, j: (l, 0, j))],
        out_specs=pl.BlockSpec((1, MOD_ROWS, bn), lambda l, j: (l, 0, j)),
        out_shape=jax.ShapeDtypeStruct((depth, MOD_ROWS, d6), F32),
        compiler_params=_cparams(2),
        name="modulation",
    )(cc, w_mod, b_mod.reshape(depth, 1, d6))


def _rot_half(xv, lane, seg):
    first = (lane & (2 * seg - 1)) < seg
    return jnp.where(first, pltpu.roll(xv, LANES - seg, 1), pltpu.roll(xv, seg, 1))


def _seg_rsqrt(xw, seg_ref):
    return lax.rsqrt(_dot((xw * xw).astype(BF16), seg_ref[...]) + EPS)


def _inproj_kernel(xl_ref, xc_ref, mod_ref, g1_ref, wa_ref, wuq_ref, wkn_ref, wvt_ref, gcq_ref, gckv_ref, gkn_ref,
                   cgq_ref, sgq_ref, cgk_ref, sgk_ref, cgqb_ref, sgqb_ref, cgkr_ref, sgkr_ref,
                   sgqa_ref, sqb_ref, skn_ref, skk_ref,
                   qa_ref, ka_ref, vat_ref, qb_ref, kb_ref, vbt_ref, pf_ref, *, n_lat_tiles):
    m = mod_ref[0, 0]
    hb = _modulated(_tile_rows(xl_ref, xc_ref, n_lat_tiles), g1_ref[0], m[0:1], m[1:2])
    lane = lax.broadcasted_iota(jnp.int32, (1, LANES), 1)
    lo = lane < 64
    hi = lane >= 64
    blk = lambda a, j: a[:, j * LANES:(j + 1) * LANES]
    n_qblk = GQA_HEADS * GQA_HEAD_DIM // LANES

    p_c = _dot(hb, wa_ref[0, :, _CQ0:_CKV0])
    p_ckv = _dot(hb, wa_ref[0, :, _CKV0:_F0])
    pq = _dot(hb, wa_ref[0, :, _Q0:_K0])
    pkv = _dot(hb, wa_ref[0, :, _K0:_CQ0])
    pf_ref[0] = _dot(hb, wa_ref[0, :, _F0:_WA]).astype(BF16)
    cq = _rms_rows(p_c[:, :MLA_Q_RANK], gcq_ref[0]).astype(BF16)
    ckv = _rms_rows(p_ckv, gckv_ref[0])
    qb = _dot(cq, wuq_ref[0])
    kn_all = _dot(ckv.astype(BF16), wkn_ref[0])
    vbt = _dot(wvt_ref[0], ckv.T.astype(BF16))
    for j in range(MLA_HEADS):
        vbt_ref[0, j] = vbt[j * LANES:(j + 1) * LANES].astype(BF16)
    vt = pkv[:, LANES:].T.astype(BF16)
    zero = jnp.zeros((64, TILE), BF16)
    for kvh in range(GQA_KV_HEADS):
        rows = vt[kvh * 64:(kvh + 1) * 64]
        vat_ref[0, 2 * kvh, 0:64] = rows
        vat_ref[0, 2 * kvh, 64:128] = zero
        vat_ref[0, 2 * kvh + 1, 0:64] = zero
        vat_ref[0, 2 * kvh + 1, 64:128] = rows

    w2 = 2 * LANES
    r_qb = jnp.concatenate([_seg_rsqrt(qb[:, c * w2:(c + 1) * w2], sqb_ref) for c in range(MLA_HEADS // 2)], axis=1)
    r_kn = jnp.concatenate([_seg_rsqrt(kn_all[:, c * w2:(c + 1) * w2], skn_ref) for c in range(MLA_HEADS // 2)],
                           axis=1)
    r_q = jnp.concatenate([_seg_rsqrt(pq[:, c * w2:(c + 1) * w2], sgqa_ref) for c in range(n_qblk // 2)], axis=1)
    xkr, xk = p_c[:, MLA_Q_RANK:], pkv[:, :LANES]
    r_kk = _seg_rsqrt(jnp.concatenate([xk, xkr], axis=1), skk_ref)
    r_k, r_kr = r_kk[:, :LANES], r_kk[:, LANES:]

    roped = lambda xb, c_ref, s_ref, seg, r: (xb * c_ref[0] + _rot_half(xb, lane, seg) * s_ref[0]) * r
    for j in range(MLA_HEADS):
        qb_ref[0, :, j * LANES:(j + 1) * LANES] = roped(blk(qb, j), cgqb_ref, sgqb_ref, 8, blk(r_qb, j)).astype(BF16)
    kr = roped(xkr, cgkr_ref, sgkr_ref, 8, r_kr)
    for j in range(MLA_HEADS):
        kb_ref[0, :, j * LANES:(j + 1) * LANES] = (blk(kn_all, j) * blk(r_kn, j) * gkn_ref[0] + kr).astype(BF16)
    for j in range(n_qblk):
        qa_ref[0, :, j * LANES:(j + 1) * LANES] = roped(blk(pq, j), cgq_ref, sgq_ref, 16, blk(r_q, j)).astype(BF16)
    kn = roped(xk, cgk_ref, sgk_ref, 16, r_k)
    sw = pltpu.roll(kn, 64, 1)
    for j, f in enumerate((jnp.where(lo, kn, 0.0), jnp.where(hi, sw, 0.0), jnp.where(lo, sw, 0.0),
                           jnp.where(hi, kn, 0.0))):
        ka_ref[0, :, j * LANES:(j + 1) * LANES] = f.astype(BF16)


def _inproj(x_lat, x_ctx, ctx_tile, mod, l, n_lat_tiles, g1, wts, tabs, segs):
    b = x_lat.shape[0]
    nt = n_lat_tiles + 1
    t = nt * TILE
    row = lambda bi, ti: (bi, ti, 0)
    rowt = lambda bi, ti: (bi, 0, 0, ti)
    tok = lambda w: (pl.BlockSpec((1, TILE, w), row), jax.ShapeDtypeStruct((b, t, w), BF16))
    tra = lambda h: (pl.BlockSpec((1, h, LANES, TILE), rowt), jax.ShapeDtypeStruct((b, h, LANES, t), BF16))
    outs = [tok(512), tok(512), tra(4), tok(1024), tok(1024), tra(MLA_HEADS), tok(512)]
    return pl.pallas_call(
        functools.partial(_inproj_kernel, n_lat_tiles=n_lat_tiles),
        grid=(b, nt),
        in_specs=_row_specs(x_lat, x_ctx, ctx_tile, n_lat_tiles)
        + [_mod_spec(mod, l, b, n_lat_tiles), _layer_spec(g1, l)]
        + [_layer_spec(w, l) for w in wts]
        + [pl.BlockSpec((1, TILE, LANES), lambda bi, ti: (l, ti, 0)) for _ in tabs]
        + [_const_spec(s) for s in segs],
        out_specs=[o[0] for o in outs],
        out_shape=[o[1] for o in outs],
        compiler_params=_cparams(2),
        name="inproj",
    )(x_lat, x_ctx, mod, g1, *wts, *tabs, *segs)


def _reduce_keys(a, op):
    nk, nq = a.shape
    part = op(a.reshape(nk // TILE, TILE, nq), axis=0)
    return op(part, axis=0, keepdims=True)


SHIFT_SAFE = 60.0
BOUND_SLACK = 1.0 + 2.0 ** -6


def _attn_kernel(shift_ref, q_ref, k_ref, vt_ref, o_ref, *, n_lat_tiles, with_ctx, qoff, koff, vidx):
    n_keys = k_ref.shape[1]
    n_pairs = len(qoff) // 2
    zero = jnp.zeros((TILE, LANES), BF16)
    bound = shift_ref[0]

    def scores(row0, pair, key0, nk):
        e, o = 2 * pair, 2 * pair + 1
        assert koff[o] == koff[e] + LANES
        q_e = q_ref[0, pl.ds(row0, TILE), qoff[e]:qoff[e] + LANES]
        q_o = q_ref[0, pl.ds(row0, TILE), qoff[o]:qoff[o] + LANES]
        qd = jnp.concatenate([jnp.concatenate([q_e, zero], axis=1), jnp.concatenate([zero, q_o], axis=1)], axis=0)
        k2 = k_ref[0, key0:key0 + nk, koff[e]:koff[e] + 2 * LANES]
        return lax.dot_general(k2, qd, (((1,), (1,)), ((), ())), preferred_element_type=F32)

    def tile(row0, key0, nk, use_bound):
        for pair in range(n_pairs):
            e, o = 2 * pair, 2 * pair + 1
            st = scores(row0, pair, key0, nk)
            pt = jnp.exp2(st - (bound if use_bound else _reduce_keys(st, jnp.max)))
            rden = 1.0 / _reduce_keys(pt, jnp.sum)
            pb = pt.astype(BF16)
            ot = (_dot(vt_ref[0, vidx[e], :, key0:key0 + nk], pb[:, :TILE]) * rden[:, :TILE]
                  + _dot(vt_ref[0, vidx[o], :, key0:key0 + nk], pb[:, TILE:]) * rden[:, TILE:])
            o_ref[0, pl.ds(row0, TILE), pair * LANES:(pair + 1) * LANES] = ot.T.astype(BF16)

    def all_tiles(use_bound):
        def body(t, carry):
            tile(pl.multiple_of(t * TILE, TILE), 0, n_keys, use_bound)
            return carry
        lax.fori_loop(0, n_lat_tiles, body, 0)
        if with_ctx:
            lat = n_lat_tiles * TILE
            tile(lat, lat, n_keys - lat, use_bound)

    lax.cond(bound < SHIFT_SAFE, lambda: all_tiles(True), lambda: all_tiles(False))


def _attention(shift, q, k, vt, n_lat_tiles, n_q_tiles, qoff, koff, vidx, name):
    b, t, wq = q.shape
    wk = k.shape[2]
    nh = vt.shape[1]
    kern = functools.partial(_attn_kernel, n_lat_tiles=n_lat_tiles, with_ctx=n_q_tiles > n_lat_tiles,
                             qoff=qoff, koff=koff, vidx=vidx)
    return pl.pallas_call(
        kern,
        grid=(b,),
        in_specs=[pl.BlockSpec(memory_space=pltpu.SMEM),
                  pl.BlockSpec((1, t, wq), lambda bi: (bi, 0, 0)),
                  pl.BlockSpec((1, t, wk), lambda bi: (bi, 0, 0)),
                  pl.BlockSpec((1, nh, LANES, t), lambda bi: (bi, 0, 0, 0))],
        out_specs=pl.BlockSpec((1, n_q_tiles * TILE, 512), lambda bi: (bi, 0, 0)),
        out_shape=jax.ShapeDtypeStruct((b, n_q_tiles * TILE, 512), BF16),
        compiler_params=_cparams(1),
        name=name,
    )(shift, q, k, vt)


def _fourier_kernel(pf_ref, dn_ref, dc_ref, cs_ref, o_ref, ab_ref, abc_ref, *, n_lat_tiles):
    ti = pl.program_id(1)
    n_lat = n_lat_tiles * TILE
    w = FOURIER_WIDTH

    @pl.when(ti == 0)
    def _():
        for c in range(n_lat_tiles):
            ab = _dot(pf_ref[0, c * TILE:(c + 1) * TILE, :], cs_ref[...])
            ab_ref[c * TILE:(c + 1) * TILE, :] = ab[:, :w].astype(BF16)
            ab_ref[n_lat + c * TILE:n_lat + (c + 1) * TILE, :] = ab[:, w:].astype(BF16)

    @pl.when(ti < n_lat_tiles)
    def _():
        o_ref[0] = _dot(dn_ref[...], ab_ref[...]).astype(BF16)

    @pl.when(ti >= n_lat_tiles)
    def _():
        ab = _dot(pf_ref[0, n_lat:n_lat + TILE, :], cs_ref[...])
        abc_ref[0:TILE, :] = ab[:, :w].astype(BF16)
        abc_ref[TILE:2 * TILE, :] = ab[:, w:].astype(BF16)
        o_ref[0] = _dot(dc_ref[...], abc_ref[...]).astype(BF16)


def _fourier(pf, dn, dc, cs, n_lat_tiles, n_tiles):
    b, t, w = pf.shape
    n_lat = n_lat_tiles * TILE
    kern = functools.partial(_fourier_kernel, n_lat_tiles=n_lat_tiles)
    return pl.pallas_call(
        kern,
        grid=(b, n_tiles),
        in_specs=[pl.BlockSpec((1, t, w), lambda bi, ti: (bi, 0, 0)),
                  pl.BlockSpec((TILE, 2 * n_lat), lambda bi, ti: (jnp.minimum(ti, n_lat_tiles - 1), 0)),
                  _const_spec(dc), _const_spec(cs)],
        out_specs=pl.BlockSpec((1, TILE, w), lambda bi, ti: (bi, ti, 0)),
        out_shape=jax.ShapeDtypeStruct((b, n_tiles * TILE, w), BF16),
        scratch_shapes=[pltpu.VMEM((2 * n_lat, w), BF16), pltpu.VMEM((2 * TILE, w), BF16)],
        compiler_params=_cparams(2),
        name="fourier",
    )(pf, dn, dc, cs)


def _merge_kernel(xl_ref, xc_ref, mod_ref, g1_ref, ya_ref, yb_ref, yc_ref, wg_ref, bg_ref, wa_ref, wb_ref, wc_ref,
                  wo_ref, o_ref, *, n_lat_tiles):
    x = _tile_rows(xl_ref, xc_ref, n_lat_tiles)
    m = mod_ref[0, 0]
    d = x.shape[-1]
    hb = _modulated(x, g1_ref[0], m[0:1], m[1:2])
    acc = None
    for i, (y_ref, w_ref) in enumerate(((ya_ref, wa_ref), (yb_ref, wb_ref), (yc_ref, wc_ref))):
        gate = jax.nn.sigmoid(_dot(hb, wg_ref[0, :, i * d:(i + 1) * d]) + bg_ref[0, :, i * d:(i + 1) * d])
        term = gate * _dot(y_ref[0], w_ref[0])
        acc = term if acc is None else acc + term
    o_ref[0] = x + m[2:3] * _dot(acc.astype(BF16), wo_ref[0])


def _merge(x_lat, x_ctx, ctx_tile, mod, l, n_lat_tiles, n_tiles, g1, ya, yb, yc, wts):
    b, _, d = x_lat.shape
    row = lambda bi, ti: (bi, ti, 0)
    return pl.pallas_call(
        functools.partial(_merge_kernel, n_lat_tiles=n_lat_tiles),
        grid=(b, n_tiles),
        in_specs=_row_specs(x_lat, x_ctx, ctx_tile, n_lat_tiles)
        + [_mod_spec(mod, l, b, n_lat_tiles), _layer_spec(g1, l)]
        + [pl.BlockSpec((1, TILE, 512), row)] * 3 + [_layer_spec(w, l) for w in wts],
        out_specs=pl.BlockSpec((1, TILE, d), row),
        out_shape=jax.ShapeDtypeStruct((b, n_tiles * TILE, d), F32),
        compiler_params=_cparams(2),
        name="merge",
    )(x_lat, x_ctx, mod, g1, ya, yb, yc, *wts)


def _ffn_kernel(x_ref, mod_ref, g2_ref, wi_ref, wo_ref, o_ref):
    x = x_ref[0]
    m = mod_ref[0, 0]
    hb = _modulated(x, g2_ref[0], m[3:4], m[4:5])
    gu = _dot(hb, wi_ref[0])
    hid = gu.shape[-1] // 2
    gate, up = gu[:, :hid], gu[:, hid:]
    act = (gate * jax.nn.sigmoid(gate) * up).astype(BF16)
    o_ref[0] = x + m[5:6] * _dot(act, wo_ref[0])


def _ffn(xm, mod, l, n_lat_tiles, n_tiles, g2, wi, wo):
    b, _, d = xm.shape
    row = lambda bi, ti: (bi, ti, 0)
    return pl.pallas_call(
        _ffn_kernel,
        grid=(b, n_tiles),
        in_specs=[pl.BlockSpec((1, TILE, d), row), _mod_spec(mod, l, b, n_lat_tiles), _layer_spec(g2, l),
                  _layer_spec(wi, l), _layer_spec(wo, l)],
        out_specs=pl.BlockSpec((1, TILE, d), row),
        out_shape=jax.ShapeDtypeStruct((b, n_tiles * TILE, d), F32),
        compiler_params=_cparams(2),
        name="ffn",
    )(xm, mod, g2, wi, wo)


def _partner(seg):
    lane = np.arange(LANES)
    return np.where((lane % (2 * seg)) < seg, lane + seg, lane - seg)


def _rope_tables(n_lat, n_ctx):
    rows = n_lat // GRID_W
    row_id = np.repeat(np.arange(rows), GRID_W).astype(np.float64)
    col_id = np.tile(np.arange(GRID_W), rows).astype(np.float64)

    def angles(dim):
        half = dim // 2
        freqs = ROPE_THETA ** (-np.arange(0, half, 2, dtype=np.float64) / half)
        ax = lambda pos: np.concatenate([pos[:, None] * freqs[None, :]] * 2, axis=-1)
        return np.concatenate([ax(row_id), ax(col_id)], axis=-1)

    def signed(sin, dim):
        sign = np.where((np.arange(dim) % (dim // 2)) < dim // 4, -1.0, 1.0)
        return sin * sign[None, :]

    a64, a32 = angles(GQA_HEAD_DIM), angles(MLA_ROPE_DIM)
    t = n_lat + n_ctx
    cosa, sina = np.ones((t, LANES)), np.zeros((t, LANES))
    cosa[:n_lat] = np.tile(np.cos(a64), (1, 2))
    sina[:n_lat] = np.tile(signed(np.sin(a64), GQA_HEAD_DIM), (1, 2))
    cosb, sinb = np.ones((t, LANES)), np.zeros((t, LANES))
    cosb[:n_lat, 64:96] = np.cos(a32)
    sinb[:n_lat, 64:96] = signed(np.sin(a32), MLA_ROPE_DIM)
    return [jnp.asarray(a, F32) for a in (cosa, sina, cosb, sinb)]


def _gained_tables(cos, sin, gain, seg, scale):
    g = gain * scale
    return [cos[None] * g[:, None, :], sin[None] * g[:, _partner(seg)][:, None, :]]


def _seg_matrices():
    def blockdiag(segs):
        m = np.zeros((2 * LANES, 2 * LANES))
        for start, n in segs:
            m[start:start + n, start:start + n] = 1.0 / n
        return m

    gqa = blockdiag([(s0, 64) for s0 in range(0, 256, 64)])
    qb = blockdiag([(0, 64), (64, 32), (128, 64), (192, 32)])
    kn = blockdiag([(0, 64), (128, 64)])
    kk = blockdiag([(0, 64), (64, 64), (192, 32)])
    return [jnp.asarray(a, F32).astype(BF16) for a in (gqa, qb, kn, kk)]


def _score_bounds(g_q_gqa, g_k_gqa, g_q_nope, g_k_nope, g_q_rope, g_k_rope):
    top = lambda g: jnp.max(g * g, axis=-1)
    gqa = jnp.sqrt(GQA_HEAD_DIM * top(g_q_gqa) * GQA_HEAD_DIM * top(g_k_gqa)) * (GQA_SCALE * LOG2E)
    mla = jnp.sqrt((MLA_NOPE_DIM * top(g_q_nope) + MLA_ROPE_DIM * top(g_q_rope))
                   * (MLA_NOPE_DIM * top(g_k_nope) + MLA_ROPE_DIM * top(g_k_rope))) * (MLA_SCALE * LOG2E)
    return jnp.stack([gqa, mla], axis=-1) * BOUND_SLACK


def _dft_tables(n_lat, n_ctx):
    def cs(n):
        j = np.arange(n)
        ang = 2.0 * np.pi * ((j[:, None] * j[None, :]) % n) / n
        return np.cos(ang) / np.sqrt(n), np.sin(ang) / np.sqrt(n)

    cn, sn = cs(n_lat)
    cc, sc = cs(n_ctx)
    cg, sg = cs(FOURIER_GROUP_DIM)
    eye = np.eye(FOURIER_GROUPS)
    dn = np.concatenate([cn, -sn], axis=1)
    dc = np.concatenate([cc, -sc], axis=1)
    chan = np.concatenate([np.kron(eye, cg), np.kron(eye, sg)], axis=1)
    return [jnp.asarray(a, F32).astype(BF16) for a in (dn, dc, chan)]


def _inproj_weights(rope, w_in, g_q_gqa, g_k_gqa, g_cq, g_ckv, w_uq, w_ukv, g_q_nope, g_k_nope, g_q_rope, g_k_rope):
    depth, d, _ = w_in.shape
    q, k, v, cq, ckv, kr, f, gates = (w_in[..., 0:512], w_in[..., 512:640], w_in[..., 640:768], w_in[..., 768:1152],
                                      w_in[..., 1152:1408], w_in[..., 1408:1440], w_in[..., 1440:1952],
                                      w_in[..., 1952:])
    zeros = lambda *s: jnp.zeros((depth,) + s, F32)
    krblk = jnp.concatenate([zeros(d, 64), kr, zeros(d, 32)], axis=-1)
    wa = jnp.concatenate([q, k, v, cq, krblk, ckv, f], axis=-1).astype(BF16)
    uq = w_uq.reshape(depth, MLA_Q_RANK, MLA_HEADS, MLA_NOPE_DIM + MLA_ROPE_DIM)
    wuq = jnp.concatenate([uq, zeros(MLA_Q_RANK, MLA_HEADS, 32)], axis=-1).reshape(depth, MLA_Q_RANK, -1)
    ukv = w_ukv.reshape(depth, MLA_KV_RANK, MLA_HEADS, MLA_NOPE_DIM + MLA_V_DIM)
    kn, vv = ukv[..., :MLA_NOPE_DIM], ukv[..., MLA_NOPE_DIM:]
    z64 = zeros(MLA_KV_RANK, MLA_HEADS, 64)
    wkn = jnp.concatenate([kn, z64], axis=-1).reshape(depth, MLA_KV_RANK, -1)
    even = (jnp.arange(MLA_HEADS) % 2 == 0)[None, None, :, None]
    wv = jnp.where(even, jnp.concatenate([vv, z64], axis=-1), jnp.concatenate([z64, vv], axis=-1))
    wvt = jnp.swapaxes(wv.reshape(depth, MLA_KV_RANK, -1), 1, 2)
    gkn = jnp.concatenate([g_k_nope, zeros(64)], axis=-1)[:, None, :]
    cosa, sina, cosb, sinb = rope
    z32, z64v = zeros(32), zeros(64)
    tabs = (_gained_tables(cosa, sina, jnp.tile(g_q_gqa, (1, 2)), 16, GQA_SCALE * LOG2E)
            + _gained_tables(cosa, sina, jnp.tile(g_k_gqa, (1, 2)), 16, 1.0)
            + _gained_tables(cosb, sinb, jnp.concatenate([g_q_nope, g_q_rope, z32], axis=-1), 8, MLA_SCALE * LOG2E)
            + _gained_tables(cosb, sinb, jnp.concatenate([z64v, g_k_rope, z32], axis=-1), 8, 1.0))
    wts = [wa, wuq.astype(BF16), wkn.astype(BF16), wvt.astype(BF16), g_cq[:, None, :], g_ckv[:, None, :], gkn]
    return wts, tabs, gates.astype(BF16)


def kernel(x, c, ctx, c_ctx, w_mod, b_mod, g_norm1, g_norm2, w_in, g_q_gqa, g_k_gqa, g_cq, g_ckv, w_uq, w_ukv,
           g_q_nope, g_k_nope, g_q_rope, g_k_rope, b_gate, w_br_a, w_br_b, w_br_c, w_out, w_ffn_in, w_ffn_out):
    b, n, d = x.shape
    nc = ctx.shape[1]
    depth = w_mod.shape[0]
    assert n % TILE == 0 and nc == TILE and b + 1 <= MOD_ROWS and n % GRID_W == 0
    n_lat_tiles, n_tiles = n // TILE, (n + nc) // TILE

    cc = jnp.concatenate([c, c_ctx[None], jnp.zeros((MOD_ROWS - b - 1, d), F32)], axis=0)
    mod = _modulation(cc, w_mod, b_mod).reshape(depth, MOD_ROWS, 6, d)
    dn, dc, chan = _dft_tables(n, nc)
    segs = _seg_matrices()
    bounds = _score_bounds(g_q_gqa, g_k_gqa, g_q_nope, g_k_nope, g_q_rope, g_k_rope)
    wts, tabs, wg = _inproj_weights(_rope_tables(n, nc), w_in, g_q_gqa, g_k_gqa, g_cq, g_ckv, w_uq, w_ukv,
                                    g_q_nope, g_k_nope, g_q_rope, g_k_rope)
    g1, g2 = g_norm1[:, None, :], g_norm2[:, None, :]
    mw = [wg, b_gate[:, None, :], w_br_a.astype(BF16), w_br_b.astype(BF16), w_br_c.astype(BF16), w_out.astype(BF16)]
    wi, wo = w_ffn_in.astype(BF16), w_ffn_out.astype(BF16)
    gqa_q = (0, 0, 128, 128, 256, 256, 384, 384)
    gqa_k = (0, 128, 0, 128, 256, 384, 256, 384)
    gqa_v = (0, 1, 0, 1, 2, 3, 2, 3)
    mla_o = tuple(range(0, 1024, 128))
    mla_h = tuple(range(MLA_HEADS))

    x_lat, x_ctx, ctx_tile = x, ctx, 0
    for l in range(depth):
        n_out = n_lat_tiles if l == depth - 1 else n_tiles
        qa, ka, vat, qb, kb, vbt, pf = _inproj(x_lat, x_ctx, ctx_tile, mod, l, n_lat_tiles, g1, wts, tabs, segs)
        ya = _attention(bounds[l, 0:1], qa, ka, vat, n_lat_tiles, n_out, gqa_q, gqa_k, gqa_v, "attn_gqa")
        yb = _attention(bounds[l, 1:2], qb, kb, vbt, n_lat_tiles, n_out, mla_o, mla_o, mla_h, "attn_mla")
        yc = _fourier(pf, dn, dc, chan, n_lat_tiles, n_out)
        xm = _merge(x_lat, x_ctx, ctx_tile, mod, l, n_lat_tiles, n_out, g1, ya, yb, yc, mw)
        xall = _ffn(xm, mod, l, n_lat_tiles, n_out, g2, wi, wo)
        x_lat, x_ctx, ctx_tile = xall, xall, n_lat_tiles
    return xall
```

```python
import functools
import math

import numpy as np
import jax
import jax.numpy as jnp
from jax import lax
from jax.experimental import pallas as pl
from jax.experimental.pallas import tpu as pltpu

GRID_W = 64
ROPE_THETA = 10000.0
EPS = 1e-6
GQA_HEADS = 8
GQA_KV_HEADS = 2
GQA_HEAD_DIM = 64
GQA_SCALE = GQA_HEAD_DIM ** -0.5
MLA_HEADS = 8
MLA_Q_RANK = 384
MLA_KV_RANK = 256
MLA_NOPE_DIM = 64
MLA_ROPE_DIM = 32
MLA_V_DIM = 64
MLA_SCALE = (MLA_NOPE_DIM + MLA_ROPE_DIM) ** -0.5
FOURIER_GROUPS = 4
FOURIER_GROUP_DIM = 128
FOURIER_WIDTH = FOURIER_GROUPS * FOURIER_GROUP_DIM
N_BRANCH = 3
LOG2E = math.log2(math.e)

LANES = 128
TILE = 256
MOD_ROWS = 16
VMEM_LIMIT = 56 * 1024 * 1024

BF16 = jnp.bfloat16
F32 = jnp.float32

_ONES_ROW_EVEN, _ONES_ROW_ODD = LANES - 1, 0

_Q0, _K0, _V0, _CQ0, _KR0, _CKV0, _F0, _WA = 0, 512, 640, 768, 1152, 1280, 1536, 2048


def _cparams(n_axes):
    return pltpu.CompilerParams(dimension_semantics=("arbitrary",) * n_axes, vmem_limit_bytes=VMEM_LIMIT)


def _dot(a, b):
    return jnp.dot(a, b, preferred_element_type=F32)


def _rms_rows(xv, g):
    return xv * lax.rsqrt(jnp.mean(xv * xv, axis=-1, keepdims=True) + EPS) * g


def _modulated(x, g, shift, scale):
    r = lax.rsqrt(jnp.mean(x * x, axis=-1, keepdims=True) + EPS)
    return (x * r * (g * (1.0 + scale)) + shift).astype(BF16)


def _layer_spec(a, l):
    return pl.BlockSpec((1,) + a.shape[1:], lambda *_: (l,) + (0,) * (a.ndim - 1))


def _const_spec(a):
    return pl.BlockSpec(a.shape, lambda *_: (0,) * a.ndim)


def _row_specs(x_lat, x_ctx, ctx_tile, n_lat_tiles):
    d = x_lat.shape[-1]
    return [pl.BlockSpec((1, TILE, d), lambda bi, ti: (bi, jnp.minimum(ti, n_lat_tiles - 1), 0)),
            pl.BlockSpec((1, TILE, d), lambda bi, ti: (bi, ctx_tile, 0))]


def _mod_spec(mod, l, n_batch, n_lat_tiles):
    return pl.BlockSpec((1, 1) + mod.shape[2:], lambda bi, ti: (l, jnp.where(ti >= n_lat_tiles, n_batch, bi), 0, 0))


def _tile_rows(xl_ref, xc_ref, n_lat_tiles):
    return jnp.where(pl.program_id(1) < n_lat_tiles, xl_ref[0], xc_ref[0])


def _mod_kernel(cc_ref, w_ref, b_ref, o_ref):
    s = cc_ref[...]
    s = s * jax.nn.sigmoid(s)
    o_ref[0] = _dot(s.astype(BF16), w_ref[0].astype(BF16)) + b_ref[0]


def _modulation(cc, w_mod, b_mod):
    depth, d, d6 = w_mod.shape
    bn = 1536
    return pl.pallas_call(
        _mod_kernel,
        grid=(depth, d6 // bn),
        in_specs=[pl.BlockSpec((MOD_ROWS, d), lambda l, j: (0, 0)),
                  pl.BlockSpec((1, d, bn), lambda l, j: (l, 0, j)),
                  pl.BlockSpec((1, 1, bn), lambda l, j: (l, 0, j))],
        out_specs=pl.BlockSpec((1, MOD_ROWS, bn), lambda l, j: (l, 0, j)),
        out_shape=jax.ShapeDtypeStruct((depth, MOD_ROWS, d6), F32),
        compiler_params=_cparams(2),
        name="modulation",
    )(cc, w_mod, b_mod.reshape(depth, 1, d6))


def _rot_half(xv, lane, seg):
    first = (lane & (2 * seg - 1)) < seg
    return jnp.where(first, pltpu.roll(xv, LANES - seg, 1), pltpu.roll(xv, seg, 1))


def _seg_rsqrt(xw, seg_ref):
    return lax.rsqrt(_dot((xw * xw).astype(BF16), seg_ref[...]) + EPS)


SUB = 3


def _inproj_kernel(*refs, n_lat_tiles):
    xl_refs, refs = refs[:SUB], refs[SUB:]
    xc_ref, refs = refs[0], refs[1:]
    mod_refs, refs = refs[:SUB], refs[SUB:]
    (g1_ref, wa_ref, wuq_ref, wkn_ref, wvt_ref, gcq_ref, gckv_ref, gkn_ref, tab_ref,
     sgqa_ref, sqb_ref, skn_ref, skk_ref, qa_ref, ka_ref, vat_ref, qb_ref, kb_ref, vbt_ref, pf_ref) = refs
    step = pl.program_id(1)
    lane = lax.broadcasted_iota(jnp.int32, (1, LANES), 1)
    lo = lane < 64
    hi = lane >= 64
    row = lax.broadcasted_iota(jnp.int32, (LANES, TILE), 0)
    blk = lambda a, j: a[:, j * LANES:(j + 1) * LANES]
    n_qblk = GQA_HEADS * GQA_HEAD_DIM // LANES
    w2 = 2 * LANES

    def project(j):
        rows = slice(j * TILE, (j + 1) * TILE)
        m = mod_refs[j][0, 0]
        x = jnp.where(step * SUB + j < n_lat_tiles, xl_refs[j][0], xc_ref[0])
        hb = _modulated(x, g1_ref[0], m[0:1], m[1:2])
        p_c = _dot(hb, wa_ref[0, :, _CQ0:_CKV0])
        p_ckv = _dot(hb, wa_ref[0, :, _CKV0:_F0])
        pq = _dot(hb, wa_ref[0, :, _Q0:_K0])
        pkv = _dot(hb, wa_ref[0, :, _K0:_CQ0])
        pf_ref[0, rows] = _dot(hb, wa_ref[0, :, _F0:_WA]).astype(BF16)
        cq = _rms_rows(p_c[:, :MLA_Q_RANK], gcq_ref[0]).astype(BF16)
        ckv = _rms_rows(p_ckv, gckv_ref[0])
        qb = _dot(cq, wuq_ref[0])
        kn_all = _dot(ckv.astype(BF16), wkn_ref[0])
        vbt = _dot(wvt_ref[0], ckv.T.astype(BF16))
        for h in range(MLA_HEADS):
            vh = vbt[h * LANES:(h + 1) * LANES]
            ones_row = _ONES_ROW_EVEN if h % 2 == 0 else _ONES_ROW_ODD
            vbt_ref[0, h, :, rows] = jnp.where(row == ones_row, 1.0, vh).astype(BF16)
        vt = pkv[:, LANES:].T
        for kvh in range(GQA_KV_HEADS):
            own = (row >= 64) == (kvh == 1)
            top = jnp.where(own, vt, 0.0) if kvh == 0 else pltpu.roll(jnp.where(own, vt, 0.0), 64, 0)
            bot = pltpu.roll(top, 64, 0)
            vat_ref[0, 2 * kvh, :, rows] = jnp.where(row == _ONES_ROW_EVEN, 1.0, top).astype(BF16)
            vat_ref[0, 2 * kvh + 1, :, rows] = jnp.where(row == _ONES_ROW_ODD, 1.0, bot).astype(BF16)
        return pq, pkv[:, :LANES], qb, kn_all, p_c[:, MLA_Q_RANK:]

    def finish(j, pq, xk, qb, kn_all, xkr):
        rows = slice(j * TILE, (j + 1) * TILE)
        r_qb = jnp.concatenate([_seg_rsqrt(qb[:, c * w2:(c + 1) * w2], sqb_ref) for c in range(MLA_HEADS // 2)],
                               axis=1)
        r_kn = jnp.concatenate([_seg_rsqrt(kn_all[:, c * w2:(c + 1) * w2], skn_ref) for c in range(MLA_HEADS // 2)],
                               axis=1)
        r_q = jnp.concatenate([_seg_rsqrt(pq[:, c * w2:(c + 1) * w2], sgqa_ref) for c in range(n_qblk // 2)], axis=1)
        r_kk = _seg_rsqrt(jnp.concatenate([xk, xkr], axis=1), skk_ref)
        r_k, r_kr = r_kk[:, :LANES], r_kk[:, LANES:]

        def roped(xb, table, seg, r):
            return (xb * tab_ref[0, table, rows] + _rot_half(xb, lane, seg) * tab_ref[0, N_TABLES + table, rows]) * r

        for h in range(MLA_HEADS):
            qb_ref[0, rows, h * LANES:(h + 1) * LANES] = roped(blk(qb, h), TAB_QB, 8, blk(r_qb, h)).astype(BF16)
        kr = roped(xkr, TAB_KR, 8, r_kr)
        for h in range(MLA_HEADS):
            kb_ref[0, rows, h * LANES:(h + 1) * LANES] = (blk(kn_all, h) * blk(r_kn, h) * gkn_ref[0] + kr).astype(BF16)
        for c in range(n_qblk):
            qa_ref[0, rows, c * LANES:(c + 1) * LANES] = roped(blk(pq, c), TAB_Q, 16, blk(r_q, c)).astype(BF16)
        kn = roped(xk, TAB_K, 16, r_k)
        sw = pltpu.roll(kn, 64, 1)
        for c, f in enumerate((jnp.where(lo, kn, 0.0), jnp.where(hi, sw, 0.0), jnp.where(lo, sw, 0.0),
                               jnp.where(hi, kn, 0.0))):
            ka_ref[0, rows, c * LANES:(c + 1) * LANES] = f.astype(BF16)

    pending = project(0)
    for j in range(SUB):
        nxt = project(j + 1) if j + 1 < SUB else None
        finish(j, *pending)
        pending = nxt


def _inproj(x_lat, x_ctx, ctx_tile, mod, l, n_lat_tiles, g1, wts, tabs, segs):
    b, _, d = x_lat.shape
    nt = n_lat_tiles + 1
    assert nt % SUB == 0
    t, step_rows = nt * TILE, SUB * TILE
    row = lambda bi, si: (bi, si, 0)
    rowt = lambda bi, si: (bi, 0, 0, si)
    tok = lambda w: (pl.BlockSpec((1, step_rows, w), row), jax.ShapeDtypeStruct((b, t, w), BF16))
    tra = lambda h: (pl.BlockSpec((1, h, LANES, step_rows), rowt), jax.ShapeDtypeStruct((b, h, LANES, t), BF16))
    outs = [tok(512), tok(512), tra(4), tok(1024), tok(1024), tra(MLA_HEADS), tok(512)]
    lat = lambda j: pl.BlockSpec((1, TILE, d), lambda bi, si: (bi, jnp.minimum(si * SUB + j, n_lat_tiles - 1), 0))
    modj = lambda j: pl.BlockSpec((1, 1) + mod.shape[2:],
                                  lambda bi, si: (l, jnp.where(si * SUB + j >= n_lat_tiles, b, bi), 0, 0))
    return pl.pallas_call(
        functools.partial(_inproj_kernel, n_lat_tiles=n_lat_tiles),
        grid=(b, nt // SUB),
        in_specs=[lat(j) for j in range(SUB)]
        + [pl.BlockSpec((1, TILE, d), lambda bi, si: (bi, ctx_tile, 0))]
        + [modj(j) for j in range(SUB)]
        + [_layer_spec(g1, l)] + [_layer_spec(w, l) for w in wts]
        + [pl.BlockSpec((1, 2 * N_TABLES, step_rows, LANES), lambda bi, si: (l, 0, si, 0))]
        + [_const_spec(s) for s in segs],
        out_specs=[o[0] for o in outs],
        out_shape=[o[1] for o in outs],
        compiler_params=_cparams(2),
        name="inproj",
    )(*([x_lat] * SUB), x_ctx, *([mod] * SUB), g1, *wts, tabs, *segs)


def _reduce_keys(a, op):
    nk, nq = a.shape
    part = op(a.reshape(nk // TILE, TILE, nq), axis=0)
    return op(part, axis=0, keepdims=True)


SHIFT_SAFE = 60.0
BOUND_SLACK = 1.0 + 2.0 ** -6


def _attn_kernel(shift_ref, q_ref, k_ref, vt_ref, o_ref, *, n_lat_tiles, with_ctx, qoff, koff, vidx):
    n_keys = k_ref.shape[1]
    n_pairs = len(qoff) // 2
    zero = jnp.zeros((TILE, LANES), BF16)
    bound = shift_ref[0]

    def scores(row0, pair, key0, nk):
        e, o = 2 * pair, 2 * pair + 1
        assert koff[o] == koff[e] + LANES
        q_e = q_ref[0, pl.ds(row0, TILE), qoff[e]:qoff[e] + LANES]
        q_o = q_ref[0, pl.ds(row0, TILE), qoff[o]:qoff[o] + LANES]
        qd = jnp.concatenate([jnp.concatenate([q_e, zero], axis=1), jnp.concatenate([zero, q_o], axis=1)], axis=0)
        k2 = k_ref[0, key0:key0 + nk, koff[e]:koff[e] + 2 * LANES]
        return lax.dot_general(k2, qd, (((1,), (1,)), ((), ())), preferred_element_type=F32)

    def tile(row0, key0, nk, use_bound):
        for pair in range(n_pairs):
            e, o = 2 * pair, 2 * pair + 1
            st = scores(row0, pair, key0, nk)
            pb = jnp.exp2(st - (bound if use_bound else _reduce_keys(st, jnp.max))).astype(BF16)
            ot_e = _dot(vt_ref[0, vidx[e], :, key0:key0 + nk], pb[:, :TILE])
            ot_o = _dot(vt_ref[0, vidx[o], :, key0:key0 + nk], pb[:, TILE:])
            ot_e = ot_e * (1.0 / ot_e[_ONES_ROW_EVEN:_ONES_ROW_EVEN + 1])
            ot_o = ot_o * (1.0 / ot_o[_ONES_ROW_ODD:_ONES_ROW_ODD + 1])
            ot = jnp.where(lax.broadcasted_iota(jnp.int32, (LANES, TILE), 0) < 64, ot_e, ot_o)
            o_ref[0, pl.ds(row0, TILE), pair * LANES:(pair + 1) * LANES] = ot.T.astype(BF16)

    def all_tiles(use_bound):
        def body(t, carry):
            tile(pl.multiple_of(t * TILE, TILE), 0, n_keys, use_bound)
            return carry
        lax.fori_loop(0, n_lat_tiles, body, 0)
        if with_ctx:
            lat = n_lat_tiles * TILE
            tile(lat, lat, n_keys - lat, use_bound)

    lax.cond(bound < SHIFT_SAFE, lambda: all_tiles(True), lambda: all_tiles(False))


def _attention(shift, q, k, vt, n_lat_tiles, n_q_tiles, qoff, koff, vidx, name):
    b, t, wq = q.shape
    wk = k.shape[2]
    nh = vt.shape[1]
    kern = functools.partial(_attn_kernel, n_lat_tiles=n_lat_tiles, with_ctx=n_q_tiles > n_lat_tiles,
                             qoff=qoff, koff=koff, vidx=vidx)
    return pl.pallas_call(
        kern,
        grid=(b,),
        in_specs=[pl.BlockSpec(memory_space=pltpu.SMEM),
                  pl.BlockSpec((1, t, wq), lambda bi: (bi, 0, 0)),
                  pl.BlockSpec((1, t, wk), lambda bi: (bi, 0, 0)),
                  pl.BlockSpec((1, nh, LANES, t), lambda bi: (bi, 0, 0, 0))],
        out_specs=pl.BlockSpec((1, n_q_tiles * TILE, 512), lambda bi: (bi, 0, 0)),
        out_shape=jax.ShapeDtypeStruct((b, n_q_tiles * TILE, 512), BF16),
        compiler_params=_cparams(1),
        name=name,
    )(shift, q, k, vt)


def _fourier_kernel(pf_ref, dn_ref, dc_ref, cs_ref, o_ref, ab_ref, abc_ref, *, n_lat_tiles):
    ti = pl.program_id(1)
    n_lat = n_lat_tiles * TILE
    w = FOURIER_WIDTH

    @pl.when(ti == 0)
    def _():
        for c in range(n_lat_tiles):
            ab = _dot(pf_ref[0, c * TILE:(c + 1) * TILE, :], cs_ref[...])
            ab_ref[c * TILE:(c + 1) * TILE, :] = ab[:, :w].astype(BF16)
            ab_ref[n_lat + c * TILE:n_lat + (c + 1) * TILE, :] = ab[:, w:].astype(BF16)

    @pl.when(ti < n_lat_tiles)
    def _():
        o_ref[0] = _dot(dn_ref[...], ab_ref[...]).astype(BF16)

    @pl.when(ti >= n_lat_tiles)
    def _():
        ab = _dot(pf_ref[0, n_lat:n_lat + TILE, :], cs_ref[...])
        abc_ref[0:TILE, :] = ab[:, :w].astype(BF16)
        abc_ref[TILE:2 * TILE, :] = ab[:, w:].astype(BF16)
        o_ref[0] = _dot(dc_ref[...], abc_ref[...]).astype(BF16)


def _fourier(pf, dn, dc, cs, n_lat_tiles, n_tiles):
    b, t, w = pf.shape
    n_lat = n_lat_tiles * TILE
    kern = functools.partial(_fourier_kernel, n_lat_tiles=n_lat_tiles)
    return pl.pallas_call(
        kern,
        grid=(b, n_tiles),
        in_specs=[pl.BlockSpec((1, t, w), lambda bi, ti: (bi, 0, 0)),
                  pl.BlockSpec((TILE, 2 * n_lat), lambda bi, ti: (jnp.minimum(ti, n_lat_tiles - 1), 0)),
                  _const_spec(dc), _const_spec(cs)],
        out_specs=pl.BlockSpec((1, TILE, w), lambda bi, ti: (bi, ti, 0)),
        out_shape=jax.ShapeDtypeStruct((b, n_tiles * TILE, w), BF16),
        scratch_shapes=[pltpu.VMEM((2 * n_lat, w), BF16), pltpu.VMEM((2 * TILE, w), BF16)],
        compiler_params=_cparams(2),
        name="fourier",
    )(pf, dn, dc, cs)


def _merge_kernel(xl_ref, xc_ref, mod_ref, g1_ref, ya_ref, yb_ref, yc_ref, wg_ref, bg_ref, wa_ref, wb_ref, wc_ref,
                  wo_ref, o_ref, *, n_lat_tiles):
    x = _tile_rows(xl_ref, xc_ref, n_lat_tiles)
    m = mod_ref[0, 0]
    d = x.shape[-1]
    hb = _modulated(x, g1_ref[0], m[0:1], m[1:2])
    acc = None
    for i, (y_ref, w_ref) in enumerate(((ya_ref, wa_ref), (yb_ref, wb_ref), (yc_ref, wc_ref))):
        gate = jax.nn.sigmoid(_dot(hb, wg_ref[0, :, i * d:(i + 1) * d]) + bg_ref[0, :, i * d:(i + 1) * d])
        term = gate * _dot(y_ref[0], w_ref[0])
        acc = term if acc is None else acc + term
    o_ref[0] = x + m[2:3] * _dot(acc.astype(BF16), wo_ref[0])


def _merge(x_lat, x_ctx, ctx_tile, mod, l, n_lat_tiles, n_tiles, g1, ya, yb, yc, wts):
    b, _, d = x_lat.shape
    row = lambda bi, ti: (bi, ti, 0)
    return pl.pallas_call(
        functools.partial(_merge_kernel, n_lat_tiles=n_lat_tiles),
        grid=(b, n_tiles),
        in_specs=_row_specs(x_lat, x_ctx, ctx_tile, n_lat_tiles)
        + [_mod_spec(mod, l, b, n_lat_tiles), _layer_spec(g1, l)]
        + [pl.BlockSpec((1, TILE, 512), row)] * 3 + [_layer_spec(w, l) for w in wts],
        out_specs=pl.BlockSpec((1, TILE, d), row),
        out_shape=jax.ShapeDtypeStruct((b, n_tiles * TILE, d), F32),
        compiler_params=_cparams(2),
        name="merge",
    )(x_lat, x_ctx, mod, g1, ya, yb, yc, *wts)


def _ffn_kernel(x_ref, mod_ref, g2_ref, wi_ref, wo_ref, o_ref):
    x = x_ref[0]
    m = mod_ref[0, 0]
    hb = _modulated(x, g2_ref[0], m[3:4], m[4:5])
    gu = _dot(hb, wi_ref[0])
    hid = gu.shape[-1] // 2
    gate, up = gu[:, :hid], gu[:, hid:]
    act = (gate * jax.nn.sigmoid(gate) * up).astype(BF16)
    o_ref[0] = x + m[5:6] * _dot(act, wo_ref[0])


def _ffn(xm, mod, l, n_lat_tiles, n_tiles, g2, wi, wo):
    b, _, d = xm.shape
    row = lambda bi, ti: (bi, ti, 0)
    return pl.pallas_call(
        _ffn_kernel,
        grid=(b, n_tiles),
        in_specs=[pl.BlockSpec((1, TILE, d), row), _mod_spec(mod, l, b, n_lat_tiles), _layer_spec(g2, l),
                  _layer_spec(wi, l), _layer_spec(wo, l)],
        out_specs=pl.BlockSpec((1, TILE, d), row),
        out_shape=jax.ShapeDtypeStruct((b, n_tiles * TILE, d), F32),
        compiler_params=_cparams(2),
        name="ffn",
    )(xm, mod, g2, wi, wo)


def _partner(seg):
    lane = np.arange(LANES)
    return np.where((lane % (2 * seg)) < seg, lane + seg, lane - seg)


def _rope_tables(n_lat, n_ctx):
    rows = n_lat // GRID_W
    row_id = np.repeat(np.arange(rows), GRID_W).astype(np.float64)
    col_id = np.tile(np.arange(GRID_W), rows).astype(np.float64)

    def angles(dim):
        half = dim // 2
        freqs = ROPE_THETA ** (-np.arange(0, half, 2, dtype=np.float64) / half)
        ax = lambda pos: np.concatenate([pos[:, None] * freqs[None, :]] * 2, axis=-1)
        return np.concatenate([ax(row_id), ax(col_id)], axis=-1)

    def signed(sin, dim):
        sign = np.where((np.arange(dim) % (dim // 2)) < dim // 4, -1.0, 1.0)
        return sin * sign[None, :]

    a64, a32 = angles(GQA_HEAD_DIM), angles(MLA_ROPE_DIM)
    t = n_lat + n_ctx
    cosa, sina = np.ones((t, LANES)), np.zeros((t, LANES))
    cosa[:n_lat] = np.tile(np.cos(a64), (1, 2))
    sina[:n_lat] = np.tile(signed(np.sin(a64), GQA_HEAD_DIM), (1, 2))
    cosb, sinb = np.ones((t, LANES)), np.zeros((t, LANES))
    cosb[:n_lat, 64:96] = np.cos(a32)
    sinb[:n_lat, 64:96] = signed(np.sin(a32), MLA_ROPE_DIM)
    return [jnp.asarray(a, F32) for a in (cosa, sina, cosb, sinb)]


TAB_Q, TAB_K, TAB_QB, TAB_KR, N_TABLES = 0, 1, 2, 3, 4


def _gained_tables(rope, gains, scales):
    cosa, sina, cosb, sinb = rope
    cos = jnp.stack([cosa, cosa, cosb, cosb])
    sin = jnp.stack([sina, sina, sinb, sinb])
    g = jnp.stack(gains, axis=1) * jnp.asarray(scales, F32)[None, :, None]
    partner = np.stack([_partner(16), _partner(16), _partner(8), _partner(8)])
    g_sin = jnp.take_along_axis(g, jnp.asarray(partner)[None], axis=-1)
    return jnp.concatenate([cos[None] * g[:, :, None, :], sin[None] * g_sin[:, :, None, :]], axis=1)


def _seg_matrices():
    def blockdiag(segs):
        m = np.zeros((2 * LANES, 2 * LANES))
        for start, n in segs:
            m[start:start + n, start:start + n] = 1.0 / n
        return m

    gqa = blockdiag([(s0, 64) for s0 in range(0, 256, 64)])
    qb = blockdiag([(0, 64), (64, 32), (128, 64), (192, 32)])
    kn = blockdiag([(0, 64), (128, 64)])
    kk = blockdiag([(0, 64), (64, 64), (192, 32)])
    return [jnp.asarray(a, F32).astype(BF16) for a in (gqa, qb, kn, kk)]


def _score_bounds(g_q_gqa, g_k_gqa, g_q_nope, g_k_nope, g_q_rope, g_k_rope):
    top = lambda g: jnp.max(g * g, axis=-1)
    gqa = jnp.sqrt(GQA_HEAD_DIM * top(g_q_gqa) * GQA_HEAD_DIM * top(g_k_gqa)) * (GQA_SCALE * LOG2E)
    mla = jnp.sqrt((MLA_NOPE_DIM * top(g_q_nope) + MLA_ROPE_DIM * top(g_q_rope))
                   * (MLA_NOPE_DIM * top(g_k_nope) + MLA_ROPE_DIM * top(g_k_rope))) * (MLA_SCALE * LOG2E)
    return jnp.stack([gqa, mla], axis=-1) * BOUND_SLACK


def _dft_tables(n_lat, n_ctx):
    def cs(n):
        j = np.arange(n)
        ang = 2.0 * np.pi * ((j[:, None] * j[None, :]) % n) / n
        return np.cos(ang) / np.sqrt(n), np.sin(ang) / np.sqrt(n)

    cn, sn = cs(n_lat)
    cc, sc = cs(n_ctx)
    cg, sg = cs(FOURIER_GROUP_DIM)
    eye = np.eye(FOURIER_GROUPS)
    dn = np.concatenate([cn, -sn], axis=1)
    dc = np.concatenate([cc, -sc], axis=1)
    chan = np.concatenate([np.kron(eye, cg), np.kron(eye, sg)], axis=1)
    return [jnp.asarray(a, F32).astype(BF16) for a in (dn, dc, chan)]


def _inproj_weights(rope, w_in, g_q_gqa, g_k_gqa, g_cq, g_ckv, w_uq, w_ukv, g_q_nope, g_k_nope, g_q_rope, g_k_rope):
    depth, d, _ = w_in.shape
    q, k, v, cq, ckv, kr, f, gates = (w_in[..., 0:512], w_in[..., 512:640], w_in[..., 640:768], w_in[..., 768:1152],
                                      w_in[..., 1152:1408], w_in[..., 1408:1440], w_in[..., 1440:1952],
                                      w_in[..., 1952:])
    zeros = lambda *s: jnp.zeros((depth,) + s, F32)
    krblk = jnp.concatenate([zeros(d, 64), kr, zeros(d, 32)], axis=-1)
    wa = jnp.concatenate([q, k, v, cq, krblk, ckv, f], axis=-1).astype(BF16)
    uq = w_uq.reshape(depth, MLA_Q_RANK, MLA_HEADS, MLA_NOPE_DIM + MLA_ROPE_DIM)
    wuq = jnp.concatenate([uq, zeros(MLA_Q_RANK, MLA_HEADS, 32)], axis=-1).reshape(depth, MLA_Q_RANK, -1)
    ukv = w_ukv.reshape(depth, MLA_KV_RANK, MLA_HEADS, MLA_NOPE_DIM + MLA_V_DIM)
    kn, vv = ukv[..., :MLA_NOPE_DIM], ukv[..., MLA_NOPE_DIM:]
    z64 = zeros(MLA_KV_RANK, MLA_HEADS, 64)
    wkn = jnp.concatenate([kn, z64], axis=-1).reshape(depth, MLA_KV_RANK, -1)
    even = (jnp.arange(MLA_HEADS) % 2 == 0)[None, None, :, None]
    wv = jnp.where(even, jnp.concatenate([vv, z64], axis=-1), jnp.concatenate([z64, vv], axis=-1))
    wvt = jnp.swapaxes(wv.reshape(depth, MLA_KV_RANK, -1), 1, 2)
    gkn = jnp.concatenate([g_k_nope, zeros(64)], axis=-1)[:, None, :]
    z32, z64v = zeros(32), zeros(64)
    tabs = _gained_tables(rope, [jnp.tile(g_q_gqa, (1, 2)), jnp.tile(g_k_gqa, (1, 2)),
                                 jnp.concatenate([g_q_nope, g_q_rope, z32], axis=-1),
                                 jnp.concatenate([z64v, g_k_rope, z32], axis=-1)],
                          [GQA_SCALE * LOG2E, 1.0, MLA_SCALE * LOG2E, 1.0])
    wts = [wa, wuq.astype(BF16), wkn.astype(BF16), wvt.astype(BF16), g_cq[:, None, :], g_ckv[:, None, :], gkn]
    return wts, tabs, gates.astype(BF16)


def kernel(x, c, ctx, c_ctx, w_mod, b_mod, g_norm1, g_norm2, w_in, g_q_gqa, g_k_gqa, g_cq, g_ckv, w_uq, w_ukv,
           g_q_nope, g_k_nope, g_q_rope, g_k_rope, b_gate, w_br_a, w_br_b, w_br_c, w_out, w_ffn_in, w_ffn_out):
    b, n, d = x.shape
    nc = ctx.shape[1]
    depth = w_mod.shape[0]
    assert n % TILE == 0 and nc == TILE and b + 1 <= MOD_ROWS and n % GRID_W == 0
    n_lat_tiles, n_tiles = n // TILE, (n + nc) // TILE

    cc = jnp.concatenate([c, c_ctx[None], jnp.zeros((MOD_ROWS - b - 1, d), F32)], axis=0)
    mod = _modulation(cc, w_mod, b_mod).reshape(depth, MOD_ROWS, 6, d)
    dn, dc, chan = _dft_tables(n, nc)
    segs = _seg_matrices()
    bounds = _score_bounds(g_q_gqa, g_k_gqa, g_q_nope, g_k_nope, g_q_rope, g_k_rope)
    wts, tabs, wg = _inproj_weights(_rope_tables(n, nc), w_in, g_q_gqa, g_k_gqa, g_cq, g_ckv, w_uq, w_ukv,
                                    g_q_nope, g_k_nope, g_q_rope, g_k_rope)
    g1, g2 = g_norm1[:, None, :], g_norm2[:, None, :]
    mw = [wg, b_gate[:, None, :], w_br_a.astype(BF16), w_br_b.astype(BF16), w_br_c.astype(BF16), w_out.astype(BF16)]
    wi, wo = w_ffn_in.astype(BF16), w_ffn_out.astype(BF16)
    gqa_q = (0, 0, 128, 128, 256, 256, 384, 384)
    gqa_k = (0, 128, 0, 128, 256, 384, 256, 384)
    gqa_v = (0, 1, 0, 1, 2, 3, 2, 3)
    mla_o = tuple(range(0, 1024, 128))
    mla_h = tuple(range(MLA_HEADS))

    x_lat, x_ctx, ctx_tile = x, ctx, 0
    for l in range(depth):
        n_out = n_lat_tiles if l == depth - 1 else n_tiles
        qa, ka, vat, qb, kb, vbt, pf = _inproj(x_lat, x_ctx, ctx_tile, mod, l, n_lat_tiles, g1, wts, tabs, segs)
        ya = _attention(bounds[l, 0:1], qa, ka, vat, n_lat_tiles, n_out, gqa_q, gqa_k, gqa_v, "attn_gqa")
        yb = _attention(bounds[l, 1:2], qb, kb, vbt, n_lat_tiles, n_out, mla_o, mla_o, mla_h, "attn_mla")
        yc = _fourier(pf, dn, dc, chan, n_lat_tiles, n_out)
        xm = _merge(x_lat, x_ctx, ctx_tile, mod, l, n_lat_tiles, n_out, g1, ya, yb, yc, mw)
        xall = _ffn(xm, mod, l, n_lat_tiles, n_out, g2, wi, wo)
        x_lat, x_ctx, ctx_tile = xall, xall, n_lat_tiles
    return xall
```

```python
import functools
import math

import numpy as np
import jax
import jax.numpy as jnp
from jax import lax
from jax.experimental import pallas as pl
from jax.experimental.pallas import tpu as pltpu

GRID_W = 64
ROPE_THETA = 10000.0
EPS = 1e-6
GQA_HEADS = 8
GQA_KV_HEADS = 2
GQA_HEAD_DIM = 64
GQA_SCALE = GQA_HEAD_DIM ** -0.5
MLA_HEADS = 8
MLA_Q_RANK = 384
MLA_KV_RANK = 256
MLA_NOPE_DIM = 64
MLA_ROPE_DIM = 32
MLA_V_DIM = 64
MLA_SCALE = (MLA_NOPE_DIM + MLA_ROPE_DIM) ** -0.5
FOURIER_GROUPS = 4
FOURIER_GROUP_DIM = 128
FOURIER_WIDTH = FOURIER_GROUPS * FOURIER_GROUP_DIM
N_BRANCH = 3
LOG2E = math.log2(math.e)

LANES = 128
TILE = 256
MOD_ROWS = 16
VMEM_LIMIT = 56 * 1024 * 1024

BF16 = jnp.bfloat16
F32 = jnp.float32

_ONES_ROW_EVEN, _ONES_ROW_ODD = LANES - 1, 0

_Q0, _K0, _V0, _CQ0, _KR0, _CKV0, _F0, _WA = 0, 512, 640, 768, 1152, 1280, 1536, 2048


def _cparams(n_axes):
    return pltpu.CompilerParams(dimension_semantics=("arbitrary",) * n_axes, vmem_limit_bytes=VMEM_LIMIT)


def _dot(a, b):
    return jnp.dot(a, b, preferred_element_type=F32)


def _rms_rows(xv, g):
    return xv * lax.rsqrt(jnp.mean(xv * xv, axis=-1, keepdims=True) + EPS) * g


def _modulated(x, g, shift, scale):
    r = lax.rsqrt(jnp.mean(x * x, axis=-1, keepdims=True) + EPS)
    return (x * r * (g * (1.0 + scale)) + shift).astype(BF16)


def _layer_spec(a, l):
    return pl.BlockSpec((1,) + a.shape[1:], lambda *_: (l,) + (0,) * (a.ndim - 1))


def _const_spec(a):
    return pl.BlockSpec(a.shape, lambda *_: (0,) * a.ndim)


def _row_specs(x_lat, x_ctx, ctx_tile, n_lat_tiles):
    d = x_lat.shape[-1]
    return [pl.BlockSpec((1, TILE, d), lambda bi, ti: (bi, jnp.minimum(ti, n_lat_tiles - 1), 0)),
            pl.BlockSpec((1, TILE, d), lambda bi, ti: (bi, ctx_tile, 0))]


def _mod_spec(mod, l, n_batch, n_lat_tiles):
    return pl.BlockSpec((1, 1) + mod.shape[2:], lambda bi, ti: (l, jnp.where(ti >= n_lat_tiles, n_batch, bi), 0, 0))


def _tile_rows(xl_ref, xc_ref, n_lat_tiles):
    return jnp.where(pl.program_id(1) < n_lat_tiles, xl_ref[0], xc_ref[0])


def _mod_kernel(cc_ref, w_ref, b_ref, o_ref):
    s = cc_ref[...]
    s = s * jax.nn.sigmoid(s)
    o_ref[0] = _dot(s.astype(BF16), w_ref[0].astype(BF16)) + b_ref[0]


def _modulation(cc, w_mod, b_mod):
    depth, d, d6 = w_mod.shape
    bn = 1536
    return pl.pallas_call(
        _mod_kernel,
        grid=(depth, d6 // bn),
        in_specs=[pl.BlockSpec((MOD_ROWS, d), lambda l, j: (0, 0)),
                  pl.BlockSpec((1, d, bn), lambda l, j: (l, 0, j)),
                  pl.BlockSpec((1, 1, bn), lambda l, j: (l, 0, j))],
        out_specs=pl.BlockSpec((1, MOD_ROWS, bn), lambda l, j: (l, 0, j)),
        out_shape=jax.ShapeDtypeStruct((depth, MOD_ROWS, d6), F32),
        compiler_params=_cparams(2),
        name="modulation",
    )(cc, w_mod, b_mod.reshape(depth, 1, d6))


def _rot_half(xv, lane, seg):
    first = (lane & (2 * seg - 1)) < seg
    return jnp.where(first, pltpu.roll(xv, LANES - seg, 1), pltpu.roll(xv, seg, 1))


def _seg_rsqrt(xw, seg_ref):
    return lax.rsqrt(_dot((xw * xw).astype(BF16), seg_ref[...]) + EPS)


SUB = 3


def _inproj_kernel(*refs, n_lat_tiles):
    xl_refs, refs = refs[:SUB], refs[SUB:]
    xc_ref, refs = refs[0], refs[1:]
    mod_refs, refs = refs[:SUB], refs[SUB:]
    (g1_ref, wa_ref, wuq_ref, wkn_ref, wvt_ref, gcq_ref, gckv_ref, gkn_ref, tab_ref,
     sgqa_ref, sqb_ref, skn_ref, skk_ref, qa_ref, ka_ref, vat_ref, qb_ref, kb_ref, vbt_ref, pf_ref) = refs
    step = pl.program_id(1)
    lane = lax.broadcasted_iota(jnp.int32, (1, LANES), 1)
    lo = lane < 64
    hi = lane >= 64
    row = lax.broadcasted_iota(jnp.int32, (LANES, TILE), 0)
    blk = lambda a, j: a[:, j * LANES:(j + 1) * LANES]
    n_qblk = GQA_HEADS * GQA_HEAD_DIM // LANES
    w2 = 2 * LANES

    def project(j):
        rows = slice(j * TILE, (j + 1) * TILE)
        m = mod_refs[j][0, 0]
        x = jnp.where(step * SUB + j < n_lat_tiles, xl_refs[j][0], xc_ref[0])
        hb = _modulated(x, g1_ref[0], m[0:1], m[1:2])
        p_c = _dot(hb, wa_ref[0, :, _CQ0:_CKV0])
        p_ckv = _dot(hb, wa_ref[0, :, _CKV0:_F0])
        pq = _dot(hb, wa_ref[0, :, _Q0:_K0])
        pkv = _dot(hb, wa_ref[0, :, _K0:_CQ0])
        pf_ref[0, rows] = _dot(hb, wa_ref[0, :, _F0:_WA]).astype(BF16)
        cq = _rms_rows(p_c[:, :MLA_Q_RANK], gcq_ref[0]).astype(BF16)
        ckv = _rms_rows(p_ckv, gckv_ref[0])
        qb = _dot(cq, wuq_ref[0])
        kn_all = _dot(ckv.astype(BF16), wkn_ref[0])
        vbt = _dot(wvt_ref[0], ckv.T.astype(BF16))
        for h in range(MLA_HEADS):
            vh = vbt[h * LANES:(h + 1) * LANES]
            ones_row = _ONES_ROW_EVEN if h % 2 == 0 else _ONES_ROW_ODD
            vbt_ref[0, h, :, rows] = jnp.where(row == ones_row, 1.0, vh).astype(BF16)
        vt = pkv[:, LANES:].T
        for kvh in range(GQA_KV_HEADS):
            own = (row >= 64) == (kvh == 1)
            top = jnp.where(own, vt, 0.0) if kvh == 0 else pltpu.roll(jnp.where(own, vt, 0.0), 64, 0)
            bot = pltpu.roll(top, 64, 0)
            vat_ref[0, 2 * kvh, :, rows] = jnp.where(row == _ONES_ROW_EVEN, 1.0, top).astype(BF16)
            vat_ref[0, 2 * kvh + 1, :, rows] = jnp.where(row == _ONES_ROW_ODD, 1.0, bot).astype(BF16)
        return pq, pkv[:, :LANES], qb, kn_all, p_c[:, MLA_Q_RANK:]

    def finish(j, pq, xk, qb, kn_all, xkr):
        rows = slice(j * TILE, (j + 1) * TILE)
        r_qb = jnp.concatenate([_seg_rsqrt(qb[:, c * w2:(c + 1) * w2], sqb_ref) for c in range(MLA_HEADS // 2)],
                               axis=1)
        r_kn = jnp.concatenate([_seg_rsqrt(kn_all[:, c * w2:(c + 1) * w2], skn_ref) for c in range(MLA_HEADS // 2)],
                               axis=1)
        r_q = jnp.concatenate([_seg_rsqrt(pq[:, c * w2:(c + 1) * w2], sgqa_ref) for c in range(n_qblk // 2)], axis=1)
        r_kk = _seg_rsqrt(jnp.concatenate([xk, xkr], axis=1), skk_ref)
        r_k, r_kr = r_kk[:, :LANES], r_kk[:, LANES:]

        def roped(xb, table, seg, r):
            return (xb * tab_ref[0, table, rows] + _rot_half(xb, lane, seg) * tab_ref[0, N_TABLES + table, rows]) * r

        for h in range(MLA_HEADS):
            qb_ref[0, rows, h * LANES:(h + 1) * LANES] = roped(blk(qb, h), TAB_QB, 8, blk(r_qb, h)).astype(BF16)
        kr = roped(xkr, TAB_KR, 8, r_kr)
        for h in range(MLA_HEADS):
            kb_ref[0, rows, h * LANES:(h + 1) * LANES] = (blk(kn_all, h) * blk(r_kn, h) * gkn_ref[0] + kr).astype(BF16)
        for c in range(n_qblk):
            qa_ref[0, rows, c * LANES:(c + 1) * LANES] = roped(blk(pq, c), TAB_Q, 16, blk(r_q, c)).astype(BF16)
        kn = roped(xk, TAB_K, 16, r_k)
        sw = pltpu.roll(kn, 64, 1)
        for c, f in enumerate((jnp.where(lo, kn, 0.0), jnp.where(hi, sw, 0.0), jnp.where(lo, sw, 0.0),
                               jnp.where(hi, kn, 0.0))):
            ka_ref[0, rows, c * LANES:(c + 1) * LANES] = f.astype(BF16)

    pending = project(0)
    for j in range(SUB):
        nxt = project(j + 1) if j + 1 < SUB else None
        finish(j, *pending)
        pending = nxt


def _inproj(x_lat, x_ctx, ctx_tile, mod, l, n_lat_tiles, g1, wts, tabs, segs):
    b, _, d = x_lat.shape
    nt = n_lat_tiles + 1
    assert nt % SUB == 0
    t, step_rows = nt * TILE, SUB * TILE
    row = lambda bi, si: (bi, si, 0)
    rowt = lambda bi, si: (bi, 0, 0, si)
    tok = lambda w: (pl.BlockSpec((1, step_rows, w), row), jax.ShapeDtypeStruct((b, t, w), BF16))
    tra = lambda h: (pl.BlockSpec((1, h, LANES, step_rows), rowt), jax.ShapeDtypeStruct((b, h, LANES, t), BF16))
    outs = [tok(512), tok(512), tra(4), tok(1024), tok(1024), tra(MLA_HEADS), tok(512)]
    lat = lambda j: pl.BlockSpec((1, TILE, d), lambda bi, si: (bi, jnp.minimum(si * SUB + j, n_lat_tiles - 1), 0))
    modj = lambda j: pl.BlockSpec((1, 1) + mod.shape[2:],
                                  lambda bi, si: (l, jnp.where(si * SUB + j >= n_lat_tiles, b, bi), 0, 0))
    return pl.pallas_call(
        functools.partial(_inproj_kernel, n_lat_tiles=n_lat_tiles),
        grid=(b, nt // SUB),
        in_specs=[lat(j) for j in range(SUB)]
        + [pl.BlockSpec((1, TILE, d), lambda bi, si: (bi, ctx_tile, 0))]
        + [modj(j) for j in range(SUB)]
        + [_layer_spec(g1, l)] + [_layer_spec(w, l) for w in wts]
        + [pl.BlockSpec((1, 2 * N_TABLES, step_rows, LANES), lambda bi, si: (l, 0, si, 0))]
        + [_const_spec(s) for s in segs],
        out_specs=[o[0] for o in outs],
        out_shape=[o[1] for o in outs],
        compiler_params=_cparams(2),
        name="inproj",
    )(*([x_lat] * SUB), x_ctx, *([mod] * SUB), g1, *wts, tabs, *segs)


def _reduce_keys(a, op):
    nk, nq = a.shape
    part = op(a.reshape(nk // TILE, TILE, nq), axis=0)
    return op(part, axis=0, keepdims=True)


SHIFT_SAFE = 60.0
BOUND_SLACK = 1.0 + 2.0 ** -6


def _attn_kernel(shift_ref, q_ref, k_ref, vt_ref, o_ref, *, n_lat_tiles, with_ctx, qoff, koff, vidx):
    n_keys = k_ref.shape[1]
    n_pairs = len(qoff) // 2
    zero = jnp.zeros((TILE, LANES), BF16)
    bound = shift_ref[0]

    def scores(row0, pair, key0, nk):
        e, o = 2 * pair, 2 * pair + 1
        assert koff[o] == koff[e] + LANES
        q_e = q_ref[0, pl.ds(row0, TILE), qoff[e]:qoff[e] + LANES]
        q_o = q_ref[0, pl.ds(row0, TILE), qoff[o]:qoff[o] + LANES]
        qd = jnp.concatenate([jnp.concatenate([q_e, zero], axis=1), jnp.concatenate([zero, q_o], axis=1)], axis=0)
        k2 = k_ref[0, key0:key0 + nk, koff[e]:koff[e] + 2 * LANES]
        return lax.dot_general(k2, qd, (((1,), (1,)), ((), ())), preferred_element_type=F32)

    def tile(row0, key0, nk, use_bound):
        for pair in range(n_pairs):
            e, o = 2 * pair, 2 * pair + 1
            st = scores(row0, pair, key0, nk)
            pb = jnp.exp2(st - (bound if use_bound else _reduce_keys(st, jnp.max))).astype(BF16)
            ot_e = _dot(vt_ref[0, vidx[e], :, key0:key0 + nk], pb[:, :TILE])
            ot_o = _dot(vt_ref[0, vidx[o], :, key0:key0 + nk], pb[:, TILE:])
            ot_e = ot_e * (1.0 / ot_e[_ONES_ROW_EVEN:_ONES_ROW_EVEN + 1])
            ot_o = ot_o * (1.0 / ot_o[_ONES_ROW_ODD:_ONES_ROW_ODD + 1])
            ot = jnp.where(lax.broadcasted_iota(jnp.int32, (LANES, TILE), 0) < 64, ot_e, ot_o)
            o_ref[0, pl.ds(row0, TILE), pair * LANES:(pair + 1) * LANES] = ot.T.astype(BF16)

    def all_tiles(use_bound):
        def body(t, carry):
            tile(pl.multiple_of(t * TILE, TILE), 0, n_keys, use_bound)
            return carry
        lax.fori_loop(0, n_lat_tiles, body, 0)
        if with_ctx:
            lat = n_lat_tiles * TILE
            tile(lat, lat, n_keys - lat, use_bound)

    lax.cond(bound < SHIFT_SAFE, lambda: all_tiles(True), lambda: all_tiles(False))


def _attention(shift, q, k, vt, n_lat_tiles, n_q_tiles, qoff, koff, vidx, name):
    b, t, wq = q.shape
    wk = k.shape[2]
    nh = vt.shape[1]
    kern = functools.partial(_attn_kernel, n_lat_tiles=n_lat_tiles, with_ctx=n_q_tiles > n_lat_tiles,
                             qoff=qoff, koff=koff, vidx=vidx)
    return pl.pallas_call(
        kern,
        grid=(b,),
        in_specs=[pl.BlockSpec(memory_space=pltpu.SMEM),
                  pl.BlockSpec((1, t, wq), lambda bi: (bi, 0, 0)),
                  pl.BlockSpec((1, t, wk), lambda bi: (bi, 0, 0)),
                  pl.BlockSpec((1, nh, LANES, t), lambda bi: (bi, 0, 0, 0))],
        out_specs=pl.BlockSpec((1, n_q_tiles * TILE, 512), lambda bi: (bi, 0, 0)),
        out_shape=jax.ShapeDtypeStruct((b, n_q_tiles * TILE, 512), BF16),
        compiler_params=_cparams(1),
        name=name,
    )(shift, q, k, vt)


def _fourier_kernel(pf_ref, cn_ref, sn_ref, dc_ref, cs_ref, flip_ref, alt_ref, o_ref, a_ref, b_ref, *, n_lat_tiles,
                    with_ctx):
    n_lat = n_lat_tiles * TILE
    half_tiles = n_lat_tiles // 2
    w = FOURIER_WIDTH
    for c in range(n_lat_tiles):
        ab = _dot(pf_ref[0, c * TILE:(c + 1) * TILE, :], cs_ref[...])
        a_ref[c * TILE:(c + 1) * TILE, :] = ab[:, :w].astype(BF16)
        b_ref[c * TILE:(c + 1) * TILE, :] = ab[:, w:].astype(BF16)
    row0 = lax.broadcasted_iota(jnp.int32, (TILE, w), 0) == 0
    carry = _dot(alt_ref[...], a_ref[...])[0:1] * (1.0 / math.sqrt(n_lat))
    for m in reversed(range(half_tiles)):
        g = _dot(cn_ref[m * TILE:(m + 1) * TILE, :], a_ref[...])
        h = _dot(sn_ref[m * TILE:(m + 1) * TILE, :], b_ref[...])
        o_ref[0, m * TILE:(m + 1) * TILE, :] = (g - h).astype(BF16)
        mirrored = (g + h).astype(BF16)
        tile_rev = _dot(flip_ref[...], mirrored)
        bot = n_lat_tiles - 1 - m
        o_ref[0, bot * TILE:(bot + 1) * TILE, :] = jnp.where(row0, carry, tile_rev).astype(BF16)
        carry = mirrored[0:1].astype(F32)
    if with_ctx:
        ab = _dot(pf_ref[0, n_lat:n_lat + TILE, :], cs_ref[...])
        abc = jnp.concatenate([ab[:, :w], ab[:, w:]], axis=0).astype(BF16)
        o_ref[0, n_lat:n_lat + TILE, :] = _dot(dc_ref[...], abc).astype(BF16)


def _fourier(pf, tables, n_lat_tiles, n_tiles):
    b, t, w = pf.shape
    n_lat = n_lat_tiles * TILE
    kern = functools.partial(_fourier_kernel, n_lat_tiles=n_lat_tiles, with_ctx=n_tiles > n_lat_tiles)
    return pl.pallas_call(
        kern,
        grid=(b,),
        in_specs=[pl.BlockSpec((1, t, w), lambda bi: (bi, 0, 0))] + [_const_spec(a) for a in tables],
        out_specs=pl.BlockSpec((1, n_tiles * TILE, w), lambda bi: (bi, 0, 0)),
        out_shape=jax.ShapeDtypeStruct((b, n_tiles * TILE, w), BF16),
        scratch_shapes=[pltpu.VMEM((n_lat, w), BF16), pltpu.VMEM((n_lat, w), BF16)],
        compiler_params=_cparams(1),
        name="fourier",
    )(pf, *tables)


def _merge_kernel(xl_ref, xc_ref, mod_ref, g1_ref, ya_ref, yb_ref, yc_ref, wg_ref, bg_ref, wa_ref, wb_ref, wc_ref,
                  wo_ref, o_ref, *, n_lat_tiles):
    x = _tile_rows(xl_ref, xc_ref, n_lat_tiles)
    m = mod_ref[0, 0]
    d = x.shape[-1]
    hb = _modulated(x, g1_ref[0], m[0:1], m[1:2])
    acc = None
    for i, (y_ref, w_ref) in enumerate(((ya_ref, wa_ref), (yb_ref, wb_ref), (yc_ref, wc_ref))):
        gate = jax.nn.sigmoid(_dot(hb, wg_ref[0, :, i * d:(i + 1) * d]) + bg_ref[0, :, i * d:(i + 1) * d])
        term = gate * _dot(y_ref[0], w_ref[0])
        acc = term if acc is None else acc + term
    o_ref[0] = x + m[2:3] * _dot(acc.astype(BF16), wo_ref[0])


def _merge(x_lat, x_ctx, ctx_tile, mod, l, n_lat_tiles, n_tiles, g1, ya, yb, yc, wts):
    b, _, d = x_lat.shape
    row = lambda bi, ti: (bi, ti, 0)
    return pl.pallas_call(
        functools.partial(_merge_kernel, n_lat_tiles=n_lat_tiles),
        grid=(b, n_tiles),
        in_specs=_row_specs(x_lat, x_ctx, ctx_tile, n_lat_tiles)
        + [_mod_spec(mod, l, b, n_lat_tiles), _layer_spec(g1, l)]
        + [pl.BlockSpec((1, TILE, 512), row)] * 3 + [_layer_spec(w, l) for w in wts],
        out_specs=pl.BlockSpec((1, TILE, d), row),
        out_shape=jax.ShapeDtypeStruct((b, n_tiles * TILE, d), F32),
        compiler_params=_cparams(2),
        name="merge",
    )(x_lat, x_ctx, mod, g1, ya, yb, yc, *wts)


def _ffn_kernel(x_ref, mod_ref, g2_ref, wi_ref, wo_ref, o_ref):
    x = x_ref[0]
    m = mod_ref[0, 0]
    hb = _modulated(x, g2_ref[0], m[3:4], m[4:5])
    gu = _dot(hb, wi_ref[0])
    hid = gu.shape[-1] // 2
    gate, up = gu[:, :hid], gu[:, hid:]
    act = (gate * jax.nn.sigmoid(gate) * up).astype(BF16)
    o_ref[0] = x + m[5:6] * _dot(act, wo_ref[0])


def _ffn(xm, mod, l, n_lat_tiles, n_tiles, g2, wi, wo):
    b, _, d = xm.shape
    row = lambda bi, ti: (bi, ti, 0)
    return pl.pallas_call(
        _ffn_kernel,
        grid=(b, n_tiles),
        in_specs=[pl.BlockSpec((1, TILE, d), row), _mod_spec(mod, l, b, n_lat_tiles), _layer_spec(g2, l),
                  _layer_spec(wi, l), _layer_spec(wo, l)],
        out_specs=pl.BlockSpec((1, TILE, d), row),
        out_shape=jax.ShapeDtypeStruct((b, n_tiles * TILE, d), F32),
        compiler_params=_cparams(2),
        name="ffn",
    )(xm, mod, g2, wi, wo)


def _partner(seg):
    lane = np.arange(LANES)
    return np.where((lane % (2 * seg)) < seg, lane + seg, lane - seg)


def _rope_tables(n_lat, n_ctx):
    rows = n_lat // GRID_W
    row_id = np.repeat(np.arange(rows), GRID_W).astype(np.float64)
    col_id = np.tile(np.arange(GRID_W), rows).astype(np.float64)

    def angles(dim):
        half = dim // 2
        freqs = ROPE_THETA ** (-np.arange(0, half, 2, dtype=np.float64) / half)
        ax = lambda pos: np.concatenate([pos[:, None] * freqs[None, :]] * 2, axis=-1)
        return np.concatenate([ax(row_id), ax(col_id)], axis=-1)

    def signed(sin, dim):
        sign = np.where((np.arange(dim) % (dim // 2)) < dim // 4, -1.0, 1.0)
        return sin * sign[None, :]

    a64, a32 = angles(GQA_HEAD_DIM), angles(MLA_ROPE_DIM)
    t = n_lat + n_ctx
    cosa, sina = np.ones((t, LANES)), np.zeros((t, LANES))
    cosa[:n_lat] = np.tile(np.cos(a64), (1, 2))
    sina[:n_lat] = np.tile(signed(np.sin(a64), GQA_HEAD_DIM), (1, 2))
    cosb, sinb = np.ones((t, LANES)), np.zeros((t, LANES))
    cosb[:n_lat, 64:96] = np.cos(a32)
    sinb[:n_lat, 64:96] = signed(np.sin(a32), MLA_ROPE_DIM)
    return [jnp.asarray(a, F32) for a in (cosa, sina, cosb, sinb)]


TAB_Q, TAB_K, TAB_QB, TAB_KR, N_TABLES = 0, 1, 2, 3, 4


def _gained_tables(rope, gains, scales):
    cosa, sina, cosb, sinb = rope
    cos = jnp.stack([cosa, cosa, cosb, cosb])
    sin = jnp.stack([sina, sina, sinb, sinb])
    g = jnp.stack(gains, axis=1) * jnp.asarray(scales, F32)[None, :, None]
    partner = np.stack([_partner(16), _partner(16), _partner(8), _partner(8)])
    g_sin = jnp.take_along_axis(g, jnp.asarray(partner)[None], axis=-1)
    return jnp.concatenate([cos[None] * g[:, :, None, :], sin[None] * g_sin[:, :, None, :]], axis=1)


def _seg_matrices():
    def blockdiag(segs):
        m = np.zeros((2 * LANES, 2 * LANES))
        for start, n in segs:
            m[start:start + n, start:start + n] = 1.0 / n
        return m

    gqa = blockdiag([(s0, 64) for s0 in range(0, 256, 64)])
    qb = blockdiag([(0, 64), (64, 32), (128, 64), (192, 32)])
    kn = blockdiag([(0, 64), (128, 64)])
    kk = blockdiag([(0, 64), (64, 64), (192, 32)])
    return [jnp.asarray(a, F32).astype(BF16) for a in (gqa, qb, kn, kk)]


def _score_bounds(g_q_gqa, g_k_gqa, g_q_nope, g_k_nope, g_q_rope, g_k_rope):
    top = lambda g: jnp.max(g * g, axis=-1)
    gqa = jnp.sqrt(GQA_HEAD_DIM * top(g_q_gqa) * GQA_HEAD_DIM * top(g_k_gqa)) * (GQA_SCALE * LOG2E)
    mla = jnp.sqrt((MLA_NOPE_DIM * top(g_q_nope) + MLA_ROPE_DIM * top(g_q_rope))
                   * (MLA_NOPE_DIM * top(g_k_nope) + MLA_ROPE_DIM * top(g_k_rope))) * (MLA_SCALE * LOG2E)
    return jnp.stack([gqa, mla], axis=-1) * BOUND_SLACK


def _dft_tables(n_lat, n_ctx):
    def cs(n):
        j = np.arange(n)
        ang = 2.0 * np.pi * ((j[:, None] * j[None, :]) % n) / n
        return np.cos(ang) / np.sqrt(n), np.sin(ang) / np.sqrt(n)

    cn, sn = cs(n_lat)
    cc, sc = cs(n_ctx)
    cg, sg = cs(FOURIER_GROUP_DIM)
    eye = np.eye(FOURIER_GROUPS)
    dc = np.concatenate([cc, -sc], axis=1)
    chan = np.concatenate([np.kron(eye, cg), np.kron(eye, sg)], axis=1)
    flip = np.zeros((TILE, TILE))
    flip[np.arange(1, TILE), TILE - np.arange(1, TILE)] = 1.0
    alt = np.tile(np.where(np.arange(n_lat) % 2 == 0, 1.0, -1.0)[None, :], (8, 1))
    return [jnp.asarray(a, F32).astype(BF16) for a in (cn[:n_lat // 2], sn[:n_lat // 2], dc, chan, flip, alt)]


def _inproj_weights(rope, w_in, g_q_gqa, g_k_gqa, g_cq, g_ckv, w_uq, w_ukv, g_q_nope, g_k_nope, g_q_rope, g_k_rope):
    depth, d, _ = w_in.shape
    q, k, v, cq, ckv, kr, f, gates = (w_in[..., 0:512], w_in[..., 512:640], w_in[..., 640:768], w_in[..., 768:1152],
                                      w_in[..., 1152:1408], w_in[..., 1408:1440], w_in[..., 1440:1952],
                                      w_in[..., 1952:])
    zeros = lambda *s: jnp.zeros((depth,) + s, F32)
    krblk = jnp.concatenate([zeros(d, 64), kr, zeros(d, 32)], axis=-1)
    wa = jnp.concatenate([q, k, v, cq, krblk, ckv, f], axis=-1).astype(BF16)
    uq = w_uq.reshape(depth, MLA_Q_RANK, MLA_HEADS, MLA_NOPE_DIM + MLA_ROPE_DIM)
    wuq = jnp.concatenate([uq, zeros(MLA_Q_RANK, MLA_HEADS, 32)], axis=-1).reshape(depth, MLA_Q_RANK, -1)
    ukv = w_ukv.reshape(depth, MLA_KV_RANK, MLA_HEADS, MLA_NOPE_DIM + MLA_V_DIM)
    kn, vv = ukv[..., :MLA_NOPE_DIM], ukv[..., MLA_NOPE_DIM:]
    z64 = zeros(MLA_KV_RANK, MLA_HEADS, 64)
    wkn = jnp.concatenate([kn, z64], axis=-1).reshape(depth, MLA_KV_RANK, -1)
    even = (jnp.arange(MLA_HEADS) % 2 == 0)[None, None, :, None]
    wv = jnp.where(even, jnp.concatenate([vv, z64], axis=-1), jnp.concatenate([z64, vv], axis=-1))
    wvt = jnp.swapaxes(wv.reshape(depth, MLA_KV_RANK, -1), 1, 2)
    gkn = jnp.concatenate([g_k_nope, zeros(64)], axis=-1)[:, None, :]
    z32, z64v = zeros(32), zeros(64)
    tabs = _gained_tables(rope, [jnp.tile(g_q_gqa, (1, 2)), jnp.tile(g_k_gqa, (1, 2)),
                                 jnp.concatenate([g_q_nope, g_q_rope, z32], axis=-1),
                                 jnp.concatenate([z64v, g_k_rope, z32], axis=-1)],
                          [GQA_SCALE * LOG2E, 1.0, MLA_SCALE * LOG2E, 1.0])
    wts = [wa, wuq.astype(BF16), wkn.astype(BF16), wvt.astype(BF16), g_cq[:, None, :], g_ckv[:, None, :], gkn]
    return wts, tabs, gates.astype(BF16)


def kernel(x, c, ctx, c_ctx, w_mod, b_mod, g_norm1, g_norm2, w_in, g_q_gqa, g_k_gqa, g_cq, g_ckv, w_uq, w_ukv,
           g_q_nope, g_k_nope, g_q_rope, g_k_rope, b_gate, w_br_a, w_br_b, w_br_c, w_out, w_ffn_in, w_ffn_out):
    b, n, d = x.shape
    nc = ctx.shape[1]
    depth = w_mod.shape[0]
    assert n % TILE == 0 and nc == TILE and b + 1 <= MOD_ROWS and n % GRID_W == 0
    n_lat_tiles, n_tiles = n // TILE, (n + nc) // TILE

    cc = jnp.concatenate([c, c_ctx[None], jnp.zeros((MOD_ROWS - b - 1, d), F32)], axis=0)
    mod = _modulation(cc, w_mod, b_mod).reshape(depth, MOD_ROWS, 6, d)
    dft = _dft_tables(n, nc)
    segs = _seg_matrices()
    bounds = _score_bounds(g_q_gqa, g_k_gqa, g_q_nope, g_k_nope, g_q_rope, g_k_rope)
    wts, tabs, wg = _inproj_weights(_rope_tables(n, nc), w_in, g_q_gqa, g_k_gqa, g_cq, g_ckv, w_uq, w_ukv,
                                    g_q_nope, g_k_nope, g_q_rope, g_k_rope)
    g1, g2 = g_norm1[:, None, :], g_norm2[:, None, :]
    mw = [wg, b_gate[:, None, :], w_br_a.astype(BF16), w_br_b.astype(BF16), w_br_c.astype(BF16), w_out.astype(BF16)]
    wi, wo = w_ffn_in.astype(BF16), w_ffn_out.astype(BF16)
    gqa_q = (0, 0, 128, 128, 256, 256, 384, 384)
    gqa_k = (0, 128, 0, 128, 256, 384, 256, 384)
    gqa_v = (0, 1, 0, 1, 2, 3, 2, 3)
    mla_o = tuple(range(0, 1024, 128))
    mla_h = tuple(range(MLA_HEADS))

    x_lat, x_ctx, ctx_tile = x, ctx, 0
    for l in range(depth):
        n_out = n_lat_tiles if l == depth - 1 else n_tiles
        qa, ka, vat, qb, kb, vbt, pf = _inproj(x_lat, x_ctx, ctx_tile, mod, l, n_lat_tiles, g1, wts, tabs, segs)
        ya = _attention(bounds[l, 0:1], qa, ka, vat, n_lat_tiles, n_out, gqa_q, gqa_k, gqa_v, "attn_gqa")
        yb = _attention(bounds[l, 1:2], qb, kb, vbt, n_lat_tiles, n_out, mla_o, mla_o, mla_h, "attn_mla")
        yc = _fourier(pf, dft, n_lat_tiles, n_out)
        xm = _merge(x_lat, x_ctx, ctx_tile, mod, l, n_lat_tiles, n_out, g1, ya, yb, yc, mw)
        xall = _ffn(xm, mod, l, n_lat_tiles, n_out, g2, wi, wo)
        x_lat, x_ctx, ctx_tile = xall, xall, n_lat_tiles
    return xall
```

```python
import functools
import math

import numpy as np
import jax
import jax.numpy as jnp
from jax import lax
from jax.experimental import pallas as pl
from jax.experimental.pallas import tpu as pltpu

GRID_W = 64
ROPE_THETA = 10000.0
EPS = 1e-6
GQA_HEADS = 8
GQA_KV_HEADS = 2
GQA_HEAD_DIM = 64
GQA_SCALE = GQA_HEAD_DIM ** -0.5
MLA_HEADS = 8
MLA_Q_RANK = 384
MLA_KV_RANK = 256
MLA_NOPE_DIM = 64
MLA_ROPE_DIM = 32
MLA_V_DIM = 64
MLA_SCALE = (MLA_NOPE_DIM + MLA_ROPE_DIM) ** -0.5
FOURIER_GROUPS = 4
FOURIER_GROUP_DIM = 128
FOURIER_WIDTH = FOURIER_GROUPS * FOURIER_GROUP_DIM
N_BRANCH = 3
LOG2E = math.log2(math.e)

LANES = 128
TILE = 256
MOD_ROWS = 16
VMEM_LIMIT = 56 * 1024 * 1024

BF16 = jnp.bfloat16
F32 = jnp.float32

_ONES_ROW_EVEN, _ONES_ROW_ODD = LANES - 1, 0

_Q0, _K0, _V0, _CQ0, _KR0, _CKV0, _F0, _WA = 0, 512, 640, 768, 1152, 1280, 1536, 2048


def _cparams(n_axes):
    return pltpu.CompilerParams(dimension_semantics=("arbitrary",) * n_axes, vmem_limit_bytes=VMEM_LIMIT)


def _dot(a, b):
    return jnp.dot(a, b, preferred_element_type=F32)


def _rms_rows(xv, g):
    return xv * lax.rsqrt(jnp.mean(xv * xv, axis=-1, keepdims=True) + EPS) * g


def _modulated(x, g, shift, scale):
    r = lax.rsqrt(jnp.mean(x * x, axis=-1, keepdims=True) + EPS)
    return (x * r * (g * (1.0 + scale)) + shift).astype(BF16)


def _layer_spec(a, l):
    return pl.BlockSpec((1,) + a.shape[1:], lambda *_: (l,) + (0,) * (a.ndim - 1))


def _const_spec(a):
    return pl.BlockSpec(a.shape, lambda *_: (0,) * a.ndim)


def _sub_tiles(n_tiles):
    return next(s for s in (3, 2, 1) if n_tiles % s == 0)


def _row_specs(x_lat, x_ctx, ctx_tile, n_lat_tiles, sub):
    d = x_lat.shape[-1]
    lat = lambda j: pl.BlockSpec((1, TILE, d), lambda bi, si: (bi, jnp.minimum(si * sub + j, n_lat_tiles - 1), 0))
    return [lat(j) for j in range(sub)] + [pl.BlockSpec((1, TILE, d), lambda bi, si: (bi, ctx_tile, 0))]


def _mod_specs(mod, l, n_batch, n_lat_tiles, sub):
    spec = lambda j: pl.BlockSpec((1, 1) + mod.shape[2:],
                                  lambda bi, si: (l, jnp.where(si * sub + j >= n_lat_tiles, n_batch, bi), 0, 0))
    return [spec(j) for j in range(sub)]


def _tile_rows(xl_refs, xc_ref, j, n_lat_tiles):
    return jnp.where(pl.program_id(1) * len(xl_refs) + j < n_lat_tiles, xl_refs[j][0], xc_ref[0])


def _mod_kernel(cc_ref, w_ref, b_ref, o_ref):
    s = cc_ref[...]
    s = s * jax.nn.sigmoid(s)
    o_ref[0] = _dot(s.astype(BF16), w_ref[0].astype(BF16)) + b_ref[0]


def _modulation(cc, w_mod, b_mod):
    depth, d, d6 = w_mod.shape
    bn = 1536
    return pl.pallas_call(
        _mod_kernel,
        grid=(depth, d6 // bn),
        in_specs=[pl.BlockSpec((MOD_ROWS, d), lambda l, j: (0, 0)),
                  pl.BlockSpec((1, d, bn), lambda l, j: (l, 0, j)),
                  pl.BlockSpec((1, 1, bn), lambda l, j: (l, 0, j))],
        out_specs=pl.BlockSpec((1, MOD_ROWS, bn), lambda l, j: (l, 0, j)),
        out_shape=jax.ShapeDtypeStruct((depth, MOD_ROWS, d6), F32),
        compiler_params=_cparams(2),
        name="modulation",
    )(cc, w_mod, b_mod.reshape(depth, 1, d6))


def _rot_half(xv, lane, seg):
    first = (lane & (2 * seg - 1)) < seg
    return jnp.where(first, pltpu.roll(xv, LANES - seg, 1), pltpu.roll(xv, seg, 1))


def _seg_rsqrt(xw, seg_ref):
    return lax.rsqrt(_dot((xw * xw).astype(BF16), seg_ref[...]) + EPS)


def _inproj_kernel(*refs, n_lat_tiles, sub):
    xl_refs, refs = refs[:sub], refs[sub:]
    xc_ref, refs = refs[0], refs[1:]
    mod_refs, refs = refs[:sub], refs[sub:]
    (g1_ref, wa_ref, wuq_ref, wkn_ref, wvt_ref, gcq_ref, gckv_ref, gkn_ref, tg_ref, tab_ref,
     sgqa_ref, sqb_ref, skn_ref, skk_ref, qa_ref, ka_ref, vat_ref, qb_ref, kb_ref, vbt_ref, pf_ref) = refs
    lane = lax.broadcasted_iota(jnp.int32, (1, LANES), 1)
    lo = lane < 64
    hi = lane >= 64
    row = lax.broadcasted_iota(jnp.int32, (LANES, TILE), 0)
    blk = lambda a, j: a[:, j * LANES:(j + 1) * LANES]
    n_qblk = GQA_HEADS * GQA_HEAD_DIM // LANES
    w2 = 2 * LANES

    def project(j):
        rows = slice(j * TILE, (j + 1) * TILE)
        m = mod_refs[j][0, 0]
        hb = _modulated(_tile_rows(xl_refs, xc_ref, j, n_lat_tiles), g1_ref[0], m[0:1], m[1:2])
        p_c = _dot(hb, wa_ref[0, :, _CQ0:_CKV0])
        p_ckv = _dot(hb, wa_ref[0, :, _CKV0:_F0])
        pq = _dot(hb, wa_ref[0, :, _Q0:_K0])
        pkv = _dot(hb, wa_ref[0, :, _K0:_CQ0])
        pf_ref[0, rows] = _dot(hb, wa_ref[0, :, _F0:_WA]).astype(BF16)
        cq = _rms_rows(p_c[:, :MLA_Q_RANK], gcq_ref[0]).astype(BF16)
        ckv = _rms_rows(p_ckv, gckv_ref[0])
        qb = _dot(cq, wuq_ref[0])
        kn_all = _dot(ckv.astype(BF16), wkn_ref[0])
        vbt = _dot(wvt_ref[0], ckv.T.astype(BF16))
        for h in range(MLA_HEADS):
            vh = vbt[h * LANES:(h + 1) * LANES]
            ones_row = _ONES_ROW_EVEN if h % 2 == 0 else _ONES_ROW_ODD
            vbt_ref[0, h, :, rows] = jnp.where(row == ones_row, 1.0, vh).astype(BF16)
        vt = pkv[:, LANES:].T
        for kvh in range(GQA_KV_HEADS):
            own = (row >= 64) == (kvh == 1)
            top = jnp.where(own, vt, 0.0) if kvh == 0 else pltpu.roll(jnp.where(own, vt, 0.0), 64, 0)
            bot = pltpu.roll(top, 64, 0)
            vat_ref[0, 2 * kvh, :, rows] = jnp.where(row == _ONES_ROW_EVEN, 1.0, top).astype(BF16)
            vat_ref[0, 2 * kvh + 1, :, rows] = jnp.where(row == _ONES_ROW_ODD, 1.0, bot).astype(BF16)
        return pq, pkv[:, :LANES], qb, kn_all, p_c[:, MLA_Q_RANK:]

    def finish(j, pq, xk, qb, kn_all, xkr):
        rows = slice(j * TILE, (j + 1) * TILE)
        r_qb = jnp.concatenate([_seg_rsqrt(qb[:, c * w2:(c + 1) * w2], sqb_ref) for c in range(MLA_HEADS // 2)],
                               axis=1)
        r_kn = jnp.concatenate([_seg_rsqrt(kn_all[:, c * w2:(c + 1) * w2], skn_ref) for c in range(MLA_HEADS // 2)],
                               axis=1)
        r_q = jnp.concatenate([_seg_rsqrt(pq[:, c * w2:(c + 1) * w2], sgqa_ref) for c in range(n_qblk // 2)], axis=1)
        r_kk = _seg_rsqrt(jnp.concatenate([xk, xkr], axis=1), skk_ref)
        r_k, r_kr = r_kk[:, :LANES], r_kk[:, LANES:]

        tab = [tab_ref[i, rows] * tg_ref[0, i:i + 1] for i in range(2 * N_TABLES)]

        def roped(xb, table, seg, r):
            return (xb * tab[table] + _rot_half(xb, lane, seg) * tab[N_TABLES + table]) * r

        for h in range(MLA_HEADS):
            qb_ref[0, rows, h * LANES:(h + 1) * LANES] = roped(blk(qb, h), TAB_QB, 8, blk(r_qb, h)).astype(BF16)
        kr = roped(xkr, TAB_KR, 8, r_kr)
        for h in range(MLA_HEADS):
            kb_ref[0, rows, h * LANES:(h + 1) * LANES] = (blk(kn_all, h) * blk(r_kn, h) * gkn_ref[0] + kr).astype(BF16)
        for c in range(n_qblk):
            qa_ref[0, rows, c * LANES:(c + 1) * LANES] = roped(blk(pq, c), TAB_Q, 16, blk(r_q, c)).astype(BF16)
        kn = roped(xk, TAB_K, 16, r_k)
        sw = pltpu.roll(kn, 64, 1)
        for c, f in enumerate((jnp.where(lo, kn, 0.0), jnp.where(hi, sw, 0.0), jnp.where(lo, sw, 0.0),
                               jnp.where(hi, kn, 0.0))):
            ka_ref[0, rows, c * LANES:(c + 1) * LANES] = f.astype(BF16)

    pending = project(0)
    for j in range(sub):
        nxt = project(j + 1) if j + 1 < sub else None
        finish(j, *pending)
        pending = nxt


def _inproj(x_lat, x_ctx, ctx_tile, mod, l, n_lat_tiles, g1, wts, tabs, segs):
    b = x_lat.shape[0]
    nt = n_lat_tiles + 1
    sub = _sub_tiles(nt)
    t, step_rows = nt * TILE, sub * TILE
    row = lambda bi, si: (bi, si, 0)
    rowt = lambda bi, si: (bi, 0, 0, si)
    tok = lambda w: (pl.BlockSpec((1, step_rows, w), row), jax.ShapeDtypeStruct((b, t, w), BF16))
    tra = lambda h: (pl.BlockSpec((1, h, LANES, step_rows), rowt), jax.ShapeDtypeStruct((b, h, LANES, t), BF16))
    outs = [tok(512), tok(512), tra(4), tok(1024), tok(1024), tra(MLA_HEADS), tok(512)]
    return pl.pallas_call(
        functools.partial(_inproj_kernel, n_lat_tiles=n_lat_tiles, sub=sub),
        grid=(b, nt // sub),
        in_specs=_row_specs(x_lat, x_ctx, ctx_tile, n_lat_tiles, sub) + _mod_specs(mod, l, b, n_lat_tiles, sub)
        + [_layer_spec(g1, l)] + [_layer_spec(w, l) for w in wts]
        + [pl.BlockSpec((2 * N_TABLES, step_rows, LANES), lambda bi, si: (0, si, 0))]
        + [_const_spec(s) for s in segs],
        out_specs=[o[0] for o in outs],
        out_shape=[o[1] for o in outs],
        compiler_params=_cparams(2),
        name="inproj",
    )(*([x_lat] * sub), x_ctx, *([mod] * sub), g1, *wts, tabs, *segs)


def _reduce_keys(a, op):
    nk, nq = a.shape
    part = op(a.reshape(nk // TILE, TILE, nq), axis=0)
    return op(part, axis=0, keepdims=True)


SHIFT_SAFE = 60.0
BOUND_SLACK = 1.0 + 2.0 ** -6


def _attn_kernel(shift_ref, q_ref, k_ref, vt_ref, o_ref, *, n_lat_tiles, with_ctx, qoff, koff, vidx):
    n_keys = k_ref.shape[1]
    n_pairs = len(qoff) // 2
    zero = jnp.zeros((TILE, LANES), BF16)
    bound = shift_ref[0]

    def scores(row0, pair, key0, nk):
        e, o = 2 * pair, 2 * pair + 1
        assert koff[o] == koff[e] + LANES
        q_e = q_ref[0, pl.ds(row0, TILE), qoff[e]:qoff[e] + LANES]
        q_o = q_ref[0, pl.ds(row0, TILE), qoff[o]:qoff[o] + LANES]
        qd = jnp.concatenate([jnp.concatenate([q_e, zero], axis=1), jnp.concatenate([zero, q_o], axis=1)], axis=0)
        k2 = k_ref[0, key0:key0 + nk, koff[e]:koff[e] + 2 * LANES]
        return lax.dot_general(k2, qd, (((1,), (1,)), ((), ())), preferred_element_type=F32)

    def tile(row0, key0, nk, use_bound):
        for pair in range(n_pairs):
            e, o = 2 * pair, 2 * pair + 1
            st = scores(row0, pair, key0, nk)
            pb = jnp.exp2(st - (bound if use_bound else _reduce_keys(st, jnp.max))).astype(BF16)
            ot_e = _dot(vt_ref[0, vidx[e], :, key0:key0 + nk], pb[:, :TILE])
            ot_o = _dot(vt_ref[0, vidx[o], :, key0:key0 + nk], pb[:, TILE:])
            ot_e = ot_e * (1.0 / ot_e[_ONES_ROW_EVEN:_ONES_ROW_EVEN + 1])
            ot_o = ot_o * (1.0 / ot_o[_ONES_ROW_ODD:_ONES_ROW_ODD + 1])
            ot = jnp.where(lax.broadcasted_iota(jnp.int32, (LANES, TILE), 0) < 64, ot_e, ot_o)
            o_ref[0, pl.ds(row0, TILE), pair * LANES:(pair + 1) * LANES] = ot.T.astype(BF16)

    def all_tiles(use_bound):
        def body(t, carry):
            tile(pl.multiple_of(t * TILE, TILE), 0, n_keys, use_bound)
            return carry
        lax.fori_loop(0, n_lat_tiles, body, 0)
        if with_ctx:
            lat = n_lat_tiles * TILE
            tile(lat, lat, n_keys - lat, use_bound)

    lax.cond(bound < SHIFT_SAFE, lambda: all_tiles(True), lambda: all_tiles(False))


def _attention(shift, q, k, vt, n_lat_tiles, n_q_tiles, qoff, koff, vidx, name):
    b, t, wq = q.shape
    wk = k.shape[2]
    nh = vt.shape[1]
    kern = functools.partial(_attn_kernel, n_lat_tiles=n_lat_tiles, with_ctx=n_q_tiles > n_lat_tiles,
                             qoff=qoff, koff=koff, vidx=vidx)
    return pl.pallas_call(
        kern,
        grid=(b,),
        in_specs=[pl.BlockSpec(memory_space=pltpu.SMEM),
                  pl.BlockSpec((1, t, wq), lambda bi: (bi, 0, 0)),
                  pl.BlockSpec((1, t, wk), lambda bi: (bi, 0, 0)),
                  pl.BlockSpec((1, nh, LANES, t), lambda bi: (bi, 0, 0, 0))],
        out_specs=pl.BlockSpec((1, n_q_tiles * TILE, 512), lambda bi: (bi, 0, 0)),
        out_shape=jax.ShapeDtypeStruct((b, n_q_tiles * TILE, 512), BF16),
        compiler_params=_cparams(1),
        name=name,
    )(shift, q, k, vt)


def _fourier_kernel(pf_ref, cn_ref, sn_ref, dc_ref, cs_ref, flip_ref, alt_ref, o_ref, a_ref, b_ref, *, n_lat_tiles,
                    with_ctx):
    n_lat = n_lat_tiles * TILE
    half_tiles = n_lat_tiles // 2
    w = FOURIER_WIDTH
    for c in range(n_lat_tiles):
        ab = _dot(pf_ref[0, c * TILE:(c + 1) * TILE, :], cs_ref[...])
        a_ref[c * TILE:(c + 1) * TILE, :] = ab[:, :w].astype(BF16)
        b_ref[c * TILE:(c + 1) * TILE, :] = ab[:, w:].astype(BF16)
    row0 = lax.broadcasted_iota(jnp.int32, (TILE, w), 0) == 0
    carry = _dot(alt_ref[...], a_ref[...])[0:1] * (1.0 / math.sqrt(n_lat))
    for m in reversed(range(half_tiles)):
        g = _dot(cn_ref[m * TILE:(m + 1) * TILE, :], a_ref[...])
        h = _dot(sn_ref[m * TILE:(m + 1) * TILE, :], b_ref[...])
        o_ref[0, m * TILE:(m + 1) * TILE, :] = (g - h).astype(BF16)
        mirrored = (g + h).astype(BF16)
        tile_rev = _dot(flip_ref[...], mirrored)
        bot = n_lat_tiles - 1 - m
        o_ref[0, bot * TILE:(bot + 1) * TILE, :] = jnp.where(row0, carry, tile_rev).astype(BF16)
        carry = mirrored[0:1].astype(F32)
    if with_ctx:
        ab = _dot(pf_ref[0, n_lat:n_lat + TILE, :], cs_ref[...])
        abc = jnp.concatenate([ab[:, :w], ab[:, w:]], axis=0).astype(BF16)
        o_ref[0, n_lat:n_lat + TILE, :] = _dot(dc_ref[...], abc).astype(BF16)


def _fourier(pf, tables, n_lat_tiles, n_tiles):
    b, t, w = pf.shape
    n_lat = n_lat_tiles * TILE
    kern = functools.partial(_fourier_kernel, n_lat_tiles=n_lat_tiles, with_ctx=n_tiles > n_lat_tiles)
    return pl.pallas_call(
        kern,
        grid=(b,),
        in_specs=[pl.BlockSpec((1, t, w), lambda bi: (bi, 0, 0))] + [_const_spec(a) for a in tables],
        out_specs=pl.BlockSpec((1, n_tiles * TILE, w), lambda bi: (bi, 0, 0)),
        out_shape=jax.ShapeDtypeStruct((b, n_tiles * TILE, w), BF16),
        scratch_shapes=[pltpu.VMEM((n_lat, w), BF16), pltpu.VMEM((n_lat, w), BF16)],
        compiler_params=_cparams(1),
        name="fourier",
    )(pf, *tables)


def _merge_kernel(*refs, n_lat_tiles, sub):
    xl_refs, refs = refs[:sub], refs[sub:]
    xc_ref, refs = refs[0], refs[1:]
    mod_refs, refs = refs[:sub], refs[sub:]
    g1_ref, ya_ref, yb_ref, yc_ref, wg_ref, bg_ref, wa_ref, wb_ref, wc_ref, wo_ref, o_ref = refs
    for j in range(sub):
        rows = slice(j * TILE, (j + 1) * TILE)
        x = _tile_rows(xl_refs, xc_ref, j, n_lat_tiles)
        m = mod_refs[j][0, 0]
        d = x.shape[-1]
        hb = _modulated(x, g1_ref[0], m[0:1], m[1:2])
        acc = None
        for i, (y_ref, w_ref) in enumerate(((ya_ref, wa_ref), (yb_ref, wb_ref), (yc_ref, wc_ref))):
            gate = jax.nn.sigmoid(_dot(hb, wg_ref[0, :, i * d:(i + 1) * d]) + bg_ref[0, :, i * d:(i + 1) * d])
            term = gate * _dot(y_ref[0, rows], w_ref[0])
            acc = term if acc is None else acc + term
        o_ref[0, rows] = x + m[2:3] * _dot(acc.astype(BF16), wo_ref[0])


def _merge(x_lat, x_ctx, ctx_tile, mod, l, n_lat_tiles, n_tiles, g1, ya, yb, yc, wts):
    b, _, d = x_lat.shape
    sub = _sub_tiles(n_tiles)
    row = lambda bi, si: (bi, si, 0)
    return pl.pallas_call(
        functools.partial(_merge_kernel, n_lat_tiles=n_lat_tiles, sub=sub),
        grid=(b, n_tiles // sub),
        in_specs=_row_specs(x_lat, x_ctx, ctx_tile, n_lat_tiles, sub) + _mod_specs(mod, l, b, n_lat_tiles, sub)
        + [_layer_spec(g1, l)] + [pl.BlockSpec((1, sub * TILE, 512), row)] * 3 + [_layer_spec(w, l) for w in wts],
        out_specs=pl.BlockSpec((1, sub * TILE, d), row),
        out_shape=jax.ShapeDtypeStruct((b, n_tiles * TILE, d), F32),
        compiler_params=_cparams(2),
        name="merge",
    )(*([x_lat] * sub), x_ctx, *([mod] * sub), g1, ya, yb, yc, *wts)


def _ffn_kernel(*refs, sub):
    x_ref, refs = refs[0], refs[1:]
    mod_refs, refs = refs[:sub], refs[sub:]
    g2_ref, wi_ref, wo_ref, o_ref = refs
    for j in range(sub):
        rows = slice(j * TILE, (j + 1) * TILE)
        x = x_ref[0, rows]
        m = mod_refs[j][0, 0]
        hb = _modulated(x, g2_ref[0], m[3:4], m[4:5])
        gu = _dot(hb, wi_ref[0])
        hid = gu.shape[-1] // 2
        gate, up = gu[:, :hid], gu[:, hid:]
        act = (gate * jax.nn.sigmoid(gate) * up).astype(BF16)
        o_ref[0, rows] = x + m[5:6] * _dot(act, wo_ref[0])


def _ffn(xm, mod, l, n_lat_tiles, n_tiles, g2, wi, wo):
    b, _, d = xm.shape
    sub = _sub_tiles(n_tiles)
    row = lambda bi, si: (bi, si, 0)
    once = lambda a: pl.BlockSpec((1,) + a.shape[1:], lambda *_: (l,) + (0,) * (a.ndim - 1),
                                  pipeline_mode=pl.Buffered(1))
    return pl.pallas_call(
        functools.partial(_ffn_kernel, sub=sub),
        grid=(b, n_tiles // sub),
        in_specs=[pl.BlockSpec((1, sub * TILE, d), row)] + _mod_specs(mod, l, b, n_lat_tiles, sub)
        + [_layer_spec(g2, l), once(wi), once(wo)],
        out_specs=pl.BlockSpec((1, sub * TILE, d), row),
        out_shape=jax.ShapeDtypeStruct((b, n_tiles * TILE, d), F32),
        compiler_params=_cparams(2),
        name="ffn",
    )(xm, *([mod] * sub), g2, wi, wo)


def _partner(seg):
    lane = np.arange(LANES)
    return np.where((lane % (2 * seg)) < seg, lane + seg, lane - seg)


def _rope_tables(n_lat, n_ctx):
    rows = n_lat // GRID_W
    row_id = np.repeat(np.arange(rows), GRID_W).astype(np.float64)
    col_id = np.tile(np.arange(GRID_W), rows).astype(np.float64)

    def angles(dim):
        half = dim // 2
        freqs = ROPE_THETA ** (-np.arange(0, half, 2, dtype=np.float64) / half)
        ax = lambda pos: np.concatenate([pos[:, None] * freqs[None, :]] * 2, axis=-1)
        return np.concatenate([ax(row_id), ax(col_id)], axis=-1)

    def signed(sin, dim):
        sign = np.where((np.arange(dim) % (dim // 2)) < dim // 4, -1.0, 1.0)
        return sin * sign[None, :]

    a64, a32 = angles(GQA_HEAD_DIM), angles(MLA_ROPE_DIM)
    t = n_lat + n_ctx
    cosa, sina = np.ones((t, LANES)), np.zeros((t, LANES))
    cosa[:n_lat] = np.tile(np.cos(a64), (1, 2))
    sina[:n_lat] = np.tile(signed(np.sin(a64), GQA_HEAD_DIM), (1, 2))
    cosb, sinb = np.ones((t, LANES)), np.zeros((t, LANES))
    cosb[:n_lat, 64:96] = np.cos(a32)
    sinb[:n_lat, 64:96] = signed(np.sin(a32), MLA_ROPE_DIM)
    return jnp.asarray(np.stack([cosa, cosa, cosb, cosb, sina, sina, sinb, sinb]), F32)


TAB_Q, TAB_K, TAB_QB, TAB_KR, N_TABLES = 0, 1, 2, 3, 4


def _table_gains(gains, scales):
    g = jnp.stack(gains, axis=1) * jnp.asarray(scales, F32)[None, :, None]
    partner = np.stack([_partner(16), _partner(16), _partner(8), _partner(8)])
    return jnp.concatenate([g, jnp.take_along_axis(g, jnp.asarray(partner)[None], axis=-1)], axis=1)


def _seg_matrices():
    def blockdiag(segs):
        m = np.zeros((2 * LANES, 2 * LANES))
        for start, n in segs:
            m[start:start + n, start:start + n] = 1.0 / n
        return m

    gqa = blockdiag([(s0, 64) for s0 in range(0, 256, 64)])
    qb = blockdiag([(0, 64), (64, 32), (128, 64), (192, 32)])
    kn = blockdiag([(0, 64), (128, 64)])
    kk = blockdiag([(0, 64), (64, 64), (192, 32)])
    return [jnp.asarray(a, F32).astype(BF16) for a in (gqa, qb, kn, kk)]


def _score_bounds(g_q_gqa, g_k_gqa, g_q_nope, g_k_nope, g_q_rope, g_k_rope):
    top = lambda g: jnp.max(g * g, axis=-1)
    gqa = jnp.sqrt(GQA_HEAD_DIM * top(g_q_gqa) * GQA_HEAD_DIM * top(g_k_gqa)) * (GQA_SCALE * LOG2E)
    mla = jnp.sqrt((MLA_NOPE_DIM * top(g_q_nope) + MLA_ROPE_DIM * top(g_q_rope))
                   * (MLA_NOPE_DIM * top(g_k_nope) + MLA_ROPE_DIM * top(g_k_rope))) * (MLA_SCALE * LOG2E)
    return jnp.stack([gqa, mla], axis=-1) * BOUND_SLACK


def _dft_tables(n_lat, n_ctx):
    def cs(n):
        j = np.arange(n)
        ang = 2.0 * np.pi * ((j[:, None] * j[None, :]) % n) / n
        return np.cos(ang) / np.sqrt(n), np.sin(ang) / np.sqrt(n)

    cn, sn = cs(n_lat)
    cc, sc = cs(n_ctx)
    cg, sg = cs(FOURIER_GROUP_DIM)
    eye = np.eye(FOURIER_GROUPS)
    dc = np.concatenate([cc, -sc], axis=1)
    chan = np.concatenate([np.kron(eye, cg), np.kron(eye, sg)], axis=1)
    flip = np.zeros((TILE, TILE))
    flip[np.arange(1, TILE), TILE - np.arange(1, TILE)] = 1.0
    alt = np.tile(np.where(np.arange(n_lat) % 2 == 0, 1.0, -1.0)[None, :], (8, 1))
    return [jnp.asarray(a, F32).astype(BF16) for a in (cn[:n_lat // 2], sn[:n_lat // 2], dc, chan, flip, alt)]


def _inproj_weights(w_in, g_q_gqa, g_k_gqa, g_cq, g_ckv, w_uq, w_ukv, g_q_nope, g_k_nope, g_q_rope, g_k_rope):
    depth, d, _ = w_in.shape
    q, k, v, cq, ckv, kr, f, gates = (w_in[..., 0:512], w_in[..., 512:640], w_in[..., 640:768], w_in[..., 768:1152],
                                      w_in[..., 1152:1408], w_in[..., 1408:1440], w_in[..., 1440:1952],
                                      w_in[..., 1952:])
    zeros = lambda *s: jnp.zeros((depth,) + s, F32)
    krblk = jnp.concatenate([zeros(d, 64), kr, zeros(d, 32)], axis=-1)
    wa = jnp.concatenate([q, k, v, cq, krblk, ckv, f], axis=-1).astype(BF16)
    uq = w_uq.reshape(depth, MLA_Q_RANK, MLA_HEADS, MLA_NOPE_DIM + MLA_ROPE_DIM)
    wuq = jnp.concatenate([uq, zeros(MLA_Q_RANK, MLA_HEADS, 32)], axis=-1).reshape(depth, MLA_Q_RANK, -1)
    ukv = w_ukv.reshape(depth, MLA_KV_RANK, MLA_HEADS, MLA_NOPE_DIM + MLA_V_DIM)
    kn, vv = ukv[..., :MLA_NOPE_DIM], ukv[..., MLA_NOPE_DIM:]
    z64 = zeros(MLA_KV_RANK, MLA_HEADS, 64)
    wkn = jnp.concatenate([kn, z64], axis=-1).reshape(depth, MLA_KV_RANK, -1)
    even = (jnp.arange(MLA_HEADS) % 2 == 0)[None, None, :, None]
    wv = jnp.where(even, jnp.concatenate([vv, z64], axis=-1), jnp.concatenate([z64, vv], axis=-1))
    wvt = jnp.swapaxes(wv.reshape(depth, MLA_KV_RANK, -1), 1, 2)
    gkn = jnp.concatenate([g_k_nope, zeros(64)], axis=-1)[:, None, :]
    z32, z64v = zeros(32), zeros(64)
    tgains = _table_gains([jnp.tile(g_q_gqa, (1, 2)), jnp.tile(g_k_gqa, (1, 2)),
                           jnp.concatenate([g_q_nope, g_q_rope, z32], axis=-1),
                           jnp.concatenate([z64v, g_k_rope, z32], axis=-1)],
                          [GQA_SCALE * LOG2E, 1.0, MLA_SCALE * LOG2E, 1.0])
    wts = [wa, wuq.astype(BF16), wkn.astype(BF16), wvt.astype(BF16), g_cq[:, None, :], g_ckv[:, None, :], gkn,
           tgains]
    return wts, gates.astype(BF16)


def kernel(x, c, ctx, c_ctx, w_mod, b_mod, g_norm1, g_norm2, w_in, g_q_gqa, g_k_gqa, g_cq, g_ckv, w_uq, w_ukv,
           g_q_nope, g_k_nope, g_q_rope, g_k_rope, b_gate, w_br_a, w_br_b, w_br_c, w_out, w_ffn_in, w_ffn_out):
    b, n, d = x.shape
    nc = ctx.shape[1]
    depth = w_mod.shape[0]
    assert n % TILE == 0 and nc == TILE and b + 1 <= MOD_ROWS and n % GRID_W == 0
    n_lat_tiles, n_tiles = n // TILE, (n + nc) // TILE

    cc = jnp.concatenate([c, c_ctx[None], jnp.zeros((MOD_ROWS - b - 1, d), F32)], axis=0)
    mod = _modulation(cc, w_mod, b_mod).reshape(depth, MOD_ROWS, 6, d)
    dft = _dft_tables(n, nc)
    segs = _seg_matrices()
    bounds = _score_bounds(g_q_gqa, g_k_gqa, g_q_nope, g_k_nope, g_q_rope, g_k_rope)
    tabs = _rope_tables(n, nc)
    wts, wg = _inproj_weights(w_in, g_q_gqa, g_k_gqa, g_cq, g_ckv, w_uq, w_ukv, g_q_nope, g_k_nope, g_q_rope,
                              g_k_rope)
    g1, g2 = g_norm1[:, None, :], g_norm2[:, None, :]
    mw = [wg, b_gate[:, None, :], w_br_a.astype(BF16), w_br_b.astype(BF16), w_br_c.astype(BF16), w_out.astype(BF16)]
    wi, wo = w_ffn_in.astype(BF16), w_ffn_out.astype(BF16)
    gqa_q = (0, 0, 128, 128, 256, 256, 384, 384)
    gqa_k = (0, 128, 0, 128, 256, 384, 256, 384)
    gqa_v = (0, 1, 0, 1, 2, 3, 2, 3)
    mla_o = tuple(range(0, 1024, 128))
    mla_h = tuple(range(MLA_HEADS))

    x_lat, x_ctx, ctx_tile = x, ctx, 0
    for l in range(depth):
        n_out = n_lat_tiles if l == depth - 1 else n_tiles
        qa, ka, vat, qb, kb, vbt, pf = _inproj(x_lat, x_ctx, ctx_tile, mod, l, n_lat_tiles, g1, wts, tabs, segs)
        ya = _attention(bounds[l, 0:1], qa, ka, vat, n_lat_tiles, n_out, gqa_q, gqa_k, gqa_v, "attn_gqa")
        yb = _attention(bounds[l, 1:2], qb, kb, vbt, n_lat_tiles, n_out, mla_o, mla_o, mla_h, "attn_mla")
        yc = _fourier(pf, dft, n_lat_tiles, n_out)
        xm = _merge(x_lat, x_ctx, ctx_tile, mod, l, n_lat_tiles, n_out, g1, ya, yb, yc, mw)
        xall = _ffn(xm, mod, l, n_lat_tiles, n_out, g2, wi, wo)
        x_lat, x_ctx, ctx_tile = xall, xall, n_lat_tiles
    return xall
```

```python
import functools
import math

import numpy as np
import jax
import jax.numpy as jnp
from jax import lax
from jax.experimental import pallas as pl
from jax.experimental.pallas import tpu as pltpu

GRID_W = 64
ROPE_THETA = 10000.0
EPS = 1e-6
GQA_HEADS = 8
GQA_KV_HEADS = 2
GQA_HEAD_DIM = 64
GQA_SCALE = GQA_HEAD_DIM ** -0.5
MLA_HEADS = 8
MLA_Q_RANK = 384
MLA_KV_RANK = 256
MLA_NOPE_DIM = 64
MLA_ROPE_DIM = 32
MLA_V_DIM = 64
MLA_SCALE = (MLA_NOPE_DIM + MLA_ROPE_DIM) ** -0.5
FOURIER_GROUPS = 4
FOURIER_GROUP_DIM = 128
FOURIER_WIDTH = FOURIER_GROUPS * FOURIER_GROUP_DIM
N_BRANCH = 3
LOG2E = math.log2(math.e)

LANES = 128
TILE = 256
MOD_ROWS = 16
VMEM_LIMIT = 56 * 1024 * 1024

BF16 = jnp.bfloat16
F32 = jnp.float32

_ONES_ROW_EVEN, _ONES_ROW_ODD = LANES - 1, 0

_Q0, _K0, _V0, _CQ0, _KR0, _CKV0, _F0, _WA = 0, 512, 640, 768, 1152, 1280, 1536, 2048


def _cparams(n_axes):
    return pltpu.CompilerParams(dimension_semantics=("arbitrary",) * n_axes, vmem_limit_bytes=VMEM_LIMIT)


def _dot(a, b):
    return jnp.dot(a, b, preferred_element_type=F32)


def _rms_rows(xv, g):
    return xv * lax.rsqrt(jnp.mean(xv * xv, axis=-1, keepdims=True) + EPS) * g


def _modulated(x, g, shift, scale):
    r = lax.rsqrt(jnp.mean(x * x, axis=-1, keepdims=True) + EPS)
    return (x * r * (g * (1.0 + scale)) + shift).astype(BF16)


def _layer_spec(a, l):
    return pl.BlockSpec((1,) + a.shape[1:], lambda *_: (l,) + (0,) * (a.ndim - 1))


def _const_spec(a):
    return pl.BlockSpec(a.shape, lambda *_: (0,) * a.ndim)


def _sub_tiles(n_tiles):
    return next(s for s in (3, 2, 1) if n_tiles % s == 0)


def _row_specs(x_lat, x_ctx, ctx_tile, n_lat_tiles, sub):
    d = x_lat.shape[-1]
    lat = lambda j: pl.BlockSpec((1, TILE, d), lambda bi, si: (bi, jnp.minimum(si * sub + j, n_lat_tiles - 1), 0))
    return [lat(j) for j in range(sub)] + [pl.BlockSpec((1, TILE, d), lambda bi, si: (bi, ctx_tile, 0))]


def _mod_specs(mod, l, n_batch, n_lat_tiles, sub):
    spec = lambda j: pl.BlockSpec((1, 1) + mod.shape[2:],
                                  lambda bi, si: (l, jnp.where(si * sub + j >= n_lat_tiles, n_batch, bi), 0, 0))
    return [spec(j) for j in range(sub)]


def _tile_rows(xl_refs, xc_ref, j, n_lat_tiles):
    return jnp.where(pl.program_id(1) * len(xl_refs) + j < n_lat_tiles, xl_refs[j][0], xc_ref[0])


def _mod_kernel(cc_ref, w_ref, b_ref, o_ref):
    s = cc_ref[...]
    s = s * jax.nn.sigmoid(s)
    o_ref[0] = _dot(s.astype(BF16), w_ref[0].astype(BF16)) + b_ref[0]


def _modulation(cc, w_mod, b_mod):
    depth, d, d6 = w_mod.shape
    bn = 1536
    return pl.pallas_call(
        _mod_kernel,
        grid=(depth, d6 // bn),
        in_specs=[pl.BlockSpec((MOD_ROWS, d), lambda l, j: (0, 0)),
                  pl.BlockSpec((1, d, bn), lambda l, j: (l, 0, j)),
                  pl.BlockSpec((1, 1, bn), lambda l, j: (l, 0, j))],
        out_specs=pl.BlockSpec((1, MOD_ROWS, bn), lambda l, j: (l, 0, j)),
        out_shape=jax.ShapeDtypeStruct((depth, MOD_ROWS, d6), F32),
        compiler_params=_cparams(2),
        name="modulation",
    )(cc, w_mod, b_mod.reshape(depth, 1, d6))


def _rot_half(xv, lane, seg):
    first = (lane & (2 * seg - 1)) < seg
    return jnp.where(first, pltpu.roll(xv, LANES - seg, 1), pltpu.roll(xv, seg, 1))


def _seg_rsqrt(xw, seg_ref):
    return lax.rsqrt(_dot((xw * xw).astype(BF16), seg_ref[...]) + EPS)


def _inproj_kernel(*refs, n_lat_tiles, sub):
    xl_refs, refs = refs[:sub], refs[sub:]
    xc_ref, refs = refs[0], refs[1:]
    mod_refs, refs = refs[:sub], refs[sub:]
    (g1_ref, wa_ref, wuq_ref, wkn_ref, wvt_ref, gcq_ref, gckv_ref, gkn_ref, tg_ref, tab_ref,
     sgqa_ref, sqb_ref, skn_ref, skk_ref, qa_ref, ka_ref, vat_ref, qb_ref, kb_ref, vbt_ref, pf_ref) = refs
    lane = lax.broadcasted_iota(jnp.int32, (1, LANES), 1)
    lo = lane < 64
    hi = lane >= 64
    row = lax.broadcasted_iota(jnp.int32, (LANES, TILE), 0)
    blk = lambda a, j: a[:, j * LANES:(j + 1) * LANES]
    n_qblk = GQA_HEADS * GQA_HEAD_DIM // LANES
    w2 = 2 * LANES

    def project(j):
        rows = slice(j * TILE, (j + 1) * TILE)
        m = mod_refs[j][0, 0]
        hb = _modulated(_tile_rows(xl_refs, xc_ref, j, n_lat_tiles), g1_ref[0], m[0:1], m[1:2])
        p_c = _dot(hb, wa_ref[0, :, _CQ0:_CKV0])
        p_ckv = _dot(hb, wa_ref[0, :, _CKV0:_F0])
        pq = _dot(hb, wa_ref[0, :, _Q0:_K0])
        pkv = _dot(hb, wa_ref[0, :, _K0:_CQ0])
        pf_ref[0, rows] = _dot(hb, wa_ref[0, :, _F0:_WA]).astype(BF16)
        cq = _rms_rows(p_c[:, :MLA_Q_RANK], gcq_ref[0]).astype(BF16)
        ckv = _rms_rows(p_ckv, gckv_ref[0])
        qb = _dot(cq, wuq_ref[0])
        kn_all = _dot(ckv.astype(BF16), wkn_ref[0])
        vbt = _dot(wvt_ref[0], ckv.T.astype(BF16))
        for h in range(MLA_HEADS):
            vh = vbt[h * LANES:(h + 1) * LANES]
            ones_row = _ONES_ROW_EVEN if h % 2 == 0 else _ONES_ROW_ODD
            vbt_ref[0, h, :, rows] = jnp.where(row == ones_row, 1.0, vh).astype(BF16)
        vt = pkv[:, LANES:].T
        for kvh in range(GQA_KV_HEADS):
            own = (row >= 64) == (kvh == 1)
            top = jnp.where(own, vt, 0.0) if kvh == 0 else pltpu.roll(jnp.where(own, vt, 0.0), 64, 0)
            bot = pltpu.roll(top, 64, 0)
            vat_ref[0, 2 * kvh, :, rows] = jnp.where(row == _ONES_ROW_EVEN, 1.0, top).astype(BF16)
            vat_ref[0, 2 * kvh + 1, :, rows] = jnp.where(row == _ONES_ROW_ODD, 1.0, bot).astype(BF16)
        return pq, pkv[:, :LANES], qb, kn_all, p_c[:, MLA_Q_RANK:]

    def finish(j, pq, xk, qb, kn_all, xkr):
        rows = slice(j * TILE, (j + 1) * TILE)
        r_qb = jnp.concatenate([_seg_rsqrt(qb[:, c * w2:(c + 1) * w2], sqb_ref) for c in range(MLA_HEADS // 2)],
                               axis=1)
        r_kn = jnp.concatenate([_seg_rsqrt(kn_all[:, c * w2:(c + 1) * w2], skn_ref) for c in range(MLA_HEADS // 2)],
                               axis=1)
        r_q = jnp.concatenate([_seg_rsqrt(pq[:, c * w2:(c + 1) * w2], sgqa_ref) for c in range(n_qblk // 2)], axis=1)
        r_kk = _seg_rsqrt(jnp.concatenate([xk, xkr], axis=1), skk_ref)
        r_k, r_kr = r_kk[:, :LANES], r_kk[:, LANES:]

        tab = [tab_ref[i, rows] * tg_ref[0, i:i + 1] for i in range(2 * N_TABLES)]

        def roped(xb, table, seg, r):
            return (xb * tab[table] + _rot_half(xb, lane, seg) * tab[N_TABLES + table]) * r

        for h in range(MLA_HEADS):
            qb_ref[0, rows, h * LANES:(h + 1) * LANES] = roped(blk(qb, h), TAB_QB, 8, blk(r_qb, h)).astype(BF16)
        kr = roped(xkr, TAB_KR, 8, r_kr)
        for h in range(MLA_HEADS):
            kb_ref[0, rows, h * LANES:(h + 1) * LANES] = (blk(kn_all, h) * blk(r_kn, h) * gkn_ref[0] + kr).astype(BF16)
        for c in range(n_qblk):
            qa_ref[0, rows, c * LANES:(c + 1) * LANES] = roped(blk(pq, c), TAB_Q, 16, blk(r_q, c)).astype(BF16)
        kn = roped(xk, TAB_K, 16, r_k)
        sw = pltpu.roll(kn, 64, 1)
        for c, f in enumerate((jnp.where(lo, kn, 0.0), jnp.where(hi, sw, 0.0), jnp.where(lo, sw, 0.0),
                               jnp.where(hi, kn, 0.0))):
            ka_ref[0, rows, c * LANES:(c + 1) * LANES] = f.astype(BF16)

    pending = project(0)
    for j in range(sub):
        nxt = project(j + 1) if j + 1 < sub else None
        finish(j, *pending)
        pending = nxt


def _inproj(x_lat, x_ctx, ctx_tile, mod, l, n_lat_tiles, g1, wts, tabs, segs):
    b = x_lat.shape[0]
    nt = n_lat_tiles + 1
    sub = _sub_tiles(nt)
    t, step_rows = nt * TILE, sub * TILE
    row = lambda bi, si: (bi, si, 0)
    rowt = lambda bi, si: (bi, 0, 0, si)
    tok = lambda w: (pl.BlockSpec((1, step_rows, w), row), jax.ShapeDtypeStruct((b, t, w), BF16))
    tra = lambda h: (pl.BlockSpec((1, h, LANES, step_rows), rowt), jax.ShapeDtypeStruct((b, h, LANES, t), BF16))
    outs = [tok(512), tok(512), tra(4), tok(1024), tok(1024), tra(MLA_HEADS), tok(512)]
    return pl.pallas_call(
        functools.partial(_inproj_kernel, n_lat_tiles=n_lat_tiles, sub=sub),
        grid=(b, nt // sub),
        in_specs=_row_specs(x_lat, x_ctx, ctx_tile, n_lat_tiles, sub) + _mod_specs(mod, l, b, n_lat_tiles, sub)
        + [_layer_spec(g1, l)] + [_layer_spec(w, l) for w in wts]
        + [pl.BlockSpec((2 * N_TABLES, step_rows, LANES), lambda bi, si: (0, si, 0))]
        + [_const_spec(s) for s in segs],
        out_specs=[o[0] for o in outs],
        out_shape=[o[1] for o in outs],
        compiler_params=_cparams(2),
        name="inproj",
    )(*([x_lat] * sub), x_ctx, *([mod] * sub), g1, *wts, tabs, *segs)


def _reduce_keys(a, op):
    nk, nq = a.shape
    part = op(a.reshape(nk // TILE, TILE, nq), axis=0)
    return op(part, axis=0, keepdims=True)


SCORE_SAFE = 60.0
BOUND_SLACK = 1.0 + 2.0 ** -6


def _attn_kernel(bound_ref, q_ref, k_ref, vt_ref, o_ref, *, n_lat_tiles, with_ctx, qoff, koff, vidx):
    n_keys = k_ref.shape[1]
    n_pairs = len(qoff) // 2
    zero = jnp.zeros((TILE, LANES), BF16)
    bound = bound_ref[0]

    def scores(row0, pair, key0, nk):
        e, o = 2 * pair, 2 * pair + 1
        assert koff[o] == koff[e] + LANES
        q_e = q_ref[0, pl.ds(row0, TILE), qoff[e]:qoff[e] + LANES]
        q_o = q_ref[0, pl.ds(row0, TILE), qoff[o]:qoff[o] + LANES]
        qd = jnp.concatenate([jnp.concatenate([q_e, zero], axis=1), jnp.concatenate([zero, q_o], axis=1)], axis=0)
        k2 = k_ref[0, key0:key0 + nk, koff[e]:koff[e] + 2 * LANES]
        return lax.dot_general(k2, qd, (((1,), (1,)), ((), ())), preferred_element_type=F32)

    def tile(row0, key0, nk, bounded):
        for pair in range(n_pairs):
            e, o = 2 * pair, 2 * pair + 1
            st = scores(row0, pair, key0, nk)
            pb = jnp.exp2(st if bounded else st - _reduce_keys(st, jnp.max)).astype(BF16)
            ot_e = _dot(vt_ref[0, vidx[e], :, key0:key0 + nk], pb[:, :TILE])
            ot_o = _dot(vt_ref[0, vidx[o], :, key0:key0 + nk], pb[:, TILE:])
            ot_e = ot_e * (1.0 / ot_e[_ONES_ROW_EVEN:_ONES_ROW_EVEN + 1])
            ot_o = ot_o * (1.0 / ot_o[_ONES_ROW_ODD:_ONES_ROW_ODD + 1])
            ot = jnp.where(lax.broadcasted_iota(jnp.int32, (LANES, TILE), 0) < 64, ot_e, ot_o)
            o_ref[0, pl.ds(row0, TILE), pair * LANES:(pair + 1) * LANES] = ot.T.astype(BF16)

    def all_tiles(bounded):
        def body(t, carry):
            tile(pl.multiple_of(t * TILE, TILE), 0, n_keys, bounded)
            return carry
        lax.fori_loop(0, n_lat_tiles, body, 0)
        if with_ctx:
            lat = n_lat_tiles * TILE
            tile(lat, lat, n_keys - lat, bounded)

    lax.cond(bound < SCORE_SAFE, lambda: all_tiles(True), lambda: all_tiles(False))


def _attention(shift, q, k, vt, n_lat_tiles, n_q_tiles, qoff, koff, vidx, name):
    b, t, wq = q.shape
    wk = k.shape[2]
    nh = vt.shape[1]
    kern = functools.partial(_attn_kernel, n_lat_tiles=n_lat_tiles, with_ctx=n_q_tiles > n_lat_tiles,
                             qoff=qoff, koff=koff, vidx=vidx)
    return pl.pallas_call(
        kern,
        grid=(b,),
        in_specs=[pl.BlockSpec(memory_space=pltpu.SMEM),
                  pl.BlockSpec((1, t, wq), lambda bi: (bi, 0, 0)),
                  pl.BlockSpec((1, t, wk), lambda bi: (bi, 0, 0)),
                  pl.BlockSpec((1, nh, LANES, t), lambda bi: (bi, 0, 0, 0))],
        out_specs=pl.BlockSpec((1, n_q_tiles * TILE, 512), lambda bi: (bi, 0, 0)),
        out_shape=jax.ShapeDtypeStruct((b, n_q_tiles * TILE, 512), BF16),
        compiler_params=_cparams(1),
        name=name,
    )(shift, q, k, vt)


def _fourier_kernel(pf_ref, cn_ref, sn_ref, dc_ref, cs_ref, flip_ref, alt_ref, o_ref, a_ref, b_ref, *, n_lat_tiles,
                    with_ctx):
    n_lat = n_lat_tiles * TILE
    half_tiles = n_lat_tiles // 2
    w = FOURIER_WIDTH
    for c in range(n_lat_tiles):
        ab = _dot(pf_ref[0, c * TILE:(c + 1) * TILE, :], cs_ref[...])
        a_ref[c * TILE:(c + 1) * TILE, :] = ab[:, :w].astype(BF16)
        b_ref[c * TILE:(c + 1) * TILE, :] = ab[:, w:].astype(BF16)
    row0 = lax.broadcasted_iota(jnp.int32, (TILE, w), 0) == 0
    carry = _dot(alt_ref[...], a_ref[...])[0:1] * (1.0 / math.sqrt(n_lat))
    for m in reversed(range(half_tiles)):
        g = _dot(cn_ref[m * TILE:(m + 1) * TILE, :], a_ref[...])
        h = _dot(sn_ref[m * TILE:(m + 1) * TILE, :], b_ref[...])
        o_ref[0, m * TILE:(m + 1) * TILE, :] = (g - h).astype(BF16)
        mirrored = (g + h).astype(BF16)
        tile_rev = _dot(flip_ref[...], mirrored)
        bot = n_lat_tiles - 1 - m
        o_ref[0, bot * TILE:(bot + 1) * TILE, :] = jnp.where(row0, carry, tile_rev).astype(BF16)
        carry = mirrored[0:1].astype(F32)
    if with_ctx:
        ab = _dot(pf_ref[0, n_lat:n_lat + TILE, :], cs_ref[...])
        abc = jnp.concatenate([ab[:, :w], ab[:, w:]], axis=0).astype(BF16)
        o_ref[0, n_lat:n_lat + TILE, :] = _dot(dc_ref[...], abc).astype(BF16)


def _fourier(pf, tables, n_lat_tiles, n_tiles):
    b, t, w = pf.shape
    n_lat = n_lat_tiles * TILE
    kern = functools.partial(_fourier_kernel, n_lat_tiles=n_lat_tiles, with_ctx=n_tiles > n_lat_tiles)
    return pl.pallas_call(
        kern,
        grid=(b,),
        in_specs=[pl.BlockSpec((1, t, w), lambda bi: (bi, 0, 0))] + [_const_spec(a) for a in tables],
        out_specs=pl.BlockSpec((1, n_tiles * TILE, w), lambda bi: (bi, 0, 0)),
        out_shape=jax.ShapeDtypeStruct((b, n_tiles * TILE, w), BF16),
        scratch_shapes=[pltpu.VMEM((n_lat, w), BF16), pltpu.VMEM((n_lat, w), BF16)],
        compiler_params=_cparams(1),
        name="fourier",
    )(pf, *tables)


def _merge_kernel(*refs, n_lat_tiles, sub):
    xl_refs, refs = refs[:sub], refs[sub:]
    xc_ref, refs = refs[0], refs[1:]
    mod_refs, refs = refs[:sub], refs[sub:]
    g1_ref, ya_ref, yb_ref, yc_ref, wg_ref, bg_ref, wa_ref, wb_ref, wc_ref, wo_ref, o_ref = refs
    for j in range(sub):
        rows = slice(j * TILE, (j + 1) * TILE)
        x = _tile_rows(xl_refs, xc_ref, j, n_lat_tiles)
        m = mod_refs[j][0, 0]
        d = x.shape[-1]
        hb = _modulated(x, g1_ref[0], m[0:1], m[1:2])
        acc = None
        for i, (y_ref, w_ref) in enumerate(((ya_ref, wa_ref), (yb_ref, wb_ref), (yc_ref, wc_ref))):
            gate = jax.nn.sigmoid(_dot(hb, wg_ref[0, :, i * d:(i + 1) * d]) + bg_ref[0, :, i * d:(i + 1) * d])
            term = gate * _dot(y_ref[0, rows], w_ref[0])
            acc = term if acc is None else acc + term
        o_ref[0, rows] = x + m[2:3] * _dot(acc.astype(BF16), wo_ref[0])


def _merge(x_lat, x_ctx, ctx_tile, mod, l, n_lat_tiles, n_tiles, g1, ya, yb, yc, wts):
    b, _, d = x_lat.shape
    sub = _sub_tiles(n_tiles)
    row = lambda bi, si: (bi, si, 0)
    return pl.pallas_call(
        functools.partial(_merge_kernel, n_lat_tiles=n_lat_tiles, sub=sub),
        grid=(b, n_tiles // sub),
        in_specs=_row_specs(x_lat, x_ctx, ctx_tile, n_lat_tiles, sub) + _mod_specs(mod, l, b, n_lat_tiles, sub)
        + [_layer_spec(g1, l)] + [pl.BlockSpec((1, sub * TILE, 512), row)] * 3 + [_layer_spec(w, l) for w in wts],
        out_specs=pl.BlockSpec((1, sub * TILE, d), row),
        out_shape=jax.ShapeDtypeStruct((b, n_tiles * TILE, d), F32),
        compiler_params=_cparams(2),
        name="merge",
    )(*([x_lat] * sub), x_ctx, *([mod] * sub), g1, ya, yb, yc, *wts)


def _ffn_kernel(*refs, sub):
    x_ref, refs = refs[0], refs[1:]
    mod_refs, refs = refs[:sub], refs[sub:]
    g2_ref, wi_ref, wo_ref, o_ref = refs
    for j in range(sub):
        rows = slice(j * TILE, (j + 1) * TILE)
        x = x_ref[0, rows]
        m = mod_refs[j][0, 0]
        hb = _modulated(x, g2_ref[0], m[3:4], m[4:5])
        gu = _dot(hb, wi_ref[0])
        hid = gu.shape[-1] // 2
        gate, up = gu[:, :hid], gu[:, hid:]
        act = (gate * jax.nn.sigmoid(gate) * up).astype(BF16)
        o_ref[0, rows] = x + m[5:6] * _dot(act, wo_ref[0])


def _ffn(xm, mod, l, n_lat_tiles, n_tiles, g2, wi, wo):
    b, _, d = xm.shape
    sub = _sub_tiles(n_tiles)
    row = lambda bi, si: (bi, si, 0)
    once = lambda a: pl.BlockSpec((1,) + a.shape[1:], lambda *_: (l,) + (0,) * (a.ndim - 1),
                                  pipeline_mode=pl.Buffered(1))
    return pl.pallas_call(
        functools.partial(_ffn_kernel, sub=sub),
        grid=(b, n_tiles // sub),
        in_specs=[pl.BlockSpec((1, sub * TILE, d), row)] + _mod_specs(mod, l, b, n_lat_tiles, sub)
        + [_layer_spec(g2, l), once(wi), once(wo)],
        out_specs=pl.BlockSpec((1, sub * TILE, d), row),
        out_shape=jax.ShapeDtypeStruct((b, n_tiles * TILE, d), F32),
        compiler_params=_cparams(2),
        name="ffn",
    )(xm, *([mod] * sub), g2, wi, wo)


def _partner(seg):
    lane = np.arange(LANES)
    return np.where((lane % (2 * seg)) < seg, lane + seg, lane - seg)


def _rope_tables(n_lat, n_ctx):
    rows = n_lat // GRID_W
    row_id = np.repeat(np.arange(rows), GRID_W).astype(np.float64)
    col_id = np.tile(np.arange(GRID_W), rows).astype(np.float64)

    def angles(dim):
        half = dim // 2
        freqs = ROPE_THETA ** (-np.arange(0, half, 2, dtype=np.float64) / half)
        ax = lambda pos: np.concatenate([pos[:, None] * freqs[None, :]] * 2, axis=-1)
        return np.concatenate([ax(row_id), ax(col_id)], axis=-1)

    def signed(sin, dim):
        sign = np.where((np.arange(dim) % (dim // 2)) < dim // 4, -1.0, 1.0)
        return sin * sign[None, :]

    a64, a32 = angles(GQA_HEAD_DIM), angles(MLA_ROPE_DIM)
    t = n_lat + n_ctx
    cosa, sina = np.ones((t, LANES)), np.zeros((t, LANES))
    cosa[:n_lat] = np.tile(np.cos(a64), (1, 2))
    sina[:n_lat] = np.tile(signed(np.sin(a64), GQA_HEAD_DIM), (1, 2))
    cosb, sinb = np.ones((t, LANES)), np.zeros((t, LANES))
    cosb[:n_lat, 64:96] = np.cos(a32)
    sinb[:n_lat, 64:96] = signed(np.sin(a32), MLA_ROPE_DIM)
    return jnp.asarray(np.stack([cosa, cosa, cosb, cosb, sina, sina, sinb, sinb]), F32)


TAB_Q, TAB_K, TAB_QB, TAB_KR, N_TABLES = 0, 1, 2, 3, 4


def _table_gains(gains, scales):
    g = jnp.stack(gains, axis=1) * jnp.asarray(scales, F32)[None, :, None]
    partner = np.stack([_partner(16), _partner(16), _partner(8), _partner(8)])
    return jnp.concatenate([g, jnp.take_along_axis(g, jnp.asarray(partner)[None], axis=-1)], axis=1)


def _seg_matrices():
    def blockdiag(segs):
        m = np.zeros((2 * LANES, 2 * LANES))
        for start, n in segs:
            m[start:start + n, start:start + n] = 1.0 / n
        return m

    gqa = blockdiag([(s0, 64) for s0 in range(0, 256, 64)])
    qb = blockdiag([(0, 64), (64, 32), (128, 64), (192, 32)])
    kn = blockdiag([(0, 64), (128, 64)])
    kk = blockdiag([(0, 64), (64, 64), (192, 32)])
    return [jnp.asarray(a, F32).astype(BF16) for a in (gqa, qb, kn, kk)]


def _score_bounds(g_q_gqa, g_k_gqa, g_q_nope, g_k_nope, g_q_rope, g_k_rope):
    top = lambda g: jnp.max(g * g, axis=-1)
    gqa = jnp.sqrt(GQA_HEAD_DIM * top(g_q_gqa) * GQA_HEAD_DIM * top(g_k_gqa)) * (GQA_SCALE * LOG2E)
    mla = jnp.sqrt((MLA_NOPE_DIM * top(g_q_nope) + MLA_ROPE_DIM * top(g_q_rope))
                   * (MLA_NOPE_DIM * top(g_k_nope) + MLA_ROPE_DIM * top(g_k_rope))) * (MLA_SCALE * LOG2E)
    return jnp.stack([gqa, mla], axis=-1) * BOUND_SLACK


def _dft_tables(n_lat, n_ctx):
    def cs(n):
        j = np.arange(n)
        ang = 2.0 * np.pi * ((j[:, None] * j[None, :]) % n) / n
        return np.cos(ang) / np.sqrt(n), np.sin(ang) / np.sqrt(n)

    cn, sn = cs(n_lat)
    cc, sc = cs(n_ctx)
    cg, sg = cs(FOURIER_GROUP_DIM)
    eye = np.eye(FOURIER_GROUPS)
    dc = np.concatenate([cc, -sc], axis=1)
    chan = np.concatenate([np.kron(eye, cg), np.kron(eye, sg)], axis=1)
    flip = np.zeros((TILE, TILE))
    flip[np.arange(1, TILE), TILE - np.arange(1, TILE)] = 1.0
    alt = np.tile(np.where(np.arange(n_lat) % 2 == 0, 1.0, -1.0)[None, :], (8, 1))
    return [jnp.asarray(a, F32).astype(BF16) for a in (cn[:n_lat // 2], sn[:n_lat // 2], dc, chan, flip, alt)]


def _inproj_weights(w_in, g_q_gqa, g_k_gqa, g_cq, g_ckv, w_uq, w_ukv, g_q_nope, g_k_nope, g_q_rope, g_k_rope):
    depth, d, _ = w_in.shape
    wb = w_in.astype(BF16)
    q, k, v, cq, ckv, kr, f, gates = (wb[..., 0:512], wb[..., 512:640], wb[..., 640:768], wb[..., 768:1152],
                                      wb[..., 1152:1408], wb[..., 1408:1440], wb[..., 1440:1952], wb[..., 1952:])
    zeros = lambda *s: jnp.zeros((depth,) + s, F32)
    zb = lambda n: jnp.zeros((depth, d, n), BF16)
    wa = jnp.concatenate([q, k, v, cq, zb(64), kr, zb(32), ckv, f], axis=-1)
    uq = w_uq.reshape(depth, MLA_Q_RANK, MLA_HEADS, MLA_NOPE_DIM + MLA_ROPE_DIM)
    wuq = jnp.concatenate([uq, zeros(MLA_Q_RANK, MLA_HEADS, 32)], axis=-1).reshape(depth, MLA_Q_RANK, -1)
    ukv = w_ukv.reshape(depth, MLA_KV_RANK, MLA_HEADS, MLA_NOPE_DIM + MLA_V_DIM)
    kn, vv = ukv[..., :MLA_NOPE_DIM], ukv[..., MLA_NOPE_DIM:]
    z64 = zeros(MLA_KV_RANK, MLA_HEADS, 64)
    wkn = jnp.concatenate([kn, z64], axis=-1).reshape(depth, MLA_KV_RANK, -1)
    even = (jnp.arange(MLA_HEADS) % 2 == 0)[None, None, :, None]
    wv = jnp.where(even, jnp.concatenate([vv, z64], axis=-1), jnp.concatenate([z64, vv], axis=-1))
    wvt = jnp.swapaxes(wv.reshape(depth, MLA_KV_RANK, -1), 1, 2)
    gkn = jnp.concatenate([g_k_nope, zeros(64)], axis=-1)[:, None, :]
    z32, z64v = zeros(32), zeros(64)
    tgains = _table_gains([jnp.tile(g_q_gqa, (1, 2)), jnp.tile(g_k_gqa, (1, 2)),
                           jnp.concatenate([g_q_nope, g_q_rope, z32], axis=-1),
                           jnp.concatenate([z64v, g_k_rope, z32], axis=-1)],
                          [GQA_SCALE * LOG2E, 1.0, MLA_SCALE * LOG2E, 1.0])
    wts = [wa, wuq.astype(BF16), wkn.astype(BF16), wvt.astype(BF16), g_cq[:, None, :], g_ckv[:, None, :], gkn,
           tgains]
    return wts, gates


def kernel(x, c, ctx, c_ctx, w_mod, b_mod, g_norm1, g_norm2, w_in, g_q_gqa, g_k_gqa, g_cq, g_ckv, w_uq, w_ukv,
           g_q_nope, g_k_nope, g_q_rope, g_k_rope, b_gate, w_br_a, w_br_b, w_br_c, w_out, w_ffn_in, w_ffn_out):
    b, n, d = x.shape
    nc = ctx.shape[1]
    depth = w_mod.shape[0]
    assert n % TILE == 0 and nc == TILE and b + 1 <= MOD_ROWS and n % GRID_W == 0
    n_lat_tiles, n_tiles = n // TILE, (n + nc) // TILE

    cc = jnp.concatenate([c, c_ctx[None], jnp.zeros((MOD_ROWS - b - 1, d), F32)], axis=0)
    mod = _modulation(cc, w_mod, b_mod).reshape(depth, MOD_ROWS, 6, d)
    dft = _dft_tables(n, nc)
    segs = _seg_matrices()
    bounds = _score_bounds(g_q_gqa, g_k_gqa, g_q_nope, g_k_nope, g_q_rope, g_k_rope)
    tabs = _rope_tables(n, nc)
    wts, wg = _inproj_weights(w_in, g_q_gqa, g_k_gqa, g_cq, g_ckv, w_uq, w_ukv, g_q_nope, g_k_nope, g_q_rope,
                              g_k_rope)
    g1, g2 = g_norm1[:, None, :], g_norm2[:, None, :]
    mw = [wg, b_gate[:, None, :], w_br_a.astype(BF16), w_br_b.astype(BF16), w_br_c.astype(BF16), w_out.astype(BF16)]
    wi, wo = w_ffn_in.astype(BF16), w_ffn_out.astype(BF16)
    gqa_q = (0, 0, 128, 128, 256, 256, 384, 384)
    gqa_k = (0, 128, 0, 128, 256, 384, 256, 384)
    gqa_v = (0, 1, 0, 1, 2, 3, 2, 3)
    mla_o = tuple(range(0, 1024, 128))
    mla_h = tuple(range(MLA_HEADS))

    x_lat, x_ctx, ctx_tile = x, ctx, 0
    for l in range(depth):
        n_out = n_lat_tiles if l == depth - 1 else n_tiles
        qa, ka, vat, qb, kb, vbt, pf = _inproj(x_lat, x_ctx, ctx_tile, mod, l, n_lat_tiles, g1, wts, tabs, segs)
        ya = _attention(bounds[l, 0:1], qa, ka, vat, n_lat_tiles, n_out, gqa_q, gqa_k, gqa_v, "attn_gqa")
        yb = _attention(bounds[l, 1:2], qb, kb, vbt, n_lat_tiles, n_out, mla_o, mla_o, mla_h, "attn_mla")
        yc = _fourier(pf, dft, n_lat_tiles, n_out)
        xm = _merge(x_lat, x_ctx, ctx_tile, mod, l, n_lat_tiles, n_out, g1, ya, yb, yc, mw)
        xall = _ffn(xm, mod, l, n_lat_tiles, n_out, g2, wi, wo)
        x_lat, x_ctx, ctx_tile = xall, xall, n_lat_tiles
    return xall
```

```python
import functools
import math

import numpy as np
import jax
import jax.numpy as jnp
from jax import lax
from jax.experimental import pallas as pl
from jax.experimental.pallas import tpu as pltpu

GRID_W = 64
ROPE_THETA = 10000.0
EPS = 1e-6
GQA_HEADS = 8
GQA_KV_HEADS = 2
GQA_HEAD_DIM = 64
GQA_SCALE = GQA_HEAD_DIM ** -0.5
MLA_HEADS = 8
MLA_Q_RANK = 384
MLA_KV_RANK = 256
MLA_NOPE_DIM = 64
MLA_ROPE_DIM = 32
MLA_V_DIM = 64
MLA_SCALE = (MLA_NOPE_DIM + MLA_ROPE_DIM) ** -0.5
FOURIER_GROUPS = 4
FOURIER_GROUP_DIM = 128
FOURIER_WIDTH = FOURIER_GROUPS * FOURIER_GROUP_DIM
N_BRANCH = 3
LOG2E = math.log2(math.e)

LANES = 128
TILE = 256
MOD_ROWS = 16
VMEM_LIMIT = 56 * 1024 * 1024

BF16 = jnp.bfloat16
F32 = jnp.float32

_ONES_ROW_EVEN, _ONES_ROW_ODD = LANES - 1, 0

_Q0, _K0, _V0, _CQ0, _KR0, _CKV0, _F0, _WA = 0, 512, 640, 768, 1152, 1280, 1536, 2048


def _cparams(n_axes):
    return pltpu.CompilerParams(dimension_semantics=("arbitrary",) * n_axes, vmem_limit_bytes=VMEM_LIMIT)


def _dot(a, b):
    return jnp.dot(a, b, preferred_element_type=F32)


def _rms_rows(xv, g):
    return xv * lax.rsqrt(jnp.mean(xv * xv, axis=-1, keepdims=True) + EPS) * g


def _modulated(x, g, shift, scale):
    r = lax.rsqrt(jnp.mean(x * x, axis=-1, keepdims=True) + EPS)
    return (x * r * (g * (1.0 + scale)) + shift).astype(BF16)


def _layer_spec(a, l):
    return pl.BlockSpec((1,) + a.shape[1:], lambda *_: (l,) + (0,) * (a.ndim - 1))


def _const_spec(a):
    return pl.BlockSpec(a.shape, lambda *_: (0,) * a.ndim)


def _sub_tiles(n_tiles):
    return next(s for s in (3, 2, 1) if n_tiles % s == 0)


def _row_specs(x_lat, x_ctx, ctx_tile, n_lat_tiles, sub):
    d = x_lat.shape[-1]
    lat = lambda j: pl.BlockSpec((1, TILE, d), lambda bi, si: (bi, jnp.minimum(si * sub + j, n_lat_tiles - 1), 0))
    return [lat(j) for j in range(sub)] + [pl.BlockSpec((1, TILE, d), lambda bi, si: (bi, ctx_tile, 0))]


def _mod_specs(mod, l, n_batch, n_lat_tiles, sub):
    spec = lambda j: pl.BlockSpec((1, 1) + mod.shape[2:],
                                  lambda bi, si: (l, jnp.where(si * sub + j >= n_lat_tiles, n_batch, bi), 0, 0))
    return [spec(j) for j in range(sub)]


def _tile_rows(xl_refs, xc_ref, j, n_lat_tiles):
    return jnp.where(pl.program_id(1) * len(xl_refs) + j < n_lat_tiles, xl_refs[j][0], xc_ref[0])


def _mod_kernel(cc_ref, w_ref, b_ref, o_ref):
    s = cc_ref[...]
    s = s * jax.nn.sigmoid(s)
    o_ref[0] = _dot(s.astype(BF16), w_ref[0].astype(BF16)) + b_ref[0]


def _modulation(cc, w_mod, b_mod):
    depth, d, d6 = w_mod.shape
    bn = 1536
    return pl.pallas_call(
        _mod_kernel,
        grid=(depth, d6 // bn),
        in_specs=[pl.BlockSpec((MOD_ROWS, d), lambda l, j: (0, 0)),
                  pl.BlockSpec((1, d, bn), lambda l, j: (l, 0, j)),
                  pl.BlockSpec((1, 1, bn), lambda l, j: (l, 0, j))],
        out_specs=pl.BlockSpec((1, MOD_ROWS, bn), lambda l, j: (l, 0, j)),
        out_shape=jax.ShapeDtypeStruct((depth, MOD_ROWS, d6), F32),
        compiler_params=_cparams(2),
        name="modulation",
    )(cc, w_mod, b_mod.reshape(depth, 1, d6))


def _rot_half(xv, lane, seg):
    first = (lane & (2 * seg - 1)) < seg
    return jnp.where(first, pltpu.roll(xv, LANES - seg, 1), pltpu.roll(xv, seg, 1))


def _seg_rsqrt(xw, seg_ref):
    return lax.rsqrt(_dot((xw * xw).astype(BF16), seg_ref[...]) + EPS)


def _inproj_kernel(*refs, n_lat_tiles, sub):
    xl_refs, refs = refs[:sub], refs[sub:]
    xc_ref, refs = refs[0], refs[1:]
    mod_refs, refs = refs[:sub], refs[sub:]
    (g1_ref, wa_ref, wuq_ref, wkn_ref, wvt_ref, gcq_ref, gckv_ref, gkn_ref, tg_ref, tab_ref,
     sgqa_ref, sqb_ref, skn_ref, skk_ref, qa_ref, ka_ref, vat_ref, qb_ref, kb_ref, vbt_ref, pf_ref) = refs
    lane = lax.broadcasted_iota(jnp.int32, (1, LANES), 1)
    lo = lane < 64
    hi = lane >= 64
    row = lax.broadcasted_iota(jnp.int32, (LANES, TILE), 0)
    blk = lambda a, j: a[:, j * LANES:(j + 1) * LANES]
    n_qblk = GQA_HEADS * GQA_HEAD_DIM // LANES
    w2 = 2 * LANES

    def project(j):
        rows = slice(j * TILE, (j + 1) * TILE)
        m = mod_refs[j][0, 0]
        hb = _modulated(_tile_rows(xl_refs, xc_ref, j, n_lat_tiles), g1_ref[0], m[0:1], m[1:2])
        p_c = _dot(hb, wa_ref[0, :, _CQ0:_CKV0])
        p_ckv = _dot(hb, wa_ref[0, :, _CKV0:_F0])
        pq = _dot(hb, wa_ref[0, :, _Q0:_K0])
        pkv = _dot(hb, wa_ref[0, :, _K0:_CQ0])
        pf_ref[0, rows] = _dot(hb, wa_ref[0, :, _F0:_WA]).astype(BF16)
        cq = _rms_rows(p_c[:, :MLA_Q_RANK], gcq_ref[0]).astype(BF16)
        ckv = _rms_rows(p_ckv, gckv_ref[0])
        qb = _dot(cq, wuq_ref[0])
        kn_all = _dot(ckv.astype(BF16), wkn_ref[0])
        vbt = _dot(wvt_ref[0], ckv.T.astype(BF16))
        for h in range(MLA_HEADS):
            vh = vbt[h * LANES:(h + 1) * LANES]
            ones_row = _ONES_ROW_EVEN if h % 2 == 0 else _ONES_ROW_ODD
            vbt_ref[0, h, :, rows] = jnp.where(row == ones_row, 1.0, vh).astype(BF16)
        vt = pkv[:, LANES:].T
        for kvh in range(GQA_KV_HEADS):
            own = (row >= 64) == (kvh == 1)
            top = jnp.where(own, vt, 0.0) if kvh == 0 else pltpu.roll(jnp.where(own, vt, 0.0), 64, 0)
            bot = pltpu.roll(top, 64, 0)
            vat_ref[0, 2 * kvh, :, rows] = jnp.where(row == _ONES_ROW_EVEN, 1.0, top).astype(BF16)
            vat_ref[0, 2 * kvh + 1, :, rows] = jnp.where(row == _ONES_ROW_ODD, 1.0, bot).astype(BF16)
        return pq, pkv[:, :LANES], qb, kn_all, p_c[:, MLA_Q_RANK:]

    def finish(j, pq, xk, qb, kn_all, xkr):
        rows = slice(j * TILE, (j + 1) * TILE)
        r_qb = jnp.concatenate([_seg_rsqrt(qb[:, c * w2:(c + 1) * w2], sqb_ref) for c in range(MLA_HEADS // 2)],
                               axis=1)
        r_kn = jnp.concatenate([_seg_rsqrt(kn_all[:, c * w2:(c + 1) * w2], skn_ref) for c in range(MLA_HEADS // 2)],
                               axis=1)
        r_q = jnp.concatenate([_seg_rsqrt(pq[:, c * w2:(c + 1) * w2], sgqa_ref) for c in range(n_qblk // 2)], axis=1)
        r_kk = _seg_rsqrt(jnp.concatenate([xk, xkr], axis=1), skk_ref)
        r_k, r_kr = r_kk[:, :LANES], r_kk[:, LANES:]

        tab = [tab_ref[i, rows] * tg_ref[0, i:i + 1] for i in range(2 * N_TABLES)]

        def roped(xb, table, seg, r):
            return (xb * tab[table] + _rot_half(xb, lane, seg) * tab[N_TABLES + table]) * r

        for h in range(MLA_HEADS):
            qb_ref[0, rows, h * LANES:(h + 1) * LANES] = roped(blk(qb, h), TAB_QB, 8, blk(r_qb, h)).astype(BF16)
        kr = roped(xkr, TAB_KR, 8, r_kr)
        for h in range(MLA_HEADS):
            kb_ref[0, rows, h * LANES:(h + 1) * LANES] = (blk(kn_all, h) * blk(r_kn, h) * gkn_ref[0] + kr).astype(BF16)
        for c in range(n_qblk):
            qa_ref[0, rows, c * LANES:(c + 1) * LANES] = roped(blk(pq, c), TAB_Q, 16, blk(r_q, c)).astype(BF16)
        kn = roped(xk, TAB_K, 16, r_k)
        sw = pltpu.roll(kn, 64, 1)
        for c, f in enumerate((jnp.where(lo, kn, 0.0), jnp.where(hi, sw, 0.0), jnp.where(lo, sw, 0.0),
                               jnp.where(hi, kn, 0.0))):
            ka_ref[0, rows, c * LANES:(c + 1) * LANES] = f.astype(BF16)

    pending = project(0)
    for j in range(sub):
        nxt = project(j + 1) if j + 1 < sub else None
        finish(j, *pending)
        pending = nxt


def _inproj(x_lat, x_ctx, ctx_tile, mod, l, n_lat_tiles, g1, wts, tabs, segs):
    b = x_lat.shape[0]
    nt = n_lat_tiles + 1
    sub = _sub_tiles(nt)
    t, step_rows = nt * TILE, sub * TILE
    row = lambda bi, si: (bi, si, 0)
    rowt = lambda bi, si: (bi, 0, 0, si)
    tok = lambda w: (pl.BlockSpec((1, step_rows, w), row), jax.ShapeDtypeStruct((b, t, w), BF16))
    tra = lambda h: (pl.BlockSpec((1, h, LANES, step_rows), rowt), jax.ShapeDtypeStruct((b, h, LANES, t), BF16))
    outs = [tok(512), tok(512), tra(4), tok(1024), tok(1024), tra(MLA_HEADS), tok(512)]
    return pl.pallas_call(
        functools.partial(_inproj_kernel, n_lat_tiles=n_lat_tiles, sub=sub),
        grid=(b, nt // sub),
        in_specs=_row_specs(x_lat, x_ctx, ctx_tile, n_lat_tiles, sub) + _mod_specs(mod, l, b, n_lat_tiles, sub)
        + [_layer_spec(g1, l)] + [_layer_spec(w, l) for w in wts]
        + [pl.BlockSpec((2 * N_TABLES, step_rows, LANES), lambda bi, si: (0, si, 0))]
        + [_const_spec(s) for s in segs],
        out_specs=[o[0] for o in outs],
        out_shape=[o[1] for o in outs],
        compiler_params=_cparams(2),
        name="inproj",
    )(*([x_lat] * sub), x_ctx, *([mod] * sub), g1, *wts, tabs, *segs)


def _reduce_keys(a, op):
    nk, nq = a.shape
    part = op(a.reshape(nk // TILE, TILE, nq), axis=0)
    return op(part, axis=0, keepdims=True)


SCORE_SAFE = 60.0
BOUND_SLACK = 1.0 + 2.0 ** -6


def _attn_kernel(bound_ref, q_ref, k_ref, vt_ref, o_ref, *, n_lat_tiles, with_ctx, qoff, koff, vidx):
    n_keys = k_ref.shape[1]
    n_pairs = len(qoff) // 2
    zero = jnp.zeros((TILE, LANES), BF16)
    bound = bound_ref[0]

    def scores(row0, pair, key0, nk):
        e, o = 2 * pair, 2 * pair + 1
        assert koff[o] == koff[e] + LANES
        q_e = q_ref[0, pl.ds(row0, TILE), qoff[e]:qoff[e] + LANES]
        q_o = q_ref[0, pl.ds(row0, TILE), qoff[o]:qoff[o] + LANES]
        qd = jnp.concatenate([jnp.concatenate([q_e, zero], axis=1), jnp.concatenate([zero, q_o], axis=1)], axis=0)
        k2 = k_ref[0, key0:key0 + nk, koff[e]:koff[e] + 2 * LANES]
        return lax.dot_general(k2, qd, (((1,), (1,)), ((), ())), preferred_element_type=F32)

    def tile(row0, key0, nk, bounded):
        for pair in range(n_pairs):
            e, o = 2 * pair, 2 * pair + 1
            st = scores(row0, pair, key0, nk)
            pb = jnp.exp2(st if bounded else st - _reduce_keys(st, jnp.max)).astype(BF16)
            ot_e = _dot(vt_ref[0, vidx[e], :, key0:key0 + nk], pb[:, :TILE])
            ot_o = _dot(vt_ref[0, vidx[o], :, key0:key0 + nk], pb[:, TILE:])
            ot_e = ot_e * (1.0 / ot_e[_ONES_ROW_EVEN:_ONES_ROW_EVEN + 1])
            ot_o = ot_o * (1.0 / ot_o[_ONES_ROW_ODD:_ONES_ROW_ODD + 1])
            ot = jnp.where(lax.broadcasted_iota(jnp.int32, (LANES, TILE), 0) < 64, ot_e, ot_o)
            o_ref[0, pl.ds(row0, TILE), pair * LANES:(pair + 1) * LANES] = ot.T.astype(BF16)

    def all_tiles(bounded):
        def body(t, carry):
            tile(pl.multiple_of(t * TILE, TILE), 0, n_keys, bounded)
            return carry
        lax.fori_loop(0, n_lat_tiles, body, 0, unroll=2 if bounded else 1)
        if with_ctx:
            lat = n_lat_tiles * TILE
            tile(lat, lat, n_keys - lat, bounded)

    lax.cond(bound < SCORE_SAFE, lambda: all_tiles(True), lambda: all_tiles(False))


def _attention(shift, q, k, vt, n_lat_tiles, n_q_tiles, qoff, koff, vidx, name):
    b, t, wq = q.shape
    wk = k.shape[2]
    nh = vt.shape[1]
    kern = functools.partial(_attn_kernel, n_lat_tiles=n_lat_tiles, with_ctx=n_q_tiles > n_lat_tiles,
                             qoff=qoff, koff=koff, vidx=vidx)
    return pl.pallas_call(
        kern,
        grid=(b,),
        in_specs=[pl.BlockSpec(memory_space=pltpu.SMEM),
                  pl.BlockSpec((1, t, wq), lambda bi: (bi, 0, 0)),
                  pl.BlockSpec((1, t, wk), lambda bi: (bi, 0, 0)),
                  pl.BlockSpec((1, nh, LANES, t), lambda bi: (bi, 0, 0, 0))],
        out_specs=pl.BlockSpec((1, n_q_tiles * TILE, 512), lambda bi: (bi, 0, 0)),
        out_shape=jax.ShapeDtypeStruct((b, n_q_tiles * TILE, 512), BF16),
        compiler_params=_cparams(1),
        name=name,
    )(shift, q, k, vt)


def _fourier_kernel(pf_ref, cn_ref, sn_ref, dc_ref, cs_ref, flip_ref, alt_ref, o_ref, a_ref, b_ref, *, n_lat_tiles,
                    with_ctx):
    n_lat = n_lat_tiles * TILE
    half_tiles = n_lat_tiles // 2
    w = FOURIER_WIDTH
    for c in range(n_lat_tiles):
        ab = _dot(pf_ref[0, c * TILE:(c + 1) * TILE, :], cs_ref[...])
        a_ref[c * TILE:(c + 1) * TILE, :] = ab[:, :w].astype(BF16)
        b_ref[c * TILE:(c + 1) * TILE, :] = ab[:, w:].astype(BF16)
    row0 = lax.broadcasted_iota(jnp.int32, (TILE, w), 0) == 0
    carry = _dot(alt_ref[...], a_ref[...])[0:1] * (1.0 / math.sqrt(n_lat))
    for m in reversed(range(half_tiles)):
        g = _dot(cn_ref[m * TILE:(m + 1) * TILE, :], a_ref[...])
        h = _dot(sn_ref[m * TILE:(m + 1) * TILE, :], b_ref[...])
        o_ref[0, m * TILE:(m + 1) * TILE, :] = (g - h).astype(BF16)
        mirrored = (g + h).astype(BF16)
        tile_rev = _dot(flip_ref[...], mirrored)
        bot = n_lat_tiles - 1 - m
        o_ref[0, bot * TILE:(bot + 1) * TILE, :] = jnp.where(row0, carry, tile_rev).astype(BF16)
        carry = mirrored[0:1].astype(F32)
    if with_ctx:
        ab = _dot(pf_ref[0, n_lat:n_lat + TILE, :], cs_ref[...])
        abc = jnp.concatenate([ab[:, :w], ab[:, w:]], axis=0).astype(BF16)
        o_ref[0, n_lat:n_lat + TILE, :] = _dot(dc_ref[...], abc).astype(BF16)


def _fourier(pf, tables, n_lat_tiles, n_tiles):
    b, t, w = pf.shape
    n_lat = n_lat_tiles * TILE
    kern = functools.partial(_fourier_kernel, n_lat_tiles=n_lat_tiles, with_ctx=n_tiles > n_lat_tiles)
    return pl.pallas_call(
        kern,
        grid=(b,),
        in_specs=[pl.BlockSpec((1, t, w), lambda bi: (bi, 0, 0))] + [_const_spec(a) for a in tables],
        out_specs=pl.BlockSpec((1, n_tiles * TILE, w), lambda bi: (bi, 0, 0)),
        out_shape=jax.ShapeDtypeStruct((b, n_tiles * TILE, w), BF16),
        scratch_shapes=[pltpu.VMEM((n_lat, w), BF16), pltpu.VMEM((n_lat, w), BF16)],
        compiler_params=_cparams(1),
        name="fourier",
    )(pf, *tables)


def _merge_kernel(*refs, n_lat_tiles, sub):
    xl_refs, refs = refs[:sub], refs[sub:]
    xc_ref, refs = refs[0], refs[1:]
    mod_refs, refs = refs[:sub], refs[sub:]
    g1_ref, ya_ref, yb_ref, yc_ref, wg_ref, bg_ref, wa_ref, wb_ref, wc_ref, wo_ref, o_ref = refs
    for j in range(sub):
        rows = slice(j * TILE, (j + 1) * TILE)
        x = _tile_rows(xl_refs, xc_ref, j, n_lat_tiles)
        m = mod_refs[j][0, 0]
        d = x.shape[-1]
        hb = _modulated(x, g1_ref[0], m[0:1], m[1:2])
        acc = None
        for i, (y_ref, w_ref) in enumerate(((ya_ref, wa_ref), (yb_ref, wb_ref), (yc_ref, wc_ref))):
            gate = jax.nn.sigmoid(_dot(hb, wg_ref[0, :, i * d:(i + 1) * d]) + bg_ref[0, :, i * d:(i + 1) * d])
            term = gate * _dot(y_ref[0, rows], w_ref[0])
            acc = term if acc is None else acc + term
        o_ref[0, rows] = x + m[2:3] * _dot(acc.astype(BF16), wo_ref[0])


def _merge(x_lat, x_ctx, ctx_tile, mod, l, n_lat_tiles, n_tiles, g1, ya, yb, yc, wts):
    b, _, d = x_lat.shape
    sub = _sub_tiles(n_tiles)
    row = lambda bi, si: (bi, si, 0)
    return pl.pallas_call(
        functools.partial(_merge_kernel, n_lat_tiles=n_lat_tiles, sub=sub),
        grid=(b, n_tiles // sub),
        in_specs=_row_specs(x_lat, x_ctx, ctx_tile, n_lat_tiles, sub) + _mod_specs(mod, l, b, n_lat_tiles, sub)
        + [_layer_spec(g1, l)] + [pl.BlockSpec((1, sub * TILE, 512), row)] * 3 + [_layer_spec(w, l) for w in wts],
        out_specs=pl.BlockSpec((1, sub * TILE, d), row),
        out_shape=jax.ShapeDtypeStruct((b, n_tiles * TILE, d), F32),
        compiler_params=_cparams(2),
        name="merge",
    )(*([x_lat] * sub), x_ctx, *([mod] * sub), g1, ya, yb, yc, *wts)


def _ffn_kernel(*refs, sub):
    x_ref, refs = refs[0], refs[1:]
    mod_refs, refs = refs[:sub], refs[sub:]
    g2_ref, wi_ref, wo_ref, o_ref = refs
    for j in range(sub):
        rows = slice(j * TILE, (j + 1) * TILE)
        x = x_ref[0, rows]
        m = mod_refs[j][0, 0]
        hb = _modulated(x, g2_ref[0], m[3:4], m[4:5])
        gu = _dot(hb, wi_ref[0])
        hid = gu.shape[-1] // 2
        gate, up = gu[:, :hid], gu[:, hid:]
        act = (gate * jax.nn.sigmoid(gate) * up).astype(BF16)
        o_ref[0, rows] = x + m[5:6] * _dot(act, wo_ref[0])


def _ffn(xm, mod, l, n_lat_tiles, n_tiles, g2, wi, wo):
    b, _, d = xm.shape
    sub = _sub_tiles(n_tiles)
    row = lambda bi, si: (bi, si, 0)
    once = lambda a: pl.BlockSpec((1,) + a.shape[1:], lambda *_: (l,) + (0,) * (a.ndim - 1),
                                  pipeline_mode=pl.Buffered(1))
    return pl.pallas_call(
        functools.partial(_ffn_kernel, sub=sub),
        grid=(b, n_tiles // sub),
        in_specs=[pl.BlockSpec((1, sub * TILE, d), row)] + _mod_specs(mod, l, b, n_lat_tiles, sub)
        + [_layer_spec(g2, l), once(wi), once(wo)],
        out_specs=pl.BlockSpec((1, sub * TILE, d), row),
        out_shape=jax.ShapeDtypeStruct((b, n_tiles * TILE, d), F32),
        compiler_params=_cparams(2),
        name="ffn",
    )(xm, *([mod] * sub), g2, wi, wo)


def _partner(seg):
    lane = np.arange(LANES)
    return np.where((lane % (2 * seg)) < seg, lane + seg, lane - seg)


def _rope_tables(n_lat, n_ctx):
    rows = n_lat // GRID_W
    row_id = np.repeat(np.arange(rows), GRID_W).astype(np.float64)
    col_id = np.tile(np.arange(GRID_W), rows).astype(np.float64)

    def angles(dim):
        half = dim // 2
        freqs = ROPE_THETA ** (-np.arange(0, half, 2, dtype=np.float64) / half)
        ax = lambda pos: np.concatenate([pos[:, None] * freqs[None, :]] * 2, axis=-1)
        return np.concatenate([ax(row_id), ax(col_id)], axis=-1)

    def signed(sin, dim):
        sign = np.where((np.arange(dim) % (dim // 2)) < dim // 4, -1.0, 1.0)
        return sin * sign[None, :]

    a64, a32 = angles(GQA_HEAD_DIM), angles(MLA_ROPE_DIM)
    t = n_lat + n_ctx
    cosa, sina = np.ones((t, LANES)), np.zeros((t, LANES))
    cosa[:n_lat] = np.tile(np.cos(a64), (1, 2))
    sina[:n_lat] = np.tile(signed(np.sin(a64), GQA_HEAD_DIM), (1, 2))
    cosb, sinb = np.ones((t, LANES)), np.zeros((t, LANES))
    cosb[:n_lat, 64:96] = np.cos(a32)
    sinb[:n_lat, 64:96] = signed(np.sin(a32), MLA_ROPE_DIM)
    return jnp.asarray(np.stack([cosa, cosa, cosb, cosb, sina, sina, sinb, sinb]), F32)


TAB_Q, TAB_K, TAB_QB, TAB_KR, N_TABLES = 0, 1, 2, 3, 4


def _table_gains(gains, scales):
    g = jnp.stack(gains, axis=1) * jnp.asarray(scales, F32)[None, :, None]
    partner = np.stack([_partner(16), _partner(16), _partner(8), _partner(8)])
    return jnp.concatenate([g, jnp.take_along_axis(g, jnp.asarray(partner)[None], axis=-1)], axis=1)


def _seg_matrices():
    def blockdiag(segs):
        m = np.zeros((2 * LANES, 2 * LANES))
        for start, n in segs:
            m[start:start + n, start:start + n] = 1.0 / n
        return m

    gqa = blockdiag([(s0, 64) for s0 in range(0, 256, 64)])
    qb = blockdiag([(0, 64), (64, 32), (128, 64), (192, 32)])
    kn = blockdiag([(0, 64), (128, 64)])
    kk = blockdiag([(0, 64), (64, 64), (192, 32)])
    return [jnp.asarray(a, F32).astype(BF16) for a in (gqa, qb, kn, kk)]


def _score_bounds(g_q_gqa, g_k_gqa, g_q_nope, g_k_nope, g_q_rope, g_k_rope):
    top = lambda g: jnp.max(g * g, axis=-1)
    gqa = jnp.sqrt(GQA_HEAD_DIM * top(g_q_gqa) * GQA_HEAD_DIM * top(g_k_gqa)) * (GQA_SCALE * LOG2E)
    mla = jnp.sqrt((MLA_NOPE_DIM * top(g_q_nope) + MLA_ROPE_DIM * top(g_q_rope))
                   * (MLA_NOPE_DIM * top(g_k_nope) + MLA_ROPE_DIM * top(g_k_rope))) * (MLA_SCALE * LOG2E)
    return jnp.stack([gqa, mla], axis=-1) * BOUND_SLACK


def _dft_tables(n_lat, n_ctx):
    def cs(n):
        j = np.arange(n)
        ang = 2.0 * np.pi * ((j[:, None] * j[None, :]) % n) / n
        return np.cos(ang) / np.sqrt(n), np.sin(ang) / np.sqrt(n)

    cn, sn = cs(n_lat)
    cc, sc = cs(n_ctx)
    cg, sg = cs(FOURIER_GROUP_DIM)
    eye = np.eye(FOURIER_GROUPS)
    dc = np.concatenate([cc, -sc], axis=1)
    chan = np.concatenate([np.kron(eye, cg), np.kron(eye, sg)], axis=1)
    flip = np.zeros((TILE, TILE))
    flip[np.arange(1, TILE), TILE - np.arange(1, TILE)] = 1.0
    alt = np.tile(np.where(np.arange(n_lat) % 2 == 0, 1.0, -1.0)[None, :], (8, 1))
    return [jnp.asarray(a, F32).astype(BF16) for a in (cn[:n_lat // 2], sn[:n_lat // 2], dc, chan, flip, alt)]


def _inproj_weights(w_in, g_q_gqa, g_k_gqa, g_cq, g_ckv, w_uq, w_ukv, g_q_nope, g_k_nope, g_q_rope, g_k_rope):
    depth, d, _ = w_in.shape
    wb = lax.optimization_barrier(w_in.astype(BF16))
    q, k, v, cq, ckv, kr, f, gates = (wb[..., 0:512], wb[..., 512:640], wb[..., 640:768], wb[..., 768:1152],
                                      wb[..., 1152:1408], wb[..., 1408:1440], wb[..., 1440:1952], wb[..., 1952:])
    zeros = lambda *s: jnp.zeros((depth,) + s, F32)
    zb = lambda n: jnp.zeros((depth, d, n), BF16)
    wa = jnp.concatenate([q, k, v, cq, zb(64), kr, zb(32), ckv, f], axis=-1)
    uq = w_uq.reshape(depth, MLA_Q_RANK, MLA_HEADS, MLA_NOPE_DIM + MLA_ROPE_DIM)
    wuq = jnp.concatenate([uq, zeros(MLA_Q_RANK, MLA_HEADS, 32)], axis=-1).reshape(depth, MLA_Q_RANK, -1)
    ukv = w_ukv.reshape(depth, MLA_KV_RANK, MLA_HEADS, MLA_NOPE_DIM + MLA_V_DIM)
    kn, vv = ukv[..., :MLA_NOPE_DIM], ukv[..., MLA_NOPE_DIM:]
    z64 = zeros(MLA_KV_RANK, MLA_HEADS, 64)
    wkn = jnp.concatenate([kn, z64], axis=-1).reshape(depth, MLA_KV_RANK, -1)
    even = (jnp.arange(MLA_HEADS) % 2 == 0)[None, None, :, None]
    wv = jnp.where(even, jnp.concatenate([vv, z64], axis=-1), jnp.concatenate([z64, vv], axis=-1))
    wvt = jnp.swapaxes(wv.reshape(depth, MLA_KV_RANK, -1), 1, 2)
    gkn = jnp.concatenate([g_k_nope, zeros(64)], axis=-1)[:, None, :]
    z32, z64v = zeros(32), zeros(64)
    tgains = _table_gains([jnp.tile(g_q_gqa, (1, 2)), jnp.tile(g_k_gqa, (1, 2)),
                           jnp.concatenate([g_q_nope, g_q_rope, z32], axis=-1),
                           jnp.concatenate([z64v, g_k_rope, z32], axis=-1)],
                          [GQA_SCALE * LOG2E, 1.0, MLA_SCALE * LOG2E, 1.0])
    wts = [wa, wuq.astype(BF16), wkn.astype(BF16), wvt.astype(BF16), g_cq[:, None, :], g_ckv[:, None, :], gkn,
           tgains]
    return wts, gates


def kernel(x, c, ctx, c_ctx, w_mod, b_mod, g_norm1, g_norm2, w_in, g_q_gqa, g_k_gqa, g_cq, g_ckv, w_uq, w_ukv,
           g_q_nope, g_k_nope, g_q_rope, g_k_rope, b_gate, w_br_a, w_br_b, w_br_c, w_out, w_ffn_in, w_ffn_out):
    b, n, d = x.shape
    nc = ctx.shape[1]
    depth = w_mod.shape[0]
    assert n % TILE == 0 and nc == TILE and b + 1 <= MOD_ROWS and n % GRID_W == 0
    n_lat_tiles, n_tiles = n // TILE, (n + nc) // TILE

    cc = jnp.concatenate([c, c_ctx[None], jnp.zeros((MOD_ROWS - b - 1, d), F32)], axis=0)
    mod = _modulation(cc, w_mod, b_mod).reshape(depth, MOD_ROWS, 6, d)
    dft = _dft_tables(n, nc)
    segs = _seg_matrices()
    bounds = _score_bounds(g_q_gqa, g_k_gqa, g_q_nope, g_k_nope, g_q_rope, g_k_rope)
    tabs = _rope_tables(n, nc)
    wts, wg = _inproj_weights(w_in, g_q_gqa, g_k_gqa, g_cq, g_ckv, w_uq, w_ukv, g_q_nope, g_k_nope, g_q_rope,
                              g_k_rope)
    g1, g2 = g_norm1[:, None, :], g_norm2[:, None, :]
    mw = [wg, b_gate[:, None, :], w_br_a.astype(BF16), w_br_b.astype(BF16), w_br_c.astype(BF16), w_out.astype(BF16)]
    wi, wo = w_ffn_in.astype(BF16), w_ffn_out.astype(BF16)
    gqa_q = (0, 0, 128, 128, 256, 256, 384, 384)
    gqa_k = (0, 128, 0, 128, 256, 384, 256, 384)
    gqa_v = (0, 1, 0, 1, 2, 3, 2, 3)
    mla_o = tuple(range(0, 1024, 128))
    mla_h = tuple(range(MLA_HEADS))

    x_lat, x_ctx, ctx_tile = x, ctx, 0
    for l in range(depth):
        n_out = n_lat_tiles if l == depth - 1 else n_tiles
        qa, ka, vat, qb, kb, vbt, pf = _inproj(x_lat, x_ctx, ctx_tile, mod, l, n_lat_tiles, g1, wts, tabs, segs)
        ya = _attention(bounds[l, 0:1], qa, ka, vat, n_lat_tiles, n_out, gqa_q, gqa_k, gqa_v, "attn_gqa")
        yb = _attention(bounds[l, 1:2], qb, kb, vbt, n_lat_tiles, n_out, mla_o, mla_o, mla_h, "attn_mla")
        yc = _fourier(pf, dft, n_lat_tiles, n_out)
        xm = _merge(x_lat, x_ctx, ctx_tile, mod, l, n_lat_tiles, n_out, g1, ya, yb, yc, mw)
        xall = _ffn(xm, mod, l, n_lat_tiles, n_out, g2, wi, wo)
        x_lat, x_ctx, ctx_tile = xall, xall, n_lat_tiles
    return xall
```

```python
import functools
import math

import numpy as np
import jax
import jax.numpy as jnp
from jax import lax
from jax.experimental import pallas as pl
from jax.experimental.pallas import tpu as pltpu

GRID_W = 64
ROPE_THETA = 10000.0
EPS = 1e-6
GQA_HEADS = 8
GQA_KV_HEADS = 2
GQA_HEAD_DIM = 64
GQA_SCALE = GQA_HEAD_DIM ** -0.5
MLA_HEADS = 8
MLA_Q_RANK = 384
MLA_KV_RANK = 256
MLA_NOPE_DIM = 64
MLA_ROPE_DIM = 32
MLA_V_DIM = 64
MLA_SCALE = (MLA_NOPE_DIM + MLA_ROPE_DIM) ** -0.5
FOURIER_GROUPS = 4
FOURIER_GROUP_DIM = 128
FOURIER_WIDTH = FOURIER_GROUPS * FOURIER_GROUP_DIM
N_BRANCH = 3
LOG2E = math.log2(math.e)

LANES = 128
TILE = 256
MOD_ROWS = 16
VMEM_LIMIT = 56 * 1024 * 1024

BF16 = jnp.bfloat16
F32 = jnp.float32

_ONES_ROW_EVEN, _ONES_ROW_ODD = LANES - 1, 0

_Q0, _K0, _V0, _CQ0, _KR0, _CKV0, _F0, _WA = 0, 512, 640, 768, 1152, 1280, 1536, 2048


def _cparams(n_axes):
    return pltpu.CompilerParams(dimension_semantics=("arbitrary",) * n_axes, vmem_limit_bytes=VMEM_LIMIT)


def _dot(a, b):
    return jnp.dot(a, b, preferred_element_type=F32)


def _rms_rows(xv, g):
    return xv * lax.rsqrt(jnp.mean(xv * xv, axis=-1, keepdims=True) + EPS) * g


def _modulated(x, g, shift, scale):
    r = lax.rsqrt(jnp.mean(x * x, axis=-1, keepdims=True) + EPS)
    return (x * r * (g * (1.0 + scale)) + shift).astype(BF16)


def _layer_spec(a, l):
    return pl.BlockSpec((1,) + a.shape[1:], lambda *_: (l,) + (0,) * (a.ndim - 1))


def _const_spec(a):
    return pl.BlockSpec(a.shape, lambda *_: (0,) * a.ndim)


def _sub_tiles(n_tiles):
    return next(s for s in (3, 2, 1) if n_tiles % s == 0)


def _row_specs(x_lat, x_ctx, ctx_tile, n_lat_tiles, sub):
    d = x_lat.shape[-1]
    lat = lambda j: pl.BlockSpec((1, TILE, d), lambda bi, si: (bi, jnp.minimum(si * sub + j, n_lat_tiles - 1), 0))
    return [lat(j) for j in range(sub)] + [pl.BlockSpec((1, TILE, d), lambda bi, si: (bi, ctx_tile, 0))]


def _mod_specs(mod, l, n_batch, n_lat_tiles, sub):
    spec = lambda j: pl.BlockSpec((1, 1) + mod.shape[2:],
                                  lambda bi, si: (l, jnp.where(si * sub + j >= n_lat_tiles, n_batch, bi), 0, 0))
    return [spec(j) for j in range(sub)]


def _tile_rows(xl_refs, xc_ref, j, n_lat_tiles):
    return jnp.where(pl.program_id(1) * len(xl_refs) + j < n_lat_tiles, xl_refs[j][0], xc_ref[0])


def _mod_kernel(cc_ref, w_ref, b_ref, o_ref):
    s = cc_ref[...]
    s = s * jax.nn.sigmoid(s)
    o_ref[0] = _dot(s.astype(BF16), w_ref[0].astype(BF16)) + b_ref[0]


def _modulation(cc, w_mod, b_mod):
    depth, d, d6 = w_mod.shape
    bn = 1536
    return pl.pallas_call(
        _mod_kernel,
        grid=(depth, d6 // bn),
        in_specs=[pl.BlockSpec((MOD_ROWS, d), lambda l, j: (0, 0)),
                  pl.BlockSpec((1, d, bn), lambda l, j: (l, 0, j)),
                  pl.BlockSpec((1, 1, bn), lambda l, j: (l, 0, j))],
        out_specs=pl.BlockSpec((1, MOD_ROWS, bn), lambda l, j: (l, 0, j)),
        out_shape=jax.ShapeDtypeStruct((depth, MOD_ROWS, d6), F32),
        compiler_params=_cparams(2),
        name="modulation",
    )(cc, w_mod, b_mod.reshape(depth, 1, d6))


def _rot_half(xv, lane, seg):
    first = (lane & (2 * seg - 1)) < seg
    return jnp.where(first, pltpu.roll(xv, LANES - seg, 1), pltpu.roll(xv, seg, 1))


def _seg_rsqrt(xw, seg_ref):
    return lax.rsqrt(_dot((xw * xw).astype(BF16), seg_ref[...]) + EPS)


def _inproj_kernel(*refs, n_lat_tiles, sub):
    xl_refs, refs = refs[:sub], refs[sub:]
    xc_ref, refs = refs[0], refs[1:]
    mod_refs, refs = refs[:sub], refs[sub:]
    (g1_ref, wa_ref, wuq_ref, wkn_ref, wvt_ref, gcq_ref, gckv_ref, gkn_ref, tg_ref, tab_ref,
     sgqa_ref, sqb_ref, skn_ref, skk_ref, qa_ref, ka_ref, vat_ref, qb_ref, kb_ref, vbt_ref, pf_ref) = refs
    lane = lax.broadcasted_iota(jnp.int32, (1, LANES), 1)
    lo = lane < 64
    hi = lane >= 64
    row = lax.broadcasted_iota(jnp.int32, (LANES, TILE), 0)
    blk = lambda a, j: a[:, j * LANES:(j + 1) * LANES]
    n_qblk = GQA_HEADS * GQA_HEAD_DIM // LANES
    w2 = 2 * LANES

    def project(j):
        rows = slice(j * TILE, (j + 1) * TILE)
        m = mod_refs[j][0, 0]
        hb = _modulated(_tile_rows(xl_refs, xc_ref, j, n_lat_tiles), g1_ref[0], m[0:1], m[1:2])
        p_c = _dot(hb, wa_ref[0, :, _CQ0:_CKV0])
        p_ckv = _dot(hb, wa_ref[0, :, _CKV0:_F0])
        pq = _dot(hb, wa_ref[0, :, _Q0:_K0])
        pkv = _dot(hb, wa_ref[0, :, _K0:_CQ0])
        pf_ref[0, rows] = _dot(hb, wa_ref[0, :, _F0:_WA]).astype(BF16)
        cq = _rms_rows(p_c[:, :MLA_Q_RANK], gcq_ref[0]).astype(BF16)
        ckv = _rms_rows(p_ckv, gckv_ref[0])
        qb = _dot(cq, wuq_ref[0])
        kn_all = _dot(ckv.astype(BF16), wkn_ref[0])
        vbt = _dot(wvt_ref[0], ckv.T.astype(BF16))
        for h in range(MLA_HEADS):
            vh = vbt[h * LANES:(h + 1) * LANES]
            ones_row = _ONES_ROW_EVEN if h % 2 == 0 else _ONES_ROW_ODD
            vbt_ref[0, h, :, rows] = jnp.where(row == ones_row, 1.0, vh).astype(BF16)
        vt = pkv[:, LANES:].T
        for kvh in range(GQA_KV_HEADS):
            own = (row >= 64) == (kvh == 1)
            top = jnp.where(own, vt, 0.0) if kvh == 0 else pltpu.roll(jnp.where(own, vt, 0.0), 64, 0)
            bot = pltpu.roll(top, 64, 0)
            vat_ref[0, 2 * kvh, :, rows] = jnp.where(row == _ONES_ROW_EVEN, 1.0, top).astype(BF16)
            vat_ref[0, 2 * kvh + 1, :, rows] = jnp.where(row == _ONES_ROW_ODD, 1.0, bot).astype(BF16)
        return pq, pkv[:, :LANES], qb, kn_all, p_c[:, MLA_Q_RANK:]

    def finish(j, pq, xk, qb, kn_all, xkr):
        rows = slice(j * TILE, (j + 1) * TILE)
        r_qb = jnp.concatenate([_seg_rsqrt(qb[:, c * w2:(c + 1) * w2], sqb_ref) for c in range(MLA_HEADS // 2)],
                               axis=1)
        r_kn = jnp.concatenate([_seg_rsqrt(kn_all[:, c * w2:(c + 1) * w2], skn_ref) for c in range(MLA_HEADS // 2)],
                               axis=1)
        r_q = jnp.concatenate([_seg_rsqrt(pq[:, c * w2:(c + 1) * w2], sgqa_ref) for c in range(n_qblk // 2)], axis=1)
        r_kk = _seg_rsqrt(jnp.concatenate([xk, xkr], axis=1), skk_ref)
        r_k, r_kr = r_kk[:, :LANES], r_kk[:, LANES:]

        tab = [tab_ref[i, rows] * tg_ref[0, i:i + 1] for i in range(2 * N_TABLES)]

        def roped(xb, table, seg, r):
            return (xb * tab[table] + _rot_half(xb, lane, seg) * tab[N_TABLES + table]) * r

        for h in range(MLA_HEADS):
            qb_ref[0, rows, h * LANES:(h + 1) * LANES] = roped(blk(qb, h), TAB_QB, 8, blk(r_qb, h)).astype(BF16)
        kr = roped(xkr, TAB_KR, 8, r_kr)
        for h in range(MLA_HEADS):
            kb_ref[0, rows, h * LANES:(h + 1) * LANES] = (blk(kn_all, h) * blk(r_kn, h) * gkn_ref[0] + kr).astype(BF16)
        for c in range(n_qblk):
            qa_ref[0, rows, c * LANES:(c + 1) * LANES] = roped(blk(pq, c), TAB_Q, 16, blk(r_q, c)).astype(BF16)
        kn = roped(xk, TAB_K, 16, r_k)
        sw = pltpu.roll(kn, 64, 1)
        for c, f in enumerate((jnp.where(lo, kn, 0.0), jnp.where(hi, sw, 0.0), jnp.where(lo, sw, 0.0),
                               jnp.where(hi, kn, 0.0))):
            ka_ref[0, rows, c * LANES:(c + 1) * LANES] = f.astype(BF16)

    pending = project(0)
    for j in range(sub):
        nxt = project(j + 1) if j + 1 < sub else None
        finish(j, *pending)
        pending = nxt


def _inproj(x_lat, x_ctx, ctx_tile, mod, l, n_lat_tiles, g1, wts, tabs, segs):
    b = x_lat.shape[0]
    nt = n_lat_tiles + 1
    sub = _sub_tiles(nt)
    t, step_rows = nt * TILE, sub * TILE
    row = lambda bi, si: (bi, si, 0)
    rowt = lambda bi, si: (bi, 0, 0, si)
    tok = lambda w: (pl.BlockSpec((1, step_rows, w), row), jax.ShapeDtypeStruct((b, t, w), BF16))
    tra = lambda h: (pl.BlockSpec((1, h, LANES, step_rows), rowt), jax.ShapeDtypeStruct((b, h, LANES, t), BF16))
    outs = [tok(512), tok(512), tra(4), tok(1024), tok(1024), tra(MLA_HEADS), tok(512)]
    return pl.pallas_call(
        functools.partial(_inproj_kernel, n_lat_tiles=n_lat_tiles, sub=sub),
        grid=(b, nt // sub),
        in_specs=_row_specs(x_lat, x_ctx, ctx_tile, n_lat_tiles, sub) + _mod_specs(mod, l, b, n_lat_tiles, sub)
        + [_layer_spec(g1, l)] + [_layer_spec(w, l) for w in wts]
        + [pl.BlockSpec((2 * N_TABLES, step_rows, LANES), lambda bi, si: (0, si, 0))]
        + [_const_spec(s) for s in segs],
        out_specs=[o[0] for o in outs],
        out_shape=[o[1] for o in outs],
        compiler_params=_cparams(2),
        name="inproj",
    )(*([x_lat] * sub), x_ctx, *([mod] * sub), g1, *wts, tabs, *segs)


def _reduce_keys(a, op):
    nk, nq = a.shape
    part = op(a.reshape(nk // TILE, TILE, nq), axis=0)
    return op(part, axis=0, keepdims=True)


SCORE_SAFE = 60.0
BOUND_SLACK = 1.0 + 2.0 ** -6


def _attn_kernel(bound_ref, q_ref, k_ref, vt_ref, o_ref, *, n_lat_tiles, with_ctx, qoff, koff, vidx):
    n_keys = k_ref.shape[1]
    n_pairs = len(qoff) // 2
    zero = jnp.zeros((TILE, LANES), BF16)
    bound = bound_ref[0]

    def scores(row0, pair, key0, nk):
        e, o = 2 * pair, 2 * pair + 1
        assert koff[o] == koff[e] + LANES
        q_e = q_ref[0, pl.ds(row0, TILE), qoff[e]:qoff[e] + LANES]
        q_o = q_ref[0, pl.ds(row0, TILE), qoff[o]:qoff[o] + LANES]
        qd = jnp.concatenate([jnp.concatenate([q_e, zero], axis=1), jnp.concatenate([zero, q_o], axis=1)], axis=0)
        k2 = k_ref[0, key0:key0 + nk, koff[e]:koff[e] + 2 * LANES]
        return lax.dot_general(k2, qd, (((1,), (1,)), ((), ())), preferred_element_type=F32)

    def tile(row0, key0, nk, bounded):
        for pair in range(n_pairs):
            e, o = 2 * pair, 2 * pair + 1
            st = scores(row0, pair, key0, nk)
            pb = jnp.exp2(st if bounded else st - _reduce_keys(st, jnp.max)).astype(BF16)
            ot_e = _dot(vt_ref[0, vidx[e], :, key0:key0 + nk], pb[:, :TILE])
            ot_o = _dot(vt_ref[0, vidx[o], :, key0:key0 + nk], pb[:, TILE:])
            ot_e = ot_e * (1.0 / ot_e[_ONES_ROW_EVEN:_ONES_ROW_EVEN + 1])
            ot_o = ot_o * (1.0 / ot_o[_ONES_ROW_ODD:_ONES_ROW_ODD + 1])
            ot = jnp.where(lax.broadcasted_iota(jnp.int32, (LANES, TILE), 0) < 64, ot_e, ot_o)
            o_ref[0, pl.ds(row0, TILE), pair * LANES:(pair + 1) * LANES] = ot.T.astype(BF16)

    def all_tiles(bounded):
        def body(t, carry):
            tile(pl.multiple_of(t * TILE, TILE), 0, n_keys, bounded)
            return carry
        lax.fori_loop(0, n_lat_tiles, body, 0, unroll=2 if bounded else 1)
        if with_ctx:
            lat = n_lat_tiles * TILE
            tile(lat, lat, n_keys - lat, bounded)

    lax.cond(bound < SCORE_SAFE, lambda: all_tiles(True), lambda: all_tiles(False))


def _attention(shift, q, k, vt, n_lat_tiles, n_q_tiles, qoff, koff, vidx, name):
    b, t, wq = q.shape
    wk = k.shape[2]
    nh = vt.shape[1]
    kern = functools.partial(_attn_kernel, n_lat_tiles=n_lat_tiles, with_ctx=n_q_tiles > n_lat_tiles,
                             qoff=qoff, koff=koff, vidx=vidx)
    return pl.pallas_call(
        kern,
        grid=(b,),
        in_specs=[pl.BlockSpec(memory_space=pltpu.SMEM),
                  pl.BlockSpec((1, t, wq), lambda bi: (bi, 0, 0)),
                  pl.BlockSpec((1, t, wk), lambda bi: (bi, 0, 0)),
                  pl.BlockSpec((1, nh, LANES, t), lambda bi: (bi, 0, 0, 0))],
        out_specs=pl.BlockSpec((1, n_q_tiles * TILE, 512), lambda bi: (bi, 0, 0)),
        out_shape=jax.ShapeDtypeStruct((b, n_q_tiles * TILE, 512), BF16),
        compiler_params=_cparams(1),
        name=name,
    )(shift, q, k, vt)


def _fourier_kernel(pf_ref, cn_ref, sn_ref, dc_ref, cs_ref, flip_ref, alt_ref, o_ref, a_ref, b_ref, *, n_lat_tiles,
                    with_ctx):
    n_lat = n_lat_tiles * TILE
    half_tiles = n_lat_tiles // 2
    w = FOURIER_WIDTH
    for c in range(n_lat_tiles):
        ab = _dot(pf_ref[0, c * TILE:(c + 1) * TILE, :], cs_ref[...])
        a_ref[c * TILE:(c + 1) * TILE, :] = ab[:, :w].astype(BF16)
        b_ref[c * TILE:(c + 1) * TILE, :] = ab[:, w:].astype(BF16)
    row0 = lax.broadcasted_iota(jnp.int32, (TILE, w), 0) == 0
    carry = _dot(alt_ref[...], a_ref[...])[0:1] * (1.0 / math.sqrt(n_lat))
    for m in reversed(range(half_tiles)):
        g = _dot(cn_ref[m * TILE:(m + 1) * TILE, :], a_ref[...])
        h = _dot(sn_ref[m * TILE:(m + 1) * TILE, :], b_ref[...])
        o_ref[0, m * TILE:(m + 1) * TILE, :] = (g - h).astype(BF16)
        mirrored = (g + h).astype(BF16)
        tile_rev = _dot(flip_ref[...], mirrored)
        bot = n_lat_tiles - 1 - m
        o_ref[0, bot * TILE:(bot + 1) * TILE, :] = jnp.where(row0, carry, tile_rev).astype(BF16)
        carry = mirrored[0:1].astype(F32)
    if with_ctx:
        ab = _dot(pf_ref[0, n_lat:n_lat + TILE, :], cs_ref[...])
        abc = jnp.concatenate([ab[:, :w], ab[:, w:]], axis=0).astype(BF16)
        o_ref[0, n_lat:n_lat + TILE, :] = _dot(dc_ref[...], abc).astype(BF16)


def _fourier(pf, tables, n_lat_tiles, n_tiles):
    b, t, w = pf.shape
    n_lat = n_lat_tiles * TILE
    kern = functools.partial(_fourier_kernel, n_lat_tiles=n_lat_tiles, with_ctx=n_tiles > n_lat_tiles)
    return pl.pallas_call(
        kern,
        grid=(b,),
        in_specs=[pl.BlockSpec((1, t, w), lambda bi: (bi, 0, 0))] + [_const_spec(a) for a in tables],
        out_specs=pl.BlockSpec((1, n_tiles * TILE, w), lambda bi: (bi, 0, 0)),
        out_shape=jax.ShapeDtypeStruct((b, n_tiles * TILE, w), BF16),
        scratch_shapes=[pltpu.VMEM((n_lat, w), BF16), pltpu.VMEM((n_lat, w), BF16)],
        compiler_params=_cparams(1),
        name="fourier",
    )(pf, *tables)


def _merge_kernel(*refs, n_lat_tiles, sub):
    xl_refs, refs = refs[:sub], refs[sub:]
    xc_ref, refs = refs[0], refs[1:]
    mod_refs, refs = refs[:sub], refs[sub:]
    g1_ref, ya_ref, yb_ref, yc_ref, wg_ref, bg_ref, wa_ref, wb_ref, wc_ref, wo_ref, o_ref = refs
    for j in range(sub):
        rows = slice(j * TILE, (j + 1) * TILE)
        x = _tile_rows(xl_refs, xc_ref, j, n_lat_tiles)
        m = mod_refs[j][0, 0]
        d = x.shape[-1]
        hb = _modulated(x, g1_ref[0], m[0:1], m[1:2])
        acc = None
        for i, (y_ref, w_ref) in enumerate(((ya_ref, wa_ref), (yb_ref, wb_ref), (yc_ref, wc_ref))):
            gate = jax.nn.sigmoid(_dot(hb, wg_ref[0, :, i * d:(i + 1) * d]) + bg_ref[0, :, i * d:(i + 1) * d])
            term = gate * _dot(y_ref[0, rows], w_ref[0])
            acc = term if acc is None else acc + term
        o_ref[0, rows] = x + m[2:3] * _dot(acc.astype(BF16), wo_ref[0])


def _merge(x_lat, x_ctx, ctx_tile, mod, l, n_lat_tiles, n_tiles, g1, ya, yb, yc, wts):
    b, _, d = x_lat.shape
    sub = _sub_tiles(n_tiles)
    row = lambda bi, si: (bi, si, 0)
    return pl.pallas_call(
        functools.partial(_merge_kernel, n_lat_tiles=n_lat_tiles, sub=sub),
        grid=(b, n_tiles // sub),
        in_specs=_row_specs(x_lat, x_ctx, ctx_tile, n_lat_tiles, sub) + _mod_specs(mod, l, b, n_lat_tiles, sub)
        + [_layer_spec(g1, l)] + [pl.BlockSpec((1, sub * TILE, 512), row)] * 3 + [_layer_spec(w, l) for w in wts],
        out_specs=pl.BlockSpec((1, sub * TILE, d), row),
        out_shape=jax.ShapeDtypeStruct((b, n_tiles * TILE, d), F32),
        compiler_params=_cparams(2),
        name="merge",
    )(*([x_lat] * sub), x_ctx, *([mod] * sub), g1, ya, yb, yc, *wts)


def _ffn_kernel(*refs, sub):
    x_ref, refs = refs[0], refs[1:]
    mod_refs, refs = refs[:sub], refs[sub:]
    g2_ref, wi_ref, wo_ref, o_ref = refs
    for j in range(sub):
        rows = slice(j * TILE, (j + 1) * TILE)
        x = x_ref[0, rows]
        m = mod_refs[j][0, 0]
        hb = _modulated(x, g2_ref[0], m[3:4], m[4:5])
        gu = _dot(hb, wi_ref[0])
        hid = gu.shape[-1] // 2
        gate, up = gu[:, :hid], gu[:, hid:]
        act = (gate * jax.nn.sigmoid(gate) * up).astype(BF16)
        o_ref[0, rows] = x + m[5:6] * _dot(act, wo_ref[0])


def _ffn(xm, mod, l, n_lat_tiles, n_tiles, g2, wi, wo):
    b, _, d = xm.shape
    sub = _sub_tiles(n_tiles)
    row = lambda bi, si: (bi, si, 0)
    once = lambda a: pl.BlockSpec((1,) + a.shape[1:], lambda *_: (l,) + (0,) * (a.ndim - 1),
                                  pipeline_mode=pl.Buffered(1))
    return pl.pallas_call(
        functools.partial(_ffn_kernel, sub=sub),
        grid=(b, n_tiles // sub),
        in_specs=[pl.BlockSpec((1, sub * TILE, d), row)] + _mod_specs(mod, l, b, n_lat_tiles, sub)
        + [_layer_spec(g2, l), once(wi), once(wo)],
        out_specs=pl.BlockSpec((1, sub * TILE, d), row),
        out_shape=jax.ShapeDtypeStruct((b, n_tiles * TILE, d), F32),
        compiler_params=_cparams(2),
        name="ffn",
    )(xm, *([mod] * sub), g2, wi, wo)


def _partner(seg):
    lane = np.arange(LANES)
    return np.where((lane % (2 * seg)) < seg, lane + seg, lane - seg)


def _rope_tables(n_lat, n_ctx):
    rows = n_lat // GRID_W
    row_id = np.repeat(np.arange(rows), GRID_W).astype(np.float64)
    col_id = np.tile(np.arange(GRID_W), rows).astype(np.float64)

    def angles(dim):
        half = dim // 2
        freqs = ROPE_THETA ** (-np.arange(0, half, 2, dtype=np.float64) / half)
        ax = lambda pos: np.concatenate([pos[:, None] * freqs[None, :]] * 2, axis=-1)
        return np.concatenate([ax(row_id), ax(col_id)], axis=-1)

    def signed(sin, dim):
        sign = np.where((np.arange(dim) % (dim // 2)) < dim // 4, -1.0, 1.0)
        return sin * sign[None, :]

    a64, a32 = angles(GQA_HEAD_DIM), angles(MLA_ROPE_DIM)
    t = n_lat + n_ctx
    cosa, sina = np.ones((t, LANES)), np.zeros((t, LANES))
    cosa[:n_lat] = np.tile(np.cos(a64), (1, 2))
    sina[:n_lat] = np.tile(signed(np.sin(a64), GQA_HEAD_DIM), (1, 2))
    cosb, sinb = np.ones((t, LANES)), np.zeros((t, LANES))
    cosb[:n_lat, 64:96] = np.cos(a32)
    sinb[:n_lat, 64:96] = signed(np.sin(a32), MLA_ROPE_DIM)
    return jnp.asarray(np.stack([cosa, cosa, cosb, cosb, sina, sina, sinb, sinb]), F32)


TAB_Q, TAB_K, TAB_QB, TAB_KR, N_TABLES = 0, 1, 2, 3, 4


def _table_gains(gains, scales):
    g = jnp.stack(gains, axis=1) * jnp.asarray(scales, F32)[None, :, None]
    partner = np.stack([_partner(16), _partner(16), _partner(8), _partner(8)])
    return jnp.concatenate([g, jnp.take_along_axis(g, jnp.asarray(partner)[None], axis=-1)], axis=1)


def _seg_matrices():
    def blockdiag(segs):
        m = np.zeros((2 * LANES, 2 * LANES))
        for start, n in segs:
            m[start:start + n, start:start + n] = 1.0 / n
        return m

    gqa = blockdiag([(s0, 64) for s0 in range(0, 256, 64)])
    qb = blockdiag([(0, 64), (64, 32), (128, 64), (192, 32)])
    kn = blockdiag([(0, 64), (128, 64)])
    kk = blockdiag([(0, 64), (64, 64), (192, 32)])
    return [jnp.asarray(a, F32).astype(BF16) for a in (gqa, qb, kn, kk)]


def _score_bounds(g_q_gqa, g_k_gqa, g_q_nope, g_k_nope, g_q_rope, g_k_rope):
    top = lambda g: jnp.max(g * g, axis=-1)
    gqa = jnp.sqrt(GQA_HEAD_DIM * top(g_q_gqa) * GQA_HEAD_DIM * top(g_k_gqa)) * (GQA_SCALE * LOG2E)
    mla = jnp.sqrt((MLA_NOPE_DIM * top(g_q_nope) + MLA_ROPE_DIM * top(g_q_rope))
                   * (MLA_NOPE_DIM * top(g_k_nope) + MLA_ROPE_DIM * top(g_k_rope))) * (MLA_SCALE * LOG2E)
    return jnp.stack([gqa, mla], axis=-1) * BOUND_SLACK


def _dft_tables(n_lat, n_ctx):
    def cs(n):
        j = np.arange(n)
        ang = 2.0 * np.pi * ((j[:, None] * j[None, :]) % n) / n
        return np.cos(ang) / np.sqrt(n), np.sin(ang) / np.sqrt(n)

    cn, sn = cs(n_lat)
    cc, sc = cs(n_ctx)
    cg, sg = cs(FOURIER_GROUP_DIM)
    eye = np.eye(FOURIER_GROUPS)
    dc = np.concatenate([cc, -sc], axis=1)
    chan = np.concatenate([np.kron(eye, cg), np.kron(eye, sg)], axis=1)
    flip = np.zeros((TILE, TILE))
    flip[np.arange(1, TILE), TILE - np.arange(1, TILE)] = 1.0
    alt = np.tile(np.where(np.arange(n_lat) % 2 == 0, 1.0, -1.0)[None, :], (8, 1))
    return [jnp.asarray(a, F32).astype(BF16) for a in (cn[:n_lat // 2], sn[:n_lat // 2], dc, chan, flip, alt)]


_W_IN_COLS = 5024
_GATE0 = 1952


def _split_kernel(w_ref, wa_ref, wg_ref):
    rows = w_ref.shape[1]
    col = lambda a, b: w_ref[0, :, a:b]
    zero = lambda n: jnp.zeros((rows, n), F32)
    wa = jnp.concatenate([col(0, 1152), zero(64), col(1408, 1440), zero(32), col(1152, 1408), col(1440, _GATE0)],
                         axis=1)
    wa_ref[0] = wa.astype(BF16)
    wg_ref[0] = col(_GATE0, _W_IN_COLS).astype(BF16)


def _split_w_in(w_in):
    depth, d, cols = w_in.shape
    assert cols == _W_IN_COLS
    rows = 256
    return pl.pallas_call(
        _split_kernel,
        grid=(depth, d // rows),
        in_specs=[pl.BlockSpec((1, rows, cols), lambda l, r: (l, r, 0))],
        out_specs=[pl.BlockSpec((1, rows, _WA), lambda l, r: (l, r, 0)),
                   pl.BlockSpec((1, rows, cols - _GATE0), lambda l, r: (l, r, 0))],
        out_shape=[jax.ShapeDtypeStruct((depth, d, _WA), BF16), jax.ShapeDtypeStruct((depth, d, cols - _GATE0), BF16)],
        compiler_params=_cparams(2),
        name="split_w_in",
    )(w_in)


def _inproj_weights(w_in, g_q_gqa, g_k_gqa, g_cq, g_ckv, w_uq, w_ukv, g_q_nope, g_k_nope, g_q_rope, g_k_rope):
    depth = w_in.shape[0]
    wa, gates = _split_w_in(w_in)
    zeros = lambda *s: jnp.zeros((depth,) + s, F32)
    uq = w_uq.reshape(depth, MLA_Q_RANK, MLA_HEADS, MLA_NOPE_DIM + MLA_ROPE_DIM)
    wuq = jnp.concatenate([uq, zeros(MLA_Q_RANK, MLA_HEADS, 32)], axis=-1).reshape(depth, MLA_Q_RANK, -1)
    ukv = w_ukv.reshape(depth, MLA_KV_RANK, MLA_HEADS, MLA_NOPE_DIM + MLA_V_DIM)
    kn, vv = ukv[..., :MLA_NOPE_DIM], ukv[..., MLA_NOPE_DIM:]
    z64 = zeros(MLA_KV_RANK, MLA_HEADS, 64)
    wkn = jnp.concatenate([kn, z64], axis=-1).reshape(depth, MLA_KV_RANK, -1)
    even = (jnp.arange(MLA_HEADS) % 2 == 0)[None, None, :, None]
    wv = jnp.where(even, jnp.concatenate([vv, z64], axis=-1), jnp.concatenate([z64, vv], axis=-1))
    wvt = jnp.swapaxes(wv.reshape(depth, MLA_KV_RANK, -1), 1, 2)
    gkn = jnp.concatenate([g_k_nope, zeros(64)], axis=-1)[:, None, :]
    z32, z64v = zeros(32), zeros(64)
    tgains = _table_gains([jnp.tile(g_q_gqa, (1, 2)), jnp.tile(g_k_gqa, (1, 2)),
                           jnp.concatenate([g_q_nope, g_q_rope, z32], axis=-1),
                           jnp.concatenate([z64v, g_k_rope, z32], axis=-1)],
                          [GQA_SCALE * LOG2E, 1.0, MLA_SCALE * LOG2E, 1.0])
    wts = [wa, wuq.astype(BF16), wkn.astype(BF16), wvt.astype(BF16), g_cq[:, None, :], g_ckv[:, None, :], gkn,
           tgains]
    return wts, gates


def kernel(x, c, ctx, c_ctx, w_mod, b_mod, g_norm1, g_norm2, w_in, g_q_gqa, g_k_gqa, g_cq, g_ckv, w_uq, w_ukv,
           g_q_nope, g_k_nope, g_q_rope, g_k_rope, b_gate, w_br_a, w_br_b, w_br_c, w_out, w_ffn_in, w_ffn_out):
    b, n, d = x.shape
    nc = ctx.shape[1]
    depth = w_mod.shape[0]
    assert n % TILE == 0 and nc == TILE and b + 1 <= MOD_ROWS and n % GRID_W == 0
    n_lat_tiles, n_tiles = n // TILE, (n + nc) // TILE

    cc = jnp.concatenate([c, c_ctx[None], jnp.zeros((MOD_ROWS - b - 1, d), F32)], axis=0)
    mod = _modulation(cc, w_mod, b_mod).reshape(depth, MOD_ROWS, 6, d)
    dft = _dft_tables(n, nc)
    segs = _seg_matrices()
    bounds = _score_bounds(g_q_gqa, g_k_gqa, g_q_nope, g_k_nope, g_q_rope, g_k_rope)
    tabs = _rope_tables(n, nc)
    wts, wg = _inproj_weights(w_in, g_q_gqa, g_k_gqa, g_cq, g_ckv, w_uq, w_ukv, g_q_nope, g_k_nope, g_q_rope,
                              g_k_rope)
    g1, g2 = g_norm1[:, None, :], g_norm2[:, None, :]
    mw = [wg, b_gate[:, None, :], w_br_a.astype(BF16), w_br_b.astype(BF16), w_br_c.astype(BF16), w_out.astype(BF16)]
    wi, wo = w_ffn_in.astype(BF16), w_ffn_out.astype(BF16)
    gqa_q = (0, 0, 128, 128, 256, 256, 384, 384)
    gqa_k = (0, 128, 0, 128, 256, 384, 256, 384)
    gqa_v = (0, 1, 0, 1, 2, 3, 2, 3)
    mla_o = tuple(range(0, 1024, 128))
    mla_h = tuple(range(MLA_HEADS))

    x_lat, x_ctx, ctx_tile = x, ctx, 0
    for l in range(depth):
        n_out = n_lat_tiles if l == depth - 1 else n_tiles
        qa, ka, vat, qb, kb, vbt, pf = _inproj(x_lat, x_ctx, ctx_tile, mod, l, n_lat_tiles, g1, wts, tabs, segs)
        ya = _attention(bounds[l, 0:1], qa, ka, vat, n_lat_tiles, n_out, gqa_q, gqa_k, gqa_v, "attn_gqa")
        yb = _attention(bounds[l, 1:2], qb, kb, vbt, n_lat_tiles, n_out, mla_o, mla_o, mla_h, "attn_mla")
        yc = _fourier(pf, dft, n_lat_tiles, n_out)
        xm = _merge(x_lat, x_ctx, ctx_tile, mod, l, n_lat_tiles, n_out, g1, ya, yb, yc, mw)
        xall = _ffn(xm, mod, l, n_lat_tiles, n_out, g2, wi, wo)
        x_lat, x_ctx, ctx_tile = xall, xall, n_lat_tiles
    return xall
```

```python
import functools
import math

import numpy as np
import jax
import jax.numpy as jnp
from jax import lax
from jax.experimental import pallas as pl
from jax.experimental.pallas import tpu as pltpu

GRID_W = 64
ROPE_THETA = 10000.0
EPS = 1e-6
GQA_HEADS = 8
GQA_KV_HEADS = 2
GQA_HEAD_DIM = 64
GQA_SCALE = GQA_HEAD_DIM ** -0.5
MLA_HEADS = 8
MLA_Q_RANK = 384
MLA_KV_RANK = 256
MLA_NOPE_DIM = 64
MLA_ROPE_DIM = 32
MLA_V_DIM = 64
MLA_SCALE = (MLA_NOPE_DIM + MLA_ROPE_DIM) ** -0.5
FOURIER_GROUPS = 4
FOURIER_GROUP_DIM = 128
FOURIER_WIDTH = FOURIER_GROUPS * FOURIER_GROUP_DIM
N_BRANCH = 3
LOG2E = math.log2(math.e)

LANES = 128
TILE = 256
MOD_ROWS = 16
VMEM_LIMIT = 56 * 1024 * 1024

BF16 = jnp.bfloat16
F32 = jnp.float32

_ONES_ROW_EVEN, _ONES_ROW_ODD = LANES - 1, 0

_Q0, _K0, _V0, _CQ0, _KR0, _CKV0, _F0, _WA = 0, 512, 640, 768, 1152, 1280, 1536, 2048


def _cparams(n_axes):
    return pltpu.CompilerParams(dimension_semantics=("arbitrary",) * n_axes, vmem_limit_bytes=VMEM_LIMIT)


def _dot(a, b):
    return jnp.dot(a, b, preferred_element_type=F32)


def _rms_rows(xv, g):
    return xv * lax.rsqrt(jnp.mean(xv * xv, axis=-1, keepdims=True) + EPS) * g


def _modulated(x, g, shift, scale):
    r = lax.rsqrt(jnp.mean(x * x, axis=-1, keepdims=True) + EPS)
    return (x * r * (g * (1.0 + scale)) + shift).astype(BF16)


def _layer_spec(a, l):
    return pl.BlockSpec((1,) + a.shape[1:], lambda *_: (l,) + (0,) * (a.ndim - 1))


def _const_spec(a):
    return pl.BlockSpec(a.shape, lambda *_: (0,) * a.ndim)


def _sub_tiles(n_tiles):
    return next(s for s in (3, 2, 1) if n_tiles % s == 0)


def _row_specs(x_lat, x_ctx, ctx_tile, n_lat_tiles, sub):
    d = x_lat.shape[-1]
    lat = lambda j: pl.BlockSpec((1, TILE, d), lambda bi, si: (bi, jnp.minimum(si * sub + j, n_lat_tiles - 1), 0))
    return [lat(j) for j in range(sub)] + [pl.BlockSpec((1, TILE, d), lambda bi, si: (bi, ctx_tile, 0))]


def _mod_specs(mod, l, n_batch, n_lat_tiles, sub):
    spec = lambda j: pl.BlockSpec((1, 1) + mod.shape[2:],
                                  lambda bi, si: (l, jnp.where(si * sub + j >= n_lat_tiles, n_batch, bi), 0, 0))
    return [spec(j) for j in range(sub)]


def _tile_rows(xl_refs, xc_ref, j, n_lat_tiles):
    return jnp.where(pl.program_id(1) * len(xl_refs) + j < n_lat_tiles, xl_refs[j][0], xc_ref[0])


def _mod_kernel(cc_ref, w_ref, b_ref, o_ref):
    s = cc_ref[...]
    s = s * jax.nn.sigmoid(s)
    o_ref[0] = _dot(s.astype(BF16), w_ref[0].astype(BF16)) + b_ref[0]


def _modulation(cc, w_mod, b_mod):
    depth, d, d6 = w_mod.shape
    bn = 1536
    return pl.pallas_call(
        _mod_kernel,
        grid=(depth, d6 // bn),
        in_specs=[pl.BlockSpec((MOD_ROWS, d), lambda l, j: (0, 0)),
                  pl.BlockSpec((1, d, bn), lambda l, j: (l, 0, j)),
                  pl.BlockSpec((1, 1, bn), lambda l, j: (l, 0, j))],
        out_specs=pl.BlockSpec((1, MOD_ROWS, bn), lambda l, j: (l, 0, j)),
        out_shape=jax.ShapeDtypeStruct((depth, MOD_ROWS, d6), F32),
        compiler_params=_cparams(2),
        name="modulation",
    )(cc, w_mod, b_mod.reshape(depth, 1, d6))


def _rot_half(xv, lane, seg):
    first = (lane & (2 * seg - 1)) < seg
    return jnp.where(first, pltpu.roll(xv, LANES - seg, 1), pltpu.roll(xv, seg, 1))


def _seg_rsqrt(xw, seg_ref):
    return lax.rsqrt(_dot((xw * xw).astype(BF16), seg_ref[...]) + EPS)


def _inproj_kernel(*refs, n_lat_tiles, sub):
    xl_refs, refs = refs[:sub], refs[sub:]
    xc_ref, refs = refs[0], refs[1:]
    mod_refs, refs = refs[:sub], refs[sub:]
    (g1_ref, wa_ref, wuq_ref, wkn_ref, wvt_ref, gcq_ref, gckv_ref, gkn_ref, tg_ref, tab_ref,
     sgqa_ref, sqb_ref, skn_ref, skk_ref, qa_ref, ka_ref, vat_ref, qb_ref, kb_ref, vbt_ref, pf_ref) = refs
    lane = lax.broadcasted_iota(jnp.int32, (1, LANES), 1)
    lo = lane < 64
    hi = lane >= 64
    row = lax.broadcasted_iota(jnp.int32, (LANES, TILE), 0)
    blk = lambda a, j: a[:, j * LANES:(j + 1) * LANES]
    n_qblk = GQA_HEADS * GQA_HEAD_DIM // LANES
    w2 = 2 * LANES

    def project(j):
        rows = slice(j * TILE, (j + 1) * TILE)
        m = mod_refs[j][0, 0]
        hb = _modulated(_tile_rows(xl_refs, xc_ref, j, n_lat_tiles), g1_ref[0], m[0:1], m[1:2])
        p_c = _dot(hb, wa_ref[0, :, _CQ0:_CKV0])
        p_ckv = _dot(hb, wa_ref[0, :, _CKV0:_F0])
        pq = _dot(hb, wa_ref[0, :, _Q0:_K0])
        pkv = _dot(hb, wa_ref[0, :, _K0:_CQ0])
        pf_ref[0, rows] = _dot(hb, wa_ref[0, :, _F0:_WA]).astype(BF16)
        cq = _rms_rows(p_c[:, :MLA_Q_RANK], gcq_ref[0]).astype(BF16)
        ckv = _rms_rows(p_ckv, gckv_ref[0])
        qb = _dot(cq, wuq_ref[0])
        kn_all = _dot(ckv.astype(BF16), wkn_ref[0])
        vbt = _dot(wvt_ref[0], ckv.T.astype(BF16))
        for h in range(MLA_HEADS):
            vh = vbt[h * LANES:(h + 1) * LANES]
            ones_row = _ONES_ROW_EVEN if h % 2 == 0 else _ONES_ROW_ODD
            vbt_ref[0, h, :, rows] = jnp.where(row == ones_row, 1.0, vh).astype(BF16)
        vt = pkv[:, LANES:].T
        for kvh in range(GQA_KV_HEADS):
            own = (row >= 64) == (kvh == 1)
            top = jnp.where(own, vt, 0.0) if kvh == 0 else pltpu.roll(jnp.where(own, vt, 0.0), 64, 0)
            bot = pltpu.roll(top, 64, 0)
            vat_ref[0, 2 * kvh, :, rows] = jnp.where(row == _ONES_ROW_EVEN, 1.0, top).astype(BF16)
            vat_ref[0, 2 * kvh + 1, :, rows] = jnp.where(row == _ONES_ROW_ODD, 1.0, bot).astype(BF16)
        return pq, pkv[:, :LANES], qb, kn_all, p_c[:, MLA_Q_RANK:]

    def finish(j, pq, xk, qb, kn_all, xkr):
        rows = slice(j * TILE, (j + 1) * TILE)
        r_qb = jnp.concatenate([_seg_rsqrt(qb[:, c * w2:(c + 1) * w2], sqb_ref) for c in range(MLA_HEADS // 2)],
                               axis=1)
        r_kn = jnp.concatenate([_seg_rsqrt(kn_all[:, c * w2:(c + 1) * w2], skn_ref) for c in range(MLA_HEADS // 2)],
                               axis=1)
        r_q = jnp.concatenate([_seg_rsqrt(pq[:, c * w2:(c + 1) * w2], sgqa_ref) for c in range(n_qblk // 2)], axis=1)
        r_kk = _seg_rsqrt(jnp.concatenate([xk, xkr], axis=1), skk_ref)
        r_k, r_kr = r_kk[:, :LANES], r_kk[:, LANES:]

        tab = [tab_ref[i, rows] * tg_ref[0, i:i + 1] for i in range(2 * N_TABLES)]

        def roped(xb, table, seg, r):
            return (xb * tab[table] + _rot_half(xb, lane, seg) * tab[N_TABLES + table]) * r

        for h in range(MLA_HEADS):
            qb_ref[0, rows, h * LANES:(h + 1) * LANES] = roped(blk(qb, h), TAB_QB, 8, blk(r_qb, h)).astype(BF16)
        kr = roped(xkr, TAB_KR, 8, r_kr)
        for h in range(MLA_HEADS):
            kb_ref[0, rows, h * LANES:(h + 1) * LANES] = (blk(kn_all, h) * blk(r_kn, h) * gkn_ref[0] + kr).astype(BF16)
        for c in range(n_qblk):
            qa_ref[0, rows, c * LANES:(c + 1) * LANES] = roped(blk(pq, c), TAB_Q, 16, blk(r_q, c)).astype(BF16)
        kn = roped(xk, TAB_K, 16, r_k)
        sw = pltpu.roll(kn, 64, 1)
        for c, f in enumerate((jnp.where(lo, kn, 0.0), jnp.where(hi, sw, 0.0), jnp.where(lo, sw, 0.0),
                               jnp.where(hi, kn, 0.0))):
            ka_ref[0, rows, c * LANES:(c + 1) * LANES] = f.astype(BF16)

    pending = project(0)
    for j in range(sub):
        nxt = project(j + 1) if j + 1 < sub else None
        finish(j, *pending)
        pending = nxt


def _inproj(x_lat, x_ctx, ctx_tile, mod, l, n_lat_tiles, g1, wts, tabs, segs):
    b = x_lat.shape[0]
    nt = n_lat_tiles + 1
    sub = _sub_tiles(nt)
    t, step_rows = nt * TILE, sub * TILE
    row = lambda bi, si: (bi, si, 0)
    rowt = lambda bi, si: (bi, 0, 0, si)
    tok = lambda w: (pl.BlockSpec((1, step_rows, w), row), jax.ShapeDtypeStruct((b, t, w), BF16))
    tra = lambda h: (pl.BlockSpec((1, h, LANES, step_rows), rowt), jax.ShapeDtypeStruct((b, h, LANES, t), BF16))
    outs = [tok(512), tok(512), tra(4), tok(1024), tok(1024), tra(MLA_HEADS), tok(512)]
    return pl.pallas_call(
        functools.partial(_inproj_kernel, n_lat_tiles=n_lat_tiles, sub=sub),
        grid=(b, nt // sub),
        in_specs=_row_specs(x_lat, x_ctx, ctx_tile, n_lat_tiles, sub) + _mod_specs(mod, l, b, n_lat_tiles, sub)
        + [_layer_spec(g1, l)] + [_layer_spec(w, l) for w in wts]
        + [pl.BlockSpec((2 * N_TABLES, step_rows, LANES), lambda bi, si: (0, si, 0))]
        + [_const_spec(s) for s in segs],
        out_specs=[o[0] for o in outs],
        out_shape=[o[1] for o in outs],
        compiler_params=_cparams(2),
        name="inproj",
    )(*([x_lat] * sub), x_ctx, *([mod] * sub), g1, *wts, tabs, *segs)


def _reduce_keys(a, op):
    nk, nq = a.shape
    part = op(a.reshape(nk // TILE, TILE, nq), axis=0)
    return op(part, axis=0, keepdims=True)


SCORE_SAFE = 60.0
BOUND_SLACK = 1.0 + 2.0 ** -6


def _attn_kernel(bound_ref, q_ref, k_ref, vt_ref, o_ref, *, n_lat_tiles, with_ctx, qoff, koff, vidx):
    n_keys = k_ref.shape[1]
    n_pairs = len(qoff) // 2
    zero = jnp.zeros((TILE, LANES), BF16)
    bound = bound_ref[0]

    def scores(row0, pair, key0, nk):
        e, o = 2 * pair, 2 * pair + 1
        assert koff[o] == koff[e] + LANES
        q_e = q_ref[0, pl.ds(row0, TILE), qoff[e]:qoff[e] + LANES]
        q_o = q_ref[0, pl.ds(row0, TILE), qoff[o]:qoff[o] + LANES]
        qd = jnp.concatenate([jnp.concatenate([q_e, zero], axis=1), jnp.concatenate([zero, q_o], axis=1)], axis=0)
        k2 = k_ref[0, key0:key0 + nk, koff[e]:koff[e] + 2 * LANES]
        return lax.dot_general(k2, qd, (((1,), (1,)), ((), ())), preferred_element_type=F32)

    def tile(row0, key0, nk, bounded):
        for pair in range(n_pairs):
            e, o = 2 * pair, 2 * pair + 1
            st = scores(row0, pair, key0, nk)
            pb = jnp.exp2(st if bounded else st - _reduce_keys(st, jnp.max)).astype(BF16)
            ot_e = _dot(vt_ref[0, vidx[e], :, key0:key0 + nk], pb[:, :TILE])
            ot_o = _dot(vt_ref[0, vidx[o], :, key0:key0 + nk], pb[:, TILE:])
            ot_e = ot_e * (1.0 / ot_e[_ONES_ROW_EVEN:_ONES_ROW_EVEN + 1])
            ot_o = ot_o * (1.0 / ot_o[_ONES_ROW_ODD:_ONES_ROW_ODD + 1])
            ot = jnp.where(lax.broadcasted_iota(jnp.int32, (LANES, TILE), 0) < 64, ot_e, ot_o)
            o_ref[0, pl.ds(row0, TILE), pair * LANES:(pair + 1) * LANES] = ot.T.astype(BF16)

    def all_tiles(bounded):
        def body(t, carry):
            tile(pl.multiple_of(t * TILE, TILE), 0, n_keys, bounded)
            return carry
        lax.fori_loop(0, n_lat_tiles, body, 0, unroll=2 if bounded else 1)
        if with_ctx:
            lat = n_lat_tiles * TILE
            tile(lat, lat, n_keys - lat, bounded)

    lax.cond(bound < SCORE_SAFE, lambda: all_tiles(True), lambda: all_tiles(False))


def _attention(shift, q, k, vt, n_lat_tiles, n_q_tiles, qoff, koff, vidx, name):
    b, t, wq = q.shape
    wk = k.shape[2]
    nh = vt.shape[1]
    kern = functools.partial(_attn_kernel, n_lat_tiles=n_lat_tiles, with_ctx=n_q_tiles > n_lat_tiles,
                             qoff=qoff, koff=koff, vidx=vidx)
    return pl.pallas_call(
        kern,
        grid=(b,),
        in_specs=[pl.BlockSpec(memory_space=pltpu.SMEM),
                  pl.BlockSpec((1, t, wq), lambda bi: (bi, 0, 0)),
                  pl.BlockSpec((1, t, wk), lambda bi: (bi, 0, 0)),
                  pl.BlockSpec((1, nh, LANES, t), lambda bi: (bi, 0, 0, 0))],
        out_specs=pl.BlockSpec((1, n_q_tiles * TILE, 512), lambda bi: (bi, 0, 0)),
        out_shape=jax.ShapeDtypeStruct((b, n_q_tiles * TILE, 512), BF16),
        compiler_params=_cparams(1),
        name=name,
    )(shift, q, k, vt)


def _fourier_kernel(pf_ref, cn_ref, sn_ref, dc_ref, cs_ref, flip_ref, alt_ref, o_ref, a_ref, b_ref, *, n_lat_tiles,
                    with_ctx):
    n_lat = n_lat_tiles * TILE
    half_tiles = n_lat_tiles // 2
    w = FOURIER_WIDTH
    for c in range(n_lat_tiles):
        ab = _dot(pf_ref[0, c * TILE:(c + 1) * TILE, :], cs_ref[...])
        a_ref[c * TILE:(c + 1) * TILE, :] = ab[:, :w].astype(BF16)
        b_ref[c * TILE:(c + 1) * TILE, :] = ab[:, w:].astype(BF16)
    row0 = lax.broadcasted_iota(jnp.int32, (TILE, w), 0) == 0
    carry = _dot(alt_ref[...], a_ref[...])[0:1] * (1.0 / math.sqrt(n_lat))
    for m in reversed(range(half_tiles)):
        g = _dot(cn_ref[m * TILE:(m + 1) * TILE, :], a_ref[...])
        h = _dot(sn_ref[m * TILE:(m + 1) * TILE, :], b_ref[...])
        o_ref[0, m * TILE:(m + 1) * TILE, :] = (g - h).astype(BF16)
        mirrored = (g + h).astype(BF16)
        tile_rev = _dot(flip_ref[...], mirrored)
        bot = n_lat_tiles - 1 - m
        o_ref[0, bot * TILE:(bot + 1) * TILE, :] = jnp.where(row0, carry, tile_rev).astype(BF16)
        carry = mirrored[0:1].astype(F32)
    if with_ctx:
        ab = _dot(pf_ref[0, n_lat:n_lat + TILE, :], cs_ref[...])
        abc = jnp.concatenate([ab[:, :w], ab[:, w:]], axis=0).astype(BF16)
        o_ref[0, n_lat:n_lat + TILE, :] = _dot(dc_ref[...], abc).astype(BF16)


def _fourier(pf, tables, n_lat_tiles, n_tiles):
    b, t, w = pf.shape
    n_lat = n_lat_tiles * TILE
    kern = functools.partial(_fourier_kernel, n_lat_tiles=n_lat_tiles, with_ctx=n_tiles > n_lat_tiles)
    return pl.pallas_call(
        kern,
        grid=(b,),
        in_specs=[pl.BlockSpec((1, t, w), lambda bi: (bi, 0, 0))] + [_const_spec(a) for a in tables],
        out_specs=pl.BlockSpec((1, n_tiles * TILE, w), lambda bi: (bi, 0, 0)),
        out_shape=jax.ShapeDtypeStruct((b, n_tiles * TILE, w), BF16),
        scratch_shapes=[pltpu.VMEM((n_lat, w), BF16), pltpu.VMEM((n_lat, w), BF16)],
        compiler_params=_cparams(1),
        name="fourier",
    )(pf, *tables)


def _merge_kernel(*refs, n_lat_tiles, sub):
    xl_refs, refs = refs[:sub], refs[sub:]
    xc_ref, refs = refs[0], refs[1:]
    mod_refs, refs = refs[:sub], refs[sub:]
    g1_ref, ya_ref, yb_ref, yc_ref, wg_ref, bg_ref, wa_ref, wb_ref, wc_ref, wo_ref, o_ref = refs
    for j in range(sub):
        rows = slice(j * TILE, (j + 1) * TILE)
        x = _tile_rows(xl_refs, xc_ref, j, n_lat_tiles)
        m = mod_refs[j][0, 0]
        d = x.shape[-1]
        hb = _modulated(x, g1_ref[0], m[0:1], m[1:2])
        acc = None
        for i, (y_ref, w_ref) in enumerate(((ya_ref, wa_ref), (yb_ref, wb_ref), (yc_ref, wc_ref))):
            gate = jax.nn.sigmoid(_dot(hb, wg_ref[0, :, i * d:(i + 1) * d]) + bg_ref[0, :, i * d:(i + 1) * d])
            term = gate * _dot(y_ref[0, rows], w_ref[0])
            acc = term if acc is None else acc + term
        o_ref[0, rows] = x + m[2:3] * _dot(acc.astype(BF16), wo_ref[0])


def _merge(x_lat, x_ctx, ctx_tile, mod, l, n_lat_tiles, n_tiles, g1, ya, yb, yc, wts):
    b, _, d = x_lat.shape
    sub = _sub_tiles(n_tiles)
    row = lambda bi, si: (bi, si, 0)
    return pl.pallas_call(
        functools.partial(_merge_kernel, n_lat_tiles=n_lat_tiles, sub=sub),
        grid=(b, n_tiles // sub),
        in_specs=_row_specs(x_lat, x_ctx, ctx_tile, n_lat_tiles, sub) + _mod_specs(mod, l, b, n_lat_tiles, sub)
        + [_layer_spec(g1, l)] + [pl.BlockSpec((1, sub * TILE, 512), row)] * 3 + [_layer_spec(w, l) for w in wts],
        out_specs=pl.BlockSpec((1, sub * TILE, d), row),
        out_shape=jax.ShapeDtypeStruct((b, n_tiles * TILE, d), F32),
        compiler_params=_cparams(2),
        name="merge",
    )(*([x_lat] * sub), x_ctx, *([mod] * sub), g1, ya, yb, yc, *wts)


def _ffn_kernel(*refs, sub):
    x_ref, refs = refs[0], refs[1:]
    mod_refs, refs = refs[:sub], refs[sub:]
    g2_ref, wi_ref, wo_ref, o_ref = refs
    for j in range(sub):
        rows = slice(j * TILE, (j + 1) * TILE)
        x = x_ref[0, rows]
        m = mod_refs[j][0, 0]
        hb = _modulated(x, g2_ref[0], m[3:4], m[4:5])
        gu = _dot(hb, wi_ref[0])
        hid = gu.shape[-1] // 2
        gate, up = gu[:, :hid], gu[:, hid:]
        act = (gate * jax.nn.sigmoid(gate) * up).astype(BF16)
        o_ref[0, rows] = x + m[5:6] * _dot(act, wo_ref[0])


def _ffn(xm, mod, l, n_lat_tiles, n_tiles, g2, wi, wo):
    b, _, d = xm.shape
    sub = _sub_tiles(n_tiles)
    row = lambda bi, si: (bi, si, 0)
    once = lambda a: pl.BlockSpec((1,) + a.shape[1:], lambda *_: (l,) + (0,) * (a.ndim - 1),
                                  pipeline_mode=pl.Buffered(1))
    return pl.pallas_call(
        functools.partial(_ffn_kernel, sub=sub),
        grid=(b, n_tiles // sub),
        in_specs=[pl.BlockSpec((1, sub * TILE, d), row)] + _mod_specs(mod, l, b, n_lat_tiles, sub)
        + [_layer_spec(g2, l), once(wi), once(wo)],
        out_specs=pl.BlockSpec((1, sub * TILE, d), row),
        out_shape=jax.ShapeDtypeStruct((b, n_tiles * TILE, d), F32),
        compiler_params=_cparams(2),
        name="ffn",
    )(xm, *([mod] * sub), g2, wi, wo)


def _partner(seg):
    lane = np.arange(LANES)
    return np.where((lane % (2 * seg)) < seg, lane + seg, lane - seg)


def _rope_tables(n_lat, n_ctx):
    rows = n_lat // GRID_W
    row_id = np.repeat(np.arange(rows), GRID_W).astype(np.float64)
    col_id = np.tile(np.arange(GRID_W), rows).astype(np.float64)

    def angles(dim):
        half = dim // 2
        freqs = ROPE_THETA ** (-np.arange(0, half, 2, dtype=np.float64) / half)
        ax = lambda pos: np.concatenate([pos[:, None] * freqs[None, :]] * 2, axis=-1)
        return np.concatenate([ax(row_id), ax(col_id)], axis=-1)

    def signed(sin, dim):
        sign = np.where((np.arange(dim) % (dim // 2)) < dim // 4, -1.0, 1.0)
        return sin * sign[None, :]

    a64, a32 = angles(GQA_HEAD_DIM), angles(MLA_ROPE_DIM)
    t = n_lat + n_ctx
    cosa, sina = np.ones((t, LANES)), np.zeros((t, LANES))
    cosa[:n_lat] = np.tile(np.cos(a64), (1, 2))
    sina[:n_lat] = np.tile(signed(np.sin(a64), GQA_HEAD_DIM), (1, 2))
    cosb, sinb = np.ones((t, LANES)), np.zeros((t, LANES))
    cosb[:n_lat, 64:96] = np.cos(a32)
    sinb[:n_lat, 64:96] = signed(np.sin(a32), MLA_ROPE_DIM)
    return jnp.asarray(np.stack([cosa, cosa, cosb, cosb, sina, sina, sinb, sinb]), F32)


TAB_Q, TAB_K, TAB_QB, TAB_KR, N_TABLES = 0, 1, 2, 3, 4


def _table_gains(gains, scales):
    g = jnp.stack(gains, axis=1) * jnp.asarray(scales, F32)[None, :, None]
    partner = np.stack([_partner(16), _partner(16), _partner(8), _partner(8)])
    return jnp.concatenate([g, jnp.take_along_axis(g, jnp.asarray(partner)[None], axis=-1)], axis=1)


def _seg_matrices():
    def blockdiag(segs):
        m = np.zeros((2 * LANES, 2 * LANES))
        for start, n in segs:
            m[start:start + n, start:start + n] = 1.0 / n
        return m

    gqa = blockdiag([(s0, 64) for s0 in range(0, 256, 64)])
    qb = blockdiag([(0, 64), (64, 32), (128, 64), (192, 32)])
    kn = blockdiag([(0, 64), (128, 64)])
    kk = blockdiag([(0, 64), (64, 64), (192, 32)])
    return [jnp.asarray(a, F32).astype(BF16) for a in (gqa, qb, kn, kk)]


def _score_bounds(g_q_gqa, g_k_gqa, g_q_nope, g_k_nope, g_q_rope, g_k_rope):
    top = lambda g: jnp.max(g * g, axis=-1)
    gqa = jnp.sqrt(GQA_HEAD_DIM * top(g_q_gqa) * GQA_HEAD_DIM * top(g_k_gqa)) * (GQA_SCALE * LOG2E)
    mla = jnp.sqrt((MLA_NOPE_DIM * top(g_q_nope) + MLA_ROPE_DIM * top(g_q_rope))
                   * (MLA_NOPE_DIM * top(g_k_nope) + MLA_ROPE_DIM * top(g_k_rope))) * (MLA_SCALE * LOG2E)
    return jnp.stack([gqa, mla], axis=-1) * BOUND_SLACK


def _dft_tables(n_lat, n_ctx):
    def cs(n):
        j = np.arange(n)
        ang = 2.0 * np.pi * ((j[:, None] * j[None, :]) % n) / n
        return np.cos(ang) / np.sqrt(n), np.sin(ang) / np.sqrt(n)

    cn, sn = cs(n_lat)
    cc, sc = cs(n_ctx)
    cg, sg = cs(FOURIER_GROUP_DIM)
    eye = np.eye(FOURIER_GROUPS)
    dc = np.concatenate([cc, -sc], axis=1)
    chan = np.concatenate([np.kron(eye, cg), np.kron(eye, sg)], axis=1)
    flip = np.zeros((TILE, TILE))
    flip[np.arange(1, TILE), TILE - np.arange(1, TILE)] = 1.0
    alt = np.tile(np.where(np.arange(n_lat) % 2 == 0, 1.0, -1.0)[None, :], (8, 1))
    return [jnp.asarray(a, F32).astype(BF16) for a in (cn[:n_lat // 2], sn[:n_lat // 2], dc, chan, flip, alt)]


_W_IN_COLS = 5024
_GATE0 = 1952


def _split_kernel(wt_ref, wa_ref, wg_ref):
    d = wt_ref.shape[2]
    piece = 256

    def emit(out_ref, col0, row0, n):
        for c in range(0, n, piece):
            w = min(piece, n - c)
            out_ref[0, :, col0 + c:col0 + c + w] = wt_ref[0, row0 + c:row0 + c + w, :].T.astype(BF16)

    emit(wa_ref, _Q0, 0, _KR0)
    kr_rows = jnp.concatenate([jnp.zeros((64, d), F32), wt_ref[0, 1408:1440, :], jnp.zeros((32, d), F32)], axis=0)
    wa_ref[0, :, _KR0:_CKV0] = kr_rows.T.astype(BF16)
    emit(wa_ref, _CKV0, 1152, _F0 - _CKV0)
    emit(wa_ref, _F0, 1440, _WA - _F0)
    emit(wg_ref, 0, _GATE0, _W_IN_COLS - _GATE0)


def _split_w_in(w_in):
    depth, d, cols = w_in.shape
    assert cols == _W_IN_COLS
    w_t = jnp.swapaxes(w_in, 1, 2)
    return pl.pallas_call(
        _split_kernel,
        grid=(depth,),
        in_specs=[pl.BlockSpec((1, cols, d), lambda l: (l, 0, 0), pipeline_mode=pl.Buffered(1))],
        out_specs=[pl.BlockSpec((1, d, _WA), lambda l: (l, 0, 0)),
                   pl.BlockSpec((1, d, cols - _GATE0), lambda l: (l, 0, 0))],
        out_shape=[jax.ShapeDtypeStruct((depth, d, _WA), BF16), jax.ShapeDtypeStruct((depth, d, cols - _GATE0), BF16)],
        compiler_params=_cparams(1),
        name="split_w_in",
    )(w_t)


def _inproj_weights(w_in, g_q_gqa, g_k_gqa, g_cq, g_ckv, w_uq, w_ukv, g_q_nope, g_k_nope, g_q_rope, g_k_rope):
    depth = w_in.shape[0]
    wa, gates = _split_w_in(w_in)
    zeros = lambda *s: jnp.zeros((depth,) + s, F32)
    uq = w_uq.reshape(depth, MLA_Q_RANK, MLA_HEADS, MLA_NOPE_DIM + MLA_ROPE_DIM)
    wuq = jnp.concatenate([uq, zeros(MLA_Q_RANK, MLA_HEADS, 32)], axis=-1).reshape(depth, MLA_Q_RANK, -1)
    ukv = w_ukv.reshape(depth, MLA_KV_RANK, MLA_HEADS, MLA_NOPE_DIM + MLA_V_DIM)
    kn, vv = ukv[..., :MLA_NOPE_DIM], ukv[..., MLA_NOPE_DIM:]
    z64 = zeros(MLA_KV_RANK, MLA_HEADS, 64)
    wkn = jnp.concatenate([kn, z64], axis=-1).reshape(depth, MLA_KV_RANK, -1)
    even = (jnp.arange(MLA_HEADS) % 2 == 0)[None, None, :, None]
    wv = jnp.where(even, jnp.concatenate([vv, z64], axis=-1), jnp.concatenate([z64, vv], axis=-1))
    wvt = jnp.swapaxes(wv.reshape(depth, MLA_KV_RANK, -1), 1, 2)
    gkn = jnp.concatenate([g_k_nope, zeros(64)], axis=-1)[:, None, :]
    z32, z64v = zeros(32), zeros(64)
    tgains = _table_gains([jnp.tile(g_q_gqa, (1, 2)), jnp.tile(g_k_gqa, (1, 2)),
                           jnp.concatenate([g_q_nope, g_q_rope, z32], axis=-1),
                           jnp.concatenate([z64v, g_k_rope, z32], axis=-1)],
                          [GQA_SCALE * LOG2E, 1.0, MLA_SCALE * LOG2E, 1.0])
    wts = [wa, wuq.astype(BF16), wkn.astype(BF16), wvt.astype(BF16), g_cq[:, None, :], g_ckv[:, None, :], gkn,
           tgains]
    return wts, gates


def kernel(x, c, ctx, c_ctx, w_mod, b_mod, g_norm1, g_norm2, w_in, g_q_gqa, g_k_gqa, g_cq, g_ckv, w_uq, w_ukv,
           g_q_nope, g_k_nope, g_q_rope, g_k_rope, b_gate, w_br_a, w_br_b, w_br_c, w_out, w_ffn_in, w_ffn_out):
    b, n, d = x.shape
    nc = ctx.shape[1]
    depth = w_mod.shape[0]
    assert n % TILE == 0 and nc == TILE and b + 1 <= MOD_ROWS and n % GRID_W == 0
    n_lat_tiles, n_tiles = n // TILE, (n + nc) // TILE

    cc = jnp.concatenate([c, c_ctx[None], jnp.zeros((MOD_ROWS - b - 1, d), F32)], axis=0)
    mod = _modulation(cc, w_mod, b_mod).reshape(depth, MOD_ROWS, 6, d)
    dft = _dft_tables(n, nc)
    segs = _seg_matrices()
    bounds = _score_bounds(g_q_gqa, g_k_gqa, g_q_nope, g_k_nope, g_q_rope, g_k_rope)
    tabs = _rope_tables(n, nc)
    wts, wg = _inproj_weights(w_in, g_q_gqa, g_k_gqa, g_cq, g_ckv, w_uq, w_ukv, g_q_nope, g_k_nope, g_q_rope,
                              g_k_rope)
    g1, g2 = g_norm1[:, None, :], g_norm2[:, None, :]
    mw = [wg, b_gate[:, None, :], w_br_a.astype(BF16), w_br_b.astype(BF16), w_br_c.astype(BF16), w_out.astype(BF16)]
    wi, wo = w_ffn_in.astype(BF16), w_ffn_out.astype(BF16)
    gqa_q = (0, 0, 128, 128, 256, 256, 384, 384)
    gqa_k = (0, 128, 0, 128, 256, 384, 256, 384)
    gqa_v = (0, 1, 0, 1, 2, 3, 2, 3)
    mla_o = tuple(range(0, 1024, 128))
    mla_h = tuple(range(MLA_HEADS))

    x_lat, x_ctx, ctx_tile = x, ctx, 0
    for l in range(depth):
        n_out = n_lat_tiles if l == depth - 1 else n_tiles
        qa, ka, vat, qb, kb, vbt, pf = _inproj(x_lat, x_ctx, ctx_tile, mod, l, n_lat_tiles, g1, wts, tabs, segs)
        ya = _attention(bounds[l, 0:1], qa, ka, vat, n_lat_tiles, n_out, gqa_q, gqa_k, gqa_v, "attn_gqa")
        yb = _attention(bounds[l, 1:2], qb, kb, vbt, n_lat_tiles, n_out, mla_o, mla_o, mla_h, "attn_mla")
        yc = _fourier(pf, dft, n_lat_tiles, n_out)
        xm = _merge(x_lat, x_ctx, ctx_tile, mod, l, n_lat_tiles, n_out, g1, ya, yb, yc, mw)
        xall = _ffn(xm, mod, l, n_lat_tiles, n_out, g2, wi, wo)
        x_lat, x_ctx, ctx_tile = xall, xall, n_lat_tiles
    return xall
```

```python
import functools
import math

import numpy as np
import jax
import jax.numpy as jnp
from jax import lax
from jax.experimental import pallas as pl
from jax.experimental.pallas import tpu as pltpu

GRID_W = 64
ROPE_THETA = 10000.0
EPS = 1e-6
GQA_HEADS = 8
GQA_KV_HEADS = 2
GQA_HEAD_DIM = 64
GQA_SCALE = GQA_HEAD_DIM ** -0.5
MLA_HEADS = 8
MLA_Q_RANK = 384
MLA_KV_RANK = 256
MLA_NOPE_DIM = 64
MLA_ROPE_DIM = 32
MLA_V_DIM = 64
MLA_SCALE = (MLA_NOPE_DIM + MLA_ROPE_DIM) ** -0.5
FOURIER_GROUPS = 4
FOURIER_GROUP_DIM = 128
FOURIER_WIDTH = FOURIER_GROUPS * FOURIER_GROUP_DIM
LOG2E = math.log2(math.e)

LANES = 128
TILE = 256
MOD_ROWS = 16
VMEM_LIMIT = 56 * 1024 * 1024

BF16 = jnp.bfloat16
F32 = jnp.float32

_ONES_ROW_EVEN, _ONES_ROW_ODD = LANES - 1, 0

_Q0, _K0, _V0, _CQ0, _KR0, _CKV0, _F0, _WA = 0, 512, 640, 768, 1152, 1280, 1536, 2048


def _cparams(n_axes):
    return pltpu.CompilerParams(dimension_semantics=("arbitrary",) * n_axes, vmem_limit_bytes=VMEM_LIMIT)


def _dot(a, b):
    return jnp.dot(a, b, preferred_element_type=F32)


def _rms_rows(xv, g):
    return xv * lax.rsqrt(jnp.mean(xv * xv, axis=-1, keepdims=True) + EPS) * g


def _modulated(x, g, shift, scale):
    r = lax.rsqrt(jnp.mean(x * x, axis=-1, keepdims=True) + EPS)
    return (x * r * (g * (1.0 + scale)) + shift).astype(BF16)


def _layer_spec(a, l):
    return pl.BlockSpec((1,) + a.shape[1:], lambda *_: (l,) + (0,) * (a.ndim - 1))


def _const_spec(a):
    return pl.BlockSpec(a.shape, lambda *_: (0,) * a.ndim)


def _sub_tiles(n_tiles):
    return next(s for s in (3, 2, 1) if n_tiles % s == 0)


def _row_specs(x_lat, x_ctx, ctx_tile, n_lat_tiles, sub):
    d = x_lat.shape[-1]
    lat = lambda j: pl.BlockSpec((1, TILE, d), lambda bi, si: (bi, jnp.minimum(si * sub + j, n_lat_tiles - 1), 0))
    return [lat(j) for j in range(sub)] + [pl.BlockSpec((1, TILE, d), lambda bi, si: (bi, ctx_tile, 0))]


def _mod_specs(mod, l, n_batch, n_lat_tiles, sub):
    spec = lambda j: pl.BlockSpec((1, 1) + mod.shape[2:],
                                  lambda bi, si: (l, jnp.where(si * sub + j >= n_lat_tiles, n_batch, bi), 0, 0))
    return [spec(j) for j in range(sub)]


def _tile_rows(xl_refs, xc_ref, j, n_lat_tiles):
    return jnp.where(pl.program_id(1) * len(xl_refs) + j < n_lat_tiles, xl_refs[j][0], xc_ref[0])


def _mod_kernel(cc_ref, w_ref, b_ref, o_ref):
    s = cc_ref[...]
    s = s * jax.nn.sigmoid(s)
    o_ref[0] = _dot(s.astype(BF16), w_ref[0].astype(BF16)) + b_ref[0]


def _modulation(cc, w_mod, b_mod):
    depth, d, d6 = w_mod.shape
    bn = 1536
    return pl.pallas_call(
        _mod_kernel,
        grid=(depth, d6 // bn),
        in_specs=[pl.BlockSpec((MOD_ROWS, d), lambda l, j: (0, 0)),
                  pl.BlockSpec((1, d, bn), lambda l, j: (l, 0, j)),
                  pl.BlockSpec((1, 1, bn), lambda l, j: (l, 0, j))],
        out_specs=pl.BlockSpec((1, MOD_ROWS, bn), lambda l, j: (l, 0, j)),
        out_shape=jax.ShapeDtypeStruct((depth, MOD_ROWS, d6), F32),
        compiler_params=_cparams(2),
        name="modulation",
    )(cc, w_mod, b_mod.reshape(depth, 1, d6))


def _rot_half(xv, lane, seg):
    first = (lane & (2 * seg - 1)) < seg
    return jnp.where(first, pltpu.roll(xv, LANES - seg, 1), pltpu.roll(xv, seg, 1))


def _seg_rsqrt(xw, seg_ref):
    return lax.rsqrt(_dot((xw * xw).astype(BF16), seg_ref[...]) + EPS)


def _inproj_kernel(*refs, n_lat_tiles, sub):
    xl_refs, refs = refs[:sub], refs[sub:]
    xc_ref, refs = refs[0], refs[1:]
    mod_refs, refs = refs[:sub], refs[sub:]
    (g1_ref, wa_ref, wuq_ref, wkn_ref, wvt_ref, gcq_ref, gckv_ref, gkn_ref, tg_ref, tab_ref,
     sgqa_ref, sqb_ref, skn_ref, skk_ref, qa_ref, ka_ref, vat_ref, qb_ref, kb_ref, vbt_ref, pf_ref) = refs
    lane = lax.broadcasted_iota(jnp.int32, (1, LANES), 1)
    lo = lane < 64
    hi = lane >= 64
    row = lax.broadcasted_iota(jnp.int32, (LANES, TILE), 0)
    blk = lambda a, j: a[:, j * LANES:(j + 1) * LANES]
    n_qblk = GQA_HEADS * GQA_HEAD_DIM // LANES
    w2 = 2 * LANES

    def project(j):
        rows = slice(j * TILE, (j + 1) * TILE)
        m = mod_refs[j][0, 0]
        hb = _modulated(_tile_rows(xl_refs, xc_ref, j, n_lat_tiles), g1_ref[0], m[0:1], m[1:2])
        p_c = _dot(hb, wa_ref[0, :, _CQ0:_CKV0])
        p_ckv = _dot(hb, wa_ref[0, :, _CKV0:_F0])
        pq = _dot(hb, wa_ref[0, :, _Q0:_K0])
        pkv = _dot(hb, wa_ref[0, :, _K0:_CQ0])
        pf_ref[0, rows] = _dot(hb, wa_ref[0, :, _F0:_WA]).astype(BF16)
        cq = _rms_rows(p_c[:, :MLA_Q_RANK], gcq_ref[0]).astype(BF16)
        ckv = _rms_rows(p_ckv, gckv_ref[0])
        qb = _dot(cq, wuq_ref[0])
        kn_all = _dot(ckv.astype(BF16), wkn_ref[0])
        vbt = _dot(wvt_ref[0], ckv.T.astype(BF16))
        for h in range(MLA_HEADS):
            vh = vbt[h * LANES:(h + 1) * LANES]
            ones_row = _ONES_ROW_EVEN if h % 2 == 0 else _ONES_ROW_ODD
            vbt_ref[0, h, :, rows] = jnp.where(row == ones_row, 1.0, vh).astype(BF16)
        vt = pkv[:, LANES:].T
        for kvh in range(GQA_KV_HEADS):
            own = (row >= 64) == (kvh == 1)
            top = jnp.where(own, vt, 0.0) if kvh == 0 else pltpu.roll(jnp.where(own, vt, 0.0), 64, 0)
            bot = pltpu.roll(top, 64, 0)
            vat_ref[0, 2 * kvh, :, rows] = jnp.where(row == _ONES_ROW_EVEN, 1.0, top).astype(BF16)
            vat_ref[0, 2 * kvh + 1, :, rows] = jnp.where(row == _ONES_ROW_ODD, 1.0, bot).astype(BF16)
        return pq, pkv[:, :LANES], qb, kn_all, p_c[:, MLA_Q_RANK:]

    def finish(j, pq, xk, qb, kn_all, xkr):
        rows = slice(j * TILE, (j + 1) * TILE)
        r_qb = jnp.concatenate([_seg_rsqrt(qb[:, c * w2:(c + 1) * w2], sqb_ref) for c in range(MLA_HEADS // 2)],
                               axis=1)
        r_kn = jnp.concatenate([_seg_rsqrt(kn_all[:, c * w2:(c + 1) * w2], skn_ref) for c in range(MLA_HEADS // 2)],
                               axis=1)
        r_q = jnp.concatenate([_seg_rsqrt(pq[:, c * w2:(c + 1) * w2], sgqa_ref) for c in range(n_qblk // 2)], axis=1)
        r_kk = _seg_rsqrt(jnp.concatenate([xk, xkr], axis=1), skk_ref)
        r_k, r_kr = r_kk[:, :LANES], r_kk[:, LANES:]

        tab = [tab_ref[i, rows] * tg_ref[0, i:i + 1] for i in range(2 * N_TABLES)]

        def roped(xb, table, seg, r):
            return (xb * tab[table] + _rot_half(xb, lane, seg) * tab[N_TABLES + table]) * r

        for h in range(MLA_HEADS):
            qb_ref[0, rows, h * LANES:(h + 1) * LANES] = roped(blk(qb, h), TAB_QB, 8, blk(r_qb, h)).astype(BF16)
        kr = roped(xkr, TAB_KR, 8, r_kr)
        for h in range(MLA_HEADS):
            kb_ref[0, rows, h * LANES:(h + 1) * LANES] = (blk(kn_all, h) * blk(r_kn, h) * gkn_ref[0] + kr).astype(BF16)
        for c in range(n_qblk):
            qa_ref[0, rows, c * LANES:(c + 1) * LANES] = roped(blk(pq, c), TAB_Q, 16, blk(r_q, c)).astype(BF16)
        kn = roped(xk, TAB_K, 16, r_k)
        sw = pltpu.roll(kn, 64, 1)
        for c, f in enumerate((jnp.where(lo, kn, 0.0), jnp.where(hi, sw, 0.0), jnp.where(lo, sw, 0.0),
                               jnp.where(hi, kn, 0.0))):
            ka_ref[0, rows, c * LANES:(c + 1) * LANES] = f.astype(BF16)

    pending = project(0)
    for j in range(sub):
        nxt = project(j + 1) if j + 1 < sub else None
        finish(j, *pending)
        pending = nxt


def _inproj(x_lat, x_ctx, ctx_tile, mod, l, n_lat_tiles, g1, wts, tabs, segs):
    b = x_lat.shape[0]
    nt = n_lat_tiles + 1
    sub = _sub_tiles(nt)
    t, step_rows = nt * TILE, sub * TILE
    row = lambda bi, si: (bi, si, 0)
    rowt = lambda bi, si: (bi, 0, 0, si)
    tok = lambda w: (pl.BlockSpec((1, step_rows, w), row), jax.ShapeDtypeStruct((b, t, w), BF16))
    tra = lambda h: (pl.BlockSpec((1, h, LANES, step_rows), rowt), jax.ShapeDtypeStruct((b, h, LANES, t), BF16))
    outs = [tok(512), tok(512), tra(4), tok(1024), tok(1024), tra(MLA_HEADS), tok(512)]
    return pl.pallas_call(
        functools.partial(_inproj_kernel, n_lat_tiles=n_lat_tiles, sub=sub),
        grid=(b, nt // sub),
        in_specs=_row_specs(x_lat, x_ctx, ctx_tile, n_lat_tiles, sub) + _mod_specs(mod, l, b, n_lat_tiles, sub)
        + [_layer_spec(g1, l)] + [_layer_spec(w, l) for w in wts]
        + [pl.BlockSpec((2 * N_TABLES, step_rows, LANES), lambda bi, si: (0, si, 0))]
        + [_const_spec(s) for s in segs],
        out_specs=[o[0] for o in outs],
        out_shape=[o[1] for o in outs],
        compiler_params=_cparams(2),
        name="inproj",
    )(*([x_lat] * sub), x_ctx, *([mod] * sub), g1, *wts, tabs, *segs)


def _reduce_keys(a, op):
    nk, nq = a.shape
    part = op(a.reshape(nk // TILE, TILE, nq), axis=0)
    return op(part, axis=0, keepdims=True)


SCORE_SAFE = 60.0
BOUND_SLACK = 1.0 + 2.0 ** -6


def _attn_kernel(bound_ref, q_ref, k_ref, vt_ref, o_ref, *, n_lat_tiles, with_ctx, qoff, koff, vidx):
    n_keys = k_ref.shape[1]
    n_pairs = len(qoff) // 2
    zero = jnp.zeros((TILE, LANES), BF16)
    bound = bound_ref[0]

    def scores(row0, pair, key0, nk):
        e, o = 2 * pair, 2 * pair + 1
        assert koff[o] == koff[e] + LANES
        q_e = q_ref[0, pl.ds(row0, TILE), qoff[e]:qoff[e] + LANES]
        q_o = q_ref[0, pl.ds(row0, TILE), qoff[o]:qoff[o] + LANES]
        qd = jnp.concatenate([jnp.concatenate([q_e, zero], axis=1), jnp.concatenate([zero, q_o], axis=1)], axis=0)
        k2 = k_ref[0, key0:key0 + nk, koff[e]:koff[e] + 2 * LANES]
        return lax.dot_general(k2, qd, (((1,), (1,)), ((), ())), preferred_element_type=F32)

    def tile(row0, key0, nk, bounded):
        for pair in range(n_pairs):
            e, o = 2 * pair, 2 * pair + 1
            st = scores(row0, pair, key0, nk)
            pb = jnp.exp2(st if bounded else st - _reduce_keys(st, jnp.max)).astype(BF16)
            ot_e = _dot(vt_ref[0, vidx[e], :, key0:key0 + nk], pb[:, :TILE])
            ot_o = _dot(vt_ref[0, vidx[o], :, key0:key0 + nk], pb[:, TILE:])
            ot_e = ot_e * (1.0 / ot_e[_ONES_ROW_EVEN:_ONES_ROW_EVEN + 1])
            ot_o = ot_o * (1.0 / ot_o[_ONES_ROW_ODD:_ONES_ROW_ODD + 1])
            ot = jnp.where(lax.broadcasted_iota(jnp.int32, (LANES, TILE), 0) < 64, ot_e, ot_o)
            o_ref[0, pl.ds(row0, TILE), pair * LANES:(pair + 1) * LANES] = ot.T.astype(BF16)

    def all_tiles(bounded):
        def body(t, carry):
            tile(pl.multiple_of(t * TILE, TILE), 0, n_keys, bounded)
            return carry
        lax.fori_loop(0, n_lat_tiles, body, 0, unroll=4 if bounded else 1)
        if with_ctx:
            lat = n_lat_tiles * TILE
            tile(lat, lat, n_keys - lat, bounded)

    lax.cond(bound < SCORE_SAFE, lambda: all_tiles(True), lambda: all_tiles(False))


def _attention(shift, q, k, vt, n_lat_tiles, n_q_tiles, qoff, koff, vidx, name):
    b, t, wq = q.shape
    wk = k.shape[2]
    nh = vt.shape[1]
    kern = functools.partial(_attn_kernel, n_lat_tiles=n_lat_tiles, with_ctx=n_q_tiles > n_lat_tiles,
                             qoff=qoff, koff=koff, vidx=vidx)
    return pl.pallas_call(
        kern,
        grid=(b,),
        in_specs=[pl.BlockSpec(memory_space=pltpu.SMEM),
                  pl.BlockSpec((1, t, wq), lambda bi: (bi, 0, 0)),
                  pl.BlockSpec((1, t, wk), lambda bi: (bi, 0, 0)),
                  pl.BlockSpec((1, nh, LANES, t), lambda bi: (bi, 0, 0, 0))],
        out_specs=pl.BlockSpec((1, n_q_tiles * TILE, 512), lambda bi: (bi, 0, 0)),
        out_shape=jax.ShapeDtypeStruct((b, n_q_tiles * TILE, 512), BF16),
        compiler_params=_cparams(1),
        name=name,
    )(shift, q, k, vt)


def _fourier_kernel(pf_ref, cn_ref, sn_ref, dc_ref, cs_ref, flip_ref, alt_ref, o_ref, a_ref, b_ref, *, n_lat_tiles,
                    with_ctx):
    n_lat = n_lat_tiles * TILE
    half_tiles = n_lat_tiles // 2
    w = FOURIER_WIDTH
    for c in range(n_lat_tiles):
        ab = _dot(pf_ref[0, c * TILE:(c + 1) * TILE, :], cs_ref[...])
        a_ref[c * TILE:(c + 1) * TILE, :] = ab[:, :w].astype(BF16)
        b_ref[c * TILE:(c + 1) * TILE, :] = ab[:, w:].astype(BF16)
    row0 = lax.broadcasted_iota(jnp.int32, (TILE, w), 0) == 0
    carry = _dot(alt_ref[...], a_ref[...])[0:1] * (1.0 / math.sqrt(n_lat))
    for m in reversed(range(half_tiles)):
        g = _dot(cn_ref[m * TILE:(m + 1) * TILE, :], a_ref[...])
        h = _dot(sn_ref[m * TILE:(m + 1) * TILE, :], b_ref[...])
        o_ref[0, m * TILE:(m + 1) * TILE, :] = (g - h).astype(BF16)
        mirrored = (g + h).astype(BF16)
        tile_rev = _dot(flip_ref[...], mirrored)
        bot = n_lat_tiles - 1 - m
        o_ref[0, bot * TILE:(bot + 1) * TILE, :] = jnp.where(row0, carry, tile_rev).astype(BF16)
        carry = mirrored[0:1].astype(F32)
    if with_ctx:
        ab = _dot(pf_ref[0, n_lat:n_lat + TILE, :], cs_ref[...])
        abc = jnp.concatenate([ab[:, :w], ab[:, w:]], axis=0).astype(BF16)
        o_ref[0, n_lat:n_lat + TILE, :] = _dot(dc_ref[...], abc).astype(BF16)


def _fourier(pf, tables, n_lat_tiles, n_tiles):
    b, t, w = pf.shape
    n_lat = n_lat_tiles * TILE
    kern = functools.partial(_fourier_kernel, n_lat_tiles=n_lat_tiles, with_ctx=n_tiles > n_lat_tiles)
    return pl.pallas_call(
        kern,
        grid=(b,),
        in_specs=[pl.BlockSpec((1, t, w), lambda bi: (bi, 0, 0))] + [_const_spec(a) for a in tables],
        out_specs=pl.BlockSpec((1, n_tiles * TILE, w), lambda bi: (bi, 0, 0)),
        out_shape=jax.ShapeDtypeStruct((b, n_tiles * TILE, w), BF16),
        scratch_shapes=[pltpu.VMEM((n_lat, w), BF16), pltpu.VMEM((n_lat, w), BF16)],
        compiler_params=_cparams(1),
        name="fourier",
    )(pf, *tables)


def _merge_kernel(*refs, n_lat_tiles, sub):
    xl_refs, refs = refs[:sub], refs[sub:]
    xc_ref, refs = refs[0], refs[1:]
    mod_refs, refs = refs[:sub], refs[sub:]
    g1_ref, ya_ref, yb_ref, yc_ref, wg_ref, bg_ref, wa_ref, wb_ref, wc_ref, wo_ref, o_ref = refs
    for j in range(sub):
        rows = slice(j * TILE, (j + 1) * TILE)
        x = _tile_rows(xl_refs, xc_ref, j, n_lat_tiles)
        m = mod_refs[j][0, 0]
        d = x.shape[-1]
        hb = _modulated(x, g1_ref[0], m[0:1], m[1:2])
        acc = None
        for i, (y_ref, w_ref) in enumerate(((ya_ref, wa_ref), (yb_ref, wb_ref), (yc_ref, wc_ref))):
            gate = jax.nn.sigmoid(_dot(hb, wg_ref[0, :, i * d:(i + 1) * d]) + bg_ref[0, :, i * d:(i + 1) * d])
            term = gate * _dot(y_ref[0, rows], w_ref[0])
            acc = term if acc is None else acc + term
        o_ref[0, rows] = x + m[2:3] * _dot(acc.astype(BF16), wo_ref[0])


def _merge(x_lat, x_ctx, ctx_tile, mod, l, n_lat_tiles, n_tiles, g1, ya, yb, yc, wts):
    b, _, d = x_lat.shape
    sub = _sub_tiles(n_tiles)
    row = lambda bi, si: (bi, si, 0)
    return pl.pallas_call(
        functools.partial(_merge_kernel, n_lat_tiles=n_lat_tiles, sub=sub),
        grid=(b, n_tiles // sub),
        in_specs=_row_specs(x_lat, x_ctx, ctx_tile, n_lat_tiles, sub) + _mod_specs(mod, l, b, n_lat_tiles, sub)
        + [_layer_spec(g1, l)] + [pl.BlockSpec((1, sub * TILE, 512), row)] * 3 + [_layer_spec(w, l) for w in wts],
        out_specs=pl.BlockSpec((1, sub * TILE, d), row),
        out_shape=jax.ShapeDtypeStruct((b, n_tiles * TILE, d), F32),
        compiler_params=_cparams(2),
        name="merge",
    )(*([x_lat] * sub), x_ctx, *([mod] * sub), g1, ya, yb, yc, *wts)


def _ffn_kernel(*refs, sub):
    x_ref, refs = refs[0], refs[1:]
    mod_refs, refs = refs[:sub], refs[sub:]
    g2_ref, wi_ref, wo_ref, o_ref = refs
    for j in range(sub):
        rows = slice(j * TILE, (j + 1) * TILE)
        x = x_ref[0, rows]
        m = mod_refs[j][0, 0]
        hb = _modulated(x, g2_ref[0], m[3:4], m[4:5])
        gu = _dot(hb, wi_ref[0])
        hid = gu.shape[-1] // 2
        gate, up = gu[:, :hid], gu[:, hid:]
        act = (gate * jax.nn.sigmoid(gate) * up).astype(BF16)
        o_ref[0, rows] = x + m[5:6] * _dot(act, wo_ref[0])


def _ffn(xm, mod, l, n_lat_tiles, n_tiles, g2, wi, wo):
    b, _, d = xm.shape
    sub = _sub_tiles(n_tiles)
    row = lambda bi, si: (bi, si, 0)
    once = lambda a: pl.BlockSpec((1,) + a.shape[1:], lambda *_: (l,) + (0,) * (a.ndim - 1),
                                  pipeline_mode=pl.Buffered(1))
    return pl.pallas_call(
        functools.partial(_ffn_kernel, sub=sub),
        grid=(b, n_tiles // sub),
        in_specs=[pl.BlockSpec((1, sub * TILE, d), row)] + _mod_specs(mod, l, b, n_lat_tiles, sub)
        + [_layer_spec(g2, l), once(wi), once(wo)],
        out_specs=pl.BlockSpec((1, sub * TILE, d), row),
        out_shape=jax.ShapeDtypeStruct((b, n_tiles * TILE, d), F32),
        compiler_params=_cparams(2),
        name="ffn",
    )(xm, *([mod] * sub), g2, wi, wo)


def _partner(seg):
    lane = np.arange(LANES)
    return np.where((lane % (2 * seg)) < seg, lane + seg, lane - seg)


def _rope_tables(n_lat, n_ctx):
    rows = n_lat // GRID_W
    row_id = np.repeat(np.arange(rows), GRID_W).astype(np.float64)
    col_id = np.tile(np.arange(GRID_W), rows).astype(np.float64)

    def angles(dim):
        half = dim // 2
        freqs = ROPE_THETA ** (-np.arange(0, half, 2, dtype=np.float64) / half)
        ax = lambda pos: np.concatenate([pos[:, None] * freqs[None, :]] * 2, axis=-1)
        return np.concatenate([ax(row_id), ax(col_id)], axis=-1)

    def signed(sin, dim):
        sign = np.where((np.arange(dim) % (dim // 2)) < dim // 4, -1.0, 1.0)
        return sin * sign[None, :]

    a64, a32 = angles(GQA_HEAD_DIM), angles(MLA_ROPE_DIM)
    t = n_lat + n_ctx
    cosa, sina = np.ones((t, LANES)), np.zeros((t, LANES))
    cosa[:n_lat] = np.tile(np.cos(a64), (1, 2))
    sina[:n_lat] = np.tile(signed(np.sin(a64), GQA_HEAD_DIM), (1, 2))
    cosb, sinb = np.ones((t, LANES)), np.zeros((t, LANES))
    cosb[:n_lat, 64:96] = np.cos(a32)
    sinb[:n_lat, 64:96] = signed(np.sin(a32), MLA_ROPE_DIM)
    return jnp.asarray(np.stack([cosa, cosa, cosb, cosb, sina, sina, sinb, sinb]), F32)


TAB_Q, TAB_K, TAB_QB, TAB_KR, N_TABLES = 0, 1, 2, 3, 4


def _table_gains(gains, scales):
    g = jnp.stack(gains, axis=1) * jnp.asarray(scales, F32)[None, :, None]
    partner = np.stack([_partner(16), _partner(16), _partner(8), _partner(8)])
    return jnp.concatenate([g, jnp.take_along_axis(g, jnp.asarray(partner)[None], axis=-1)], axis=1)


def _seg_matrices():
    def blockdiag(segs):
        m = np.zeros((2 * LANES, 2 * LANES))
        for start, n in segs:
            m[start:start + n, start:start + n] = 1.0 / n
        return m

    gqa = blockdiag([(s0, 64) for s0 in range(0, 256, 64)])
    qb = blockdiag([(0, 64), (64, 32), (128, 64), (192, 32)])
    kn = blockdiag([(0, 64), (128, 64)])
    kk = blockdiag([(0, 64), (64, 64), (192, 32)])
    return [jnp.asarray(a, F32).astype(BF16) for a in (gqa, qb, kn, kk)]


def _score_bounds(g_q_gqa, g_k_gqa, g_q_nope, g_k_nope, g_q_rope, g_k_rope):
    top = lambda g: jnp.max(g * g, axis=-1)
    gqa = jnp.sqrt(GQA_HEAD_DIM * top(g_q_gqa) * GQA_HEAD_DIM * top(g_k_gqa)) * (GQA_SCALE * LOG2E)
    mla = jnp.sqrt((MLA_NOPE_DIM * top(g_q_nope) + MLA_ROPE_DIM * top(g_q_rope))
                   * (MLA_NOPE_DIM * top(g_k_nope) + MLA_ROPE_DIM * top(g_k_rope))) * (MLA_SCALE * LOG2E)
    return jnp.stack([gqa, mla], axis=-1) * BOUND_SLACK


def _dft_tables(n_lat, n_ctx):
    def cs(n):
        j = np.arange(n)
        ang = 2.0 * np.pi * ((j[:, None] * j[None, :]) % n) / n
        return np.cos(ang) / np.sqrt(n), np.sin(ang) / np.sqrt(n)

    cn, sn = cs(n_lat)
    cc, sc = cs(n_ctx)
    cg, sg = cs(FOURIER_GROUP_DIM)
    eye = np.eye(FOURIER_GROUPS)
    dc = np.concatenate([cc, -sc], axis=1)
    chan = np.concatenate([np.kron(eye, cg), np.kron(eye, sg)], axis=1)
    flip = np.zeros((TILE, TILE))
    flip[np.arange(1, TILE), TILE - np.arange(1, TILE)] = 1.0
    alt = np.tile(np.where(np.arange(n_lat) % 2 == 0, 1.0, -1.0)[None, :], (8, 1))
    return [jnp.asarray(a, F32).astype(BF16) for a in (cn[:n_lat // 2], sn[:n_lat // 2], dc, chan, flip, alt)]


_SRC_CKV0, _SRC_KR0, _SRC_F0, _GATE0, _W_IN_COLS = 1152, 1408, 1440, 1952, 5024


def _split_kernel(wt_ref, wa_ref, wg_ref):
    d = wt_ref.shape[2]
    piece = 256

    def emit(out_ref, col0, row0, n):
        for c in range(0, n, piece):
            w = min(piece, n - c)
            out_ref[0, :, col0 + c:col0 + c + w] = wt_ref[0, row0 + c:row0 + c + w, :].T.astype(BF16)

    emit(wa_ref, _Q0, 0, _KR0)
    kr_rows = jnp.concatenate([jnp.zeros((64, d), F32), wt_ref[0, _SRC_KR0:_SRC_F0, :], jnp.zeros((32, d), F32)],
                              axis=0)
    wa_ref[0, :, _KR0:_CKV0] = kr_rows.T.astype(BF16)
    emit(wa_ref, _CKV0, _SRC_CKV0, _F0 - _CKV0)
    emit(wa_ref, _F0, _SRC_F0, _WA - _F0)
    emit(wg_ref, 0, _GATE0, _W_IN_COLS - _GATE0)


def _split_w_in(w_in):
    depth, d, cols = w_in.shape
    assert cols == _W_IN_COLS
    w_t = jnp.swapaxes(w_in, 1, 2)
    return pl.pallas_call(
        _split_kernel,
        grid=(depth,),
        in_specs=[pl.BlockSpec((1, cols, d), lambda l: (l, 0, 0), pipeline_mode=pl.Buffered(1))],
        out_specs=[pl.BlockSpec((1, d, _WA), lambda l: (l, 0, 0)),
                   pl.BlockSpec((1, d, cols - _GATE0), lambda l: (l, 0, 0))],
        out_shape=[jax.ShapeDtypeStruct((depth, d, _WA), BF16), jax.ShapeDtypeStruct((depth, d, cols - _GATE0), BF16)],
        compiler_params=_cparams(1),
        name="split_w_in",
    )(w_t)


def _inproj_weights(w_in, g_q_gqa, g_k_gqa, g_cq, g_ckv, w_uq, w_ukv, g_q_nope, g_k_nope, g_q_rope, g_k_rope):
    depth = w_in.shape[0]
    wa, gates = _split_w_in(w_in)
    zeros = lambda *s: jnp.zeros((depth,) + s, F32)
    uq = w_uq.reshape(depth, MLA_Q_RANK, MLA_HEADS, MLA_NOPE_DIM + MLA_ROPE_DIM)
    wuq = jnp.concatenate([uq, zeros(MLA_Q_RANK, MLA_HEADS, 32)], axis=-1).reshape(depth, MLA_Q_RANK, -1)
    ukv = w_ukv.reshape(depth, MLA_KV_RANK, MLA_HEADS, MLA_NOPE_DIM + MLA_V_DIM)
    kn, vv = ukv[..., :MLA_NOPE_DIM], ukv[..., MLA_NOPE_DIM:]
    z64 = zeros(MLA_KV_RANK, MLA_HEADS, 64)
    wkn = jnp.concatenate([kn, z64], axis=-1).reshape(depth, MLA_KV_RANK, -1)
    even = (jnp.arange(MLA_HEADS) % 2 == 0)[None, None, :, None]
    wv = jnp.where(even, jnp.concatenate([vv, z64], axis=-1), jnp.concatenate([z64, vv], axis=-1))
    wvt = jnp.swapaxes(wv.reshape(depth, MLA_KV_RANK, -1), 1, 2)
    gkn = jnp.concatenate([g_k_nope, zeros(64)], axis=-1)[:, None, :]
    z32, z64v = zeros(32), zeros(64)
    tgains = _table_gains([jnp.tile(g_q_gqa, (1, 2)), jnp.tile(g_k_gqa, (1, 2)),
                           jnp.concatenate([g_q_nope, g_q_rope, z32], axis=-1),
                           jnp.concatenate([z64v, g_k_rope, z32], axis=-1)],
                          [GQA_SCALE * LOG2E, 1.0, MLA_SCALE * LOG2E, 1.0])
    wts = [wa, wuq.astype(BF16), wkn.astype(BF16), wvt.astype(BF16), g_cq[:, None, :], g_ckv[:, None, :], gkn,
           tgains]
    return wts, gates


def kernel(x, c, ctx, c_ctx, w_mod, b_mod, g_norm1, g_norm2, w_in, g_q_gqa, g_k_gqa, g_cq, g_ckv, w_uq, w_ukv,
           g_q_nope, g_k_nope, g_q_rope, g_k_rope, b_gate, w_br_a, w_br_b, w_br_c, w_out, w_ffn_in, w_ffn_out):
    b, n, d = x.shape
    nc = ctx.shape[1]
    depth = w_mod.shape[0]
    assert n % TILE == 0 and nc == TILE and b + 1 <= MOD_ROWS and n % GRID_W == 0
    n_lat_tiles, n_tiles = n // TILE, (n + nc) // TILE

    cc = jnp.concatenate([c, c_ctx[None], jnp.zeros((MOD_ROWS - b - 1, d), F32)], axis=0)
    mod = _modulation(cc, w_mod, b_mod).reshape(depth, MOD_ROWS, 6, d)
    dft = _dft_tables(n, nc)
    segs = _seg_matrices()
    bounds = _score_bounds(g_q_gqa, g_k_gqa, g_q_nope, g_k_nope, g_q_rope, g_k_rope)
    tabs = _rope_tables(n, nc)
    wts, wg = _inproj_weights(w_in, g_q_gqa, g_k_gqa, g_cq, g_ckv, w_uq, w_ukv, g_q_nope, g_k_nope, g_q_rope,
                              g_k_rope)
    g1, g2 = g_norm1[:, None, :], g_norm2[:, None, :]
    mw = [wg, b_gate[:, None, :], w_br_a.astype(BF16), w_br_b.astype(BF16), w_br_c.astype(BF16), w_out.astype(BF16)]
    wi, wo = w_ffn_in.astype(BF16), w_ffn_out.astype(BF16)
    gqa_q = (0, 0, 128, 128, 256, 256, 384, 384)
    gqa_k = (0, 128, 0, 128, 256, 384, 256, 384)
    gqa_v = (0, 1, 0, 1, 2, 3, 2, 3)
    mla_o = tuple(range(0, 1024, 128))
    mla_h = tuple(range(MLA_HEADS))

    x_lat, x_ctx, ctx_tile = x, ctx, 0
    for l in range(depth):
        n_out = n_lat_tiles if l == depth - 1 else n_tiles
        qa, ka, vat, qb, kb, vbt, pf = _inproj(x_lat, x_ctx, ctx_tile, mod, l, n_lat_tiles, g1, wts, tabs, segs)
        ya = _attention(bounds[l, 0:1], qa, ka, vat, n_lat_tiles, n_out, gqa_q, gqa_k, gqa_v, "attn_gqa")
        yb = _attention(bounds[l, 1:2], qb, kb, vbt, n_lat_tiles, n_out, mla_o, mla_o, mla_h, "attn_mla")
        yc = _fourier(pf, dft, n_lat_tiles, n_out)
        xm = _merge(x_lat, x_ctx, ctx_tile, mod, l, n_lat_tiles, n_out, g1, ya, yb, yc, mw)
        xall = _ffn(xm, mod, l, n_lat_tiles, n_out, g2, wi, wo)
        x_lat, x_ctx, ctx_tile = xall, xall, n_lat_tiles
    return xall
```

```python
import functools
import math

import numpy as np
import jax
import jax.numpy as jnp
from jax import lax
from jax.experimental import pallas as pl
from jax.experimental.pallas import tpu as pltpu

GRID_W = 64
ROPE_THETA = 10000.0
EPS = 1e-6
GQA_HEADS = 8
GQA_KV_HEADS = 2
GQA_HEAD_DIM = 64
GQA_SCALE = GQA_HEAD_DIM ** -0.5
MLA_HEADS = 8
MLA_Q_RANK = 384
MLA_KV_RANK = 256
MLA_NOPE_DIM = 64
MLA_ROPE_DIM = 32
MLA_V_DIM = 64
MLA_SCALE = (MLA_NOPE_DIM + MLA_ROPE_DIM) ** -0.5
FOURIER_GROUPS = 4
FOURIER_GROUP_DIM = 128
FOURIER_WIDTH = FOURIER_GROUPS * FOURIER_GROUP_DIM
LOG2E = math.log2(math.e)

LANES = 128
TILE = 256
MOD_ROWS = 16
VMEM_LIMIT = 56 * 1024 * 1024

BF16 = jnp.bfloat16
F32 = jnp.float32

_ONES_ROW_EVEN, _ONES_ROW_ODD = LANES - 1, 0

_Q0, _K0, _V0, _CQ0, _KR0, _CKV0, _F0, _WA = 0, 512, 640, 768, 1152, 1280, 1536, 2048


def _cparams(n_axes):
    return pltpu.CompilerParams(dimension_semantics=("arbitrary",) * n_axes, vmem_limit_bytes=VMEM_LIMIT)


def _dot(a, b):
    return jnp.dot(a, b, preferred_element_type=F32)


def _rms_rows(xv, g):
    return xv * lax.rsqrt(jnp.mean(xv * xv, axis=-1, keepdims=True) + EPS) * g


def _modulated(x, g, shift, scale):
    r = lax.rsqrt(jnp.mean(x * x, axis=-1, keepdims=True) + EPS)
    return (x * r * (g * (1.0 + scale)) + shift).astype(BF16)


def _layer_spec(a, l):
    return pl.BlockSpec((1,) + a.shape[1:], lambda *_: (l,) + (0,) * (a.ndim - 1))


def _const_spec(a):
    return pl.BlockSpec(a.shape, lambda *_: (0,) * a.ndim)


def _sub_tiles(n_tiles):
    return next(s for s in (3, 2, 1) if n_tiles % s == 0)


def _row_specs(x_lat, x_ctx, ctx_tile, n_lat_tiles, sub):
    d = x_lat.shape[-1]
    lat = lambda j: pl.BlockSpec((1, TILE, d), lambda bi, si: (bi, jnp.minimum(si * sub + j, n_lat_tiles - 1), 0))
    return [lat(j) for j in range(sub)] + [pl.BlockSpec((1, TILE, d), lambda bi, si: (bi, ctx_tile, 0))]


def _mod_specs(mod, l, n_batch, n_lat_tiles, sub):
    spec = lambda j: pl.BlockSpec((1, 1) + mod.shape[2:],
                                  lambda bi, si: (l, jnp.where(si * sub + j >= n_lat_tiles, n_batch, bi), 0, 0))
    return [spec(j) for j in range(sub)]


def _tile_rows(xl_refs, xc_ref, j, n_lat_tiles):
    return jnp.where(pl.program_id(1) * len(xl_refs) + j < n_lat_tiles, xl_refs[j][0], xc_ref[0])


def _mod_kernel(cc_ref, w_ref, b_ref, o_ref):
    s = cc_ref[...]
    s = s * jax.nn.sigmoid(s)
    o_ref[0] = _dot(s.astype(BF16), w_ref[0].astype(BF16)) + b_ref[0]


def _modulation(cc, w_mod, b_mod):
    depth, d, d6 = w_mod.shape
    bn = 1536
    return pl.pallas_call(
        _mod_kernel,
        grid=(depth, d6 // bn),
        in_specs=[pl.BlockSpec((MOD_ROWS, d), lambda l, j: (0, 0)),
                  pl.BlockSpec((1, d, bn), lambda l, j: (l, 0, j)),
                  pl.BlockSpec((1, 1, bn), lambda l, j: (l, 0, j))],
        out_specs=pl.BlockSpec((1, MOD_ROWS, bn), lambda l, j: (l, 0, j)),
        out_shape=jax.ShapeDtypeStruct((depth, MOD_ROWS, d6), F32),
        compiler_params=_cparams(2),
        name="modulation",
    )(cc, w_mod, b_mod.reshape(depth, 1, d6))


def _rot_half(xv, lane, seg):
    first = (lane & (2 * seg - 1)) < seg
    return jnp.where(first, pltpu.roll(xv, LANES - seg, 1), pltpu.roll(xv, seg, 1))


def _seg_rsqrt(xw, seg_ref):
    return lax.rsqrt(_dot((xw * xw).astype(BF16), seg_ref[...]) + EPS)


def _inproj_kernel(*refs, n_lat_tiles, sub):
    xl_refs, refs = refs[:sub], refs[sub:]
    xc_ref, refs = refs[0], refs[1:]
    mod_refs, refs = refs[:sub], refs[sub:]
    (g1_ref, wa_ref, wuq_ref, wkn_ref, wvt_ref, gcq_ref, gckv_ref, gkn_ref, tg_ref, tab_ref,
     sgqa_ref, sqb_ref, skn_ref, skk_ref, qa_ref, ka_ref, vat_ref, qb_ref, kb_ref, vbt_ref, pf_ref) = refs
    lane = lax.broadcasted_iota(jnp.int32, (1, LANES), 1)
    lo = lane < 64
    hi = lane >= 64
    row = lax.broadcasted_iota(jnp.int32, (LANES, TILE), 0)
    blk = lambda a, j: a[:, j * LANES:(j + 1) * LANES]
    n_qblk = GQA_HEADS * GQA_HEAD_DIM // LANES
    w2 = 2 * LANES

    def project(j):
        rows = slice(j * TILE, (j + 1) * TILE)
        m = mod_refs[j][0, 0]
        hb = _modulated(_tile_rows(xl_refs, xc_ref, j, n_lat_tiles), g1_ref[0], m[0:1], m[1:2])
        p_c = _dot(hb, wa_ref[0, :, _CQ0:_CKV0])
        p_ckv = _dot(hb, wa_ref[0, :, _CKV0:_F0])
        pq = _dot(hb, wa_ref[0, :, _Q0:_K0])
        pkv = _dot(hb, wa_ref[0, :, _K0:_CQ0])
        pf_ref[0, rows] = _dot(hb, wa_ref[0, :, _F0:_WA]).astype(BF16)
        cq = _rms_rows(p_c[:, :MLA_Q_RANK], gcq_ref[0]).astype(BF16)
        ckv = _rms_rows(p_ckv, gckv_ref[0])
        qb = _dot(cq, wuq_ref[0])
        kn_all = _dot(ckv.astype(BF16), wkn_ref[0])
        vbt = _dot(wvt_ref[0], ckv.T.astype(BF16))
        for h in range(MLA_HEADS):
            vh = vbt[h * LANES:(h + 1) * LANES]
            ones_row = _ONES_ROW_EVEN if h % 2 == 0 else _ONES_ROW_ODD
            vbt_ref[0, h, :, rows] = jnp.where(row == ones_row, 1.0, vh).astype(BF16)
        vt = pkv[:, LANES:].T
        for kvh in range(GQA_KV_HEADS):
            own = (row >= 64) == (kvh == 1)
            top = jnp.where(own, vt, 0.0) if kvh == 0 else pltpu.roll(jnp.where(own, vt, 0.0), 64, 0)
            bot = pltpu.roll(top, 64, 0)
            vat_ref[0, 2 * kvh, :, rows] = jnp.where(row == _ONES_ROW_EVEN, 1.0, top).astype(BF16)
            vat_ref[0, 2 * kvh + 1, :, rows] = jnp.where(row == _ONES_ROW_ODD, 1.0, bot).astype(BF16)
        return pq, pkv[:, :LANES], qb, kn_all, p_c[:, MLA_Q_RANK:]

    def finish(j, pq, xk, qb, kn_all, xkr):
        rows = slice(j * TILE, (j + 1) * TILE)
        r_qb = jnp.concatenate([_seg_rsqrt(qb[:, c * w2:(c + 1) * w2], sqb_ref) for c in range(MLA_HEADS // 2)],
                               axis=1)
        r_kn = jnp.concatenate([_seg_rsqrt(kn_all[:, c * w2:(c + 1) * w2], skn_ref) for c in range(MLA_HEADS // 2)],
                               axis=1)
        r_q = jnp.concatenate([_seg_rsqrt(pq[:, c * w2:(c + 1) * w2], sgqa_ref) for c in range(n_qblk // 2)], axis=1)
        r_kk = _seg_rsqrt(jnp.concatenate([xk, xkr], axis=1), skk_ref)
        r_k, r_kr = r_kk[:, :LANES], r_kk[:, LANES:]

        tab = [tab_ref[i, rows] * tg_ref[0, i:i + 1] for i in range(2 * N_TABLES)]

        def roped(xb, table, seg, r):
            return (xb * tab[table] + _rot_half(xb, lane, seg) * tab[N_TABLES + table]) * r

        for h in range(MLA_HEADS):
            qb_ref[0, rows, h * LANES:(h + 1) * LANES] = roped(blk(qb, h), TAB_QB, 8, blk(r_qb, h)).astype(BF16)
        kr = roped(xkr, TAB_KR, 8, r_kr)
        for h in range(MLA_HEADS):
            kb_ref[0, rows, h * LANES:(h + 1) * LANES] = (blk(kn_all, h) * blk(r_kn, h) * gkn_ref[0] + kr).astype(BF16)
        for c in range(n_qblk):
            qa_ref[0, rows, c * LANES:(c + 1) * LANES] = roped(blk(pq, c), TAB_Q, 16, blk(r_q, c)).astype(BF16)
        kn = roped(xk, TAB_K, 16, r_k)
        sw = pltpu.roll(kn, 64, 1)
        for c, f in enumerate((jnp.where(lo, kn, 0.0), jnp.where(hi, sw, 0.0), jnp.where(lo, sw, 0.0),
                               jnp.where(hi, kn, 0.0))):
            ka_ref[0, rows, c * LANES:(c + 1) * LANES] = f.astype(BF16)

    pending = project(0)
    for j in range(sub):
        nxt = project(j + 1) if j + 1 < sub else None
        finish(j, *pending)
        pending = nxt


def _inproj(x_lat, x_ctx, ctx_tile, mod, l, n_lat_tiles, g1, wts, tabs, segs):
    b = x_lat.shape[0]
    nt = n_lat_tiles + 1
    sub = _sub_tiles(nt)
    t, step_rows = nt * TILE, sub * TILE
    row = lambda bi, si: (bi, si, 0)
    rowt = lambda bi, si: (bi, 0, 0, si)
    tok = lambda w: (pl.BlockSpec((1, step_rows, w), row), jax.ShapeDtypeStruct((b, t, w), BF16))
    tra = lambda h: (pl.BlockSpec((1, h, LANES, step_rows), rowt), jax.ShapeDtypeStruct((b, h, LANES, t), BF16))
    outs = [tok(512), tok(512), tra(4), tok(1024), tok(1024), tra(MLA_HEADS), tok(512)]
    return pl.pallas_call(
        functools.partial(_inproj_kernel, n_lat_tiles=n_lat_tiles, sub=sub),
        grid=(b, nt // sub),
        in_specs=_row_specs(x_lat, x_ctx, ctx_tile, n_lat_tiles, sub) + _mod_specs(mod, l, b, n_lat_tiles, sub)
        + [_layer_spec(g1, l)] + [_layer_spec(w, l) for w in wts]
        + [pl.BlockSpec((2 * N_TABLES, step_rows, LANES), lambda bi, si: (0, si, 0))]
        + [_const_spec(s) for s in segs],
        out_specs=[o[0] for o in outs],
        out_shape=[o[1] for o in outs],
        compiler_params=_cparams(2),
        name="inproj",
    )(*([x_lat] * sub), x_ctx, *([mod] * sub), g1, *wts, tabs, *segs)


def _reduce_keys(a, op):
    nk, nq = a.shape
    part = op(a.reshape(nk // TILE, TILE, nq), axis=0)
    return op(part, axis=0, keepdims=True)


SCORE_SAFE = 60.0
BOUND_SLACK = 1.0 + 2.0 ** -6


def _attn_kernel(bound_ref, q_ref, k_ref, vt_ref, o_ref, *, n_lat_tiles, with_ctx, qoff, koff, vidx):
    n_keys = k_ref.shape[1]
    n_pairs = len(qoff) // 2
    zero = jnp.zeros((TILE, LANES), BF16)
    bound = bound_ref[0]

    def scores(row0, pair, key0, nk):
        e, o = 2 * pair, 2 * pair + 1
        assert koff[o] == koff[e] + LANES
        q_e = q_ref[0, pl.ds(row0, TILE), qoff[e]:qoff[e] + LANES]
        q_o = q_ref[0, pl.ds(row0, TILE), qoff[o]:qoff[o] + LANES]
        qd = jnp.concatenate([jnp.concatenate([q_e, zero], axis=1), jnp.concatenate([zero, q_o], axis=1)], axis=0)
        k2 = k_ref[0, key0:key0 + nk, koff[e]:koff[e] + 2 * LANES]
        return lax.dot_general(k2, qd, (((1,), (1,)), ((), ())), preferred_element_type=F32)

    def tile(row0, key0, nk, bounded):
        for pair in range(n_pairs):
            e, o = 2 * pair, 2 * pair + 1
            st = scores(row0, pair, key0, nk)
            pb = jnp.exp2(st if bounded else st - _reduce_keys(st, jnp.max)).astype(BF16)
            ot_e = _dot(vt_ref[0, vidx[e], :, key0:key0 + nk], pb[:, :TILE])
            ot_o = _dot(vt_ref[0, vidx[o], :, key0:key0 + nk], pb[:, TILE:])
            ot_e = ot_e * (1.0 / ot_e[_ONES_ROW_EVEN:_ONES_ROW_EVEN + 1])
            ot_o = ot_o * (1.0 / ot_o[_ONES_ROW_ODD:_ONES_ROW_ODD + 1])
            ot = jnp.where(lax.broadcasted_iota(jnp.int32, (LANES, TILE), 0) < 64, ot_e, ot_o)
            o_ref[0, pl.ds(row0, TILE), pair * LANES:(pair + 1) * LANES] = ot.T.astype(BF16)

    def all_tiles(bounded):
        def body(t, carry):
            tile(pl.multiple_of(t * TILE, TILE), 0, n_keys, bounded)
            return carry
        lax.fori_loop(0, n_lat_tiles, body, 0, unroll=2 if bounded else 1)
        if with_ctx:
            lat = n_lat_tiles * TILE
            tile(lat, lat, n_keys - lat, bounded)

    lax.cond(bound < SCORE_SAFE, lambda: all_tiles(True), lambda: all_tiles(False))


def _attention(shift, q, k, vt, n_lat_tiles, n_q_tiles, qoff, koff, vidx, name):
    b, t, wq = q.shape
    wk = k.shape[2]
    nh = vt.shape[1]
    kern = functools.partial(_attn_kernel, n_lat_tiles=n_lat_tiles, with_ctx=n_q_tiles > n_lat_tiles,
                             qoff=qoff, koff=koff, vidx=vidx)
    return pl.pallas_call(
        kern,
        grid=(b,),
        in_specs=[pl.BlockSpec(memory_space=pltpu.SMEM),
                  pl.BlockSpec((1, t, wq), lambda bi: (bi, 0, 0)),
                  pl.BlockSpec((1, t, wk), lambda bi: (bi, 0, 0)),
                  pl.BlockSpec((1, nh, LANES, t), lambda bi: (bi, 0, 0, 0))],
        out_specs=pl.BlockSpec((1, n_q_tiles * TILE, 512), lambda bi: (bi, 0, 0)),
        out_shape=jax.ShapeDtypeStruct((b, n_q_tiles * TILE, 512), BF16),
        compiler_params=_cparams(1),
        name=name,
    )(shift, q, k, vt)


def _fourier_kernel(pf_ref, cn_ref, sn_ref, dc_ref, cs_ref, flip_ref, alt_ref, o_ref, a_ref, b_ref, *, n_lat_tiles,
                    with_ctx):
    n_lat = n_lat_tiles * TILE
    half_tiles = n_lat_tiles // 2
    w = FOURIER_WIDTH
    for c in range(n_lat_tiles):
        ab = _dot(pf_ref[0, c * TILE:(c + 1) * TILE, :], cs_ref[...])
        a_ref[c * TILE:(c + 1) * TILE, :] = ab[:, :w].astype(BF16)
        b_ref[c * TILE:(c + 1) * TILE, :] = ab[:, w:].astype(BF16)
    row0 = lax.broadcasted_iota(jnp.int32, (TILE, w), 0) == 0
    carry = _dot(alt_ref[...], a_ref[...])[0:1] * (1.0 / math.sqrt(n_lat))
    for m in reversed(range(half_tiles)):
        g = _dot(cn_ref[m * TILE:(m + 1) * TILE, :], a_ref[...])
        h = _dot(sn_ref[m * TILE:(m + 1) * TILE, :], b_ref[...])
        o_ref[0, m * TILE:(m + 1) * TILE, :] = (g - h).astype(BF16)
        mirrored = (g + h).astype(BF16)
        tile_rev = _dot(flip_ref[...], mirrored)
        bot = n_lat_tiles - 1 - m
        o_ref[0, bot * TILE:(bot + 1) * TILE, :] = jnp.where(row0, carry, tile_rev).astype(BF16)
        carry = mirrored[0:1].astype(F32)
    if with_ctx:
        ab = _dot(pf_ref[0, n_lat:n_lat + TILE, :], cs_ref[...])
        abc = jnp.concatenate([ab[:, :w], ab[:, w:]], axis=0).astype(BF16)
        o_ref[0, n_lat:n_lat + TILE, :] = _dot(dc_ref[...], abc).astype(BF16)


def _fourier(pf, tables, n_lat_tiles, n_tiles):
    b, t, w = pf.shape
    n_lat = n_lat_tiles * TILE
    kern = functools.partial(_fourier_kernel, n_lat_tiles=n_lat_tiles, with_ctx=n_tiles > n_lat_tiles)
    return pl.pallas_call(
        kern,
        grid=(b,),
        in_specs=[pl.BlockSpec((1, t, w), lambda bi: (bi, 0, 0))] + [_const_spec(a) for a in tables],
        out_specs=pl.BlockSpec((1, n_tiles * TILE, w), lambda bi: (bi, 0, 0)),
        out_shape=jax.ShapeDtypeStruct((b, n_tiles * TILE, w), BF16),
        scratch_shapes=[pltpu.VMEM((n_lat, w), BF16), pltpu.VMEM((n_lat, w), BF16)],
        compiler_params=_cparams(1),
        name="fourier",
    )(pf, *tables)


def _merge_kernel(*refs, n_lat_tiles, sub):
    xl_refs, refs = refs[:sub], refs[sub:]
    xc_ref, refs = refs[0], refs[1:]
    mod_refs, refs = refs[:sub], refs[sub:]
    g1_ref, ya_ref, yb_ref, yc_ref, wg_ref, bg_ref, wa_ref, wb_ref, wc_ref, wo_ref, o_ref = refs
    for j in range(sub):
        rows = slice(j * TILE, (j + 1) * TILE)
        x = _tile_rows(xl_refs, xc_ref, j, n_lat_tiles)
        m = mod_refs[j][0, 0]
        d = x.shape[-1]
        hb = _modulated(x, g1_ref[0], m[0:1], m[1:2])
        acc = None
        for i, (y_ref, w_ref) in enumerate(((ya_ref, wa_ref), (yb_ref, wb_ref), (yc_ref, wc_ref))):
            gate = jax.nn.sigmoid(_dot(hb, wg_ref[0, :, i * d:(i + 1) * d]) + bg_ref[0, :, i * d:(i + 1) * d])
            term = gate * _dot(y_ref[0, rows], w_ref[0])
            acc = term if acc is None else acc + term
        o_ref[0, rows] = x + m[2:3] * _dot(acc.astype(BF16), wo_ref[0])


def _merge(x_lat, x_ctx, ctx_tile, mod, l, n_lat_tiles, n_tiles, g1, ya, yb, yc, wts):
    b, _, d = x_lat.shape
    sub = _sub_tiles(n_tiles)
    row = lambda bi, si: (bi, si, 0)
    return pl.pallas_call(
        functools.partial(_merge_kernel, n_lat_tiles=n_lat_tiles, sub=sub),
        grid=(b, n_tiles // sub),
        in_specs=_row_specs(x_lat, x_ctx, ctx_tile, n_lat_tiles, sub) + _mod_specs(mod, l, b, n_lat_tiles, sub)
        + [_layer_spec(g1, l)] + [pl.BlockSpec((1, sub * TILE, 512), row)] * 3 + [_layer_spec(w, l) for w in wts],
        out_specs=pl.BlockSpec((1, sub * TILE, d), row),
        out_shape=jax.ShapeDtypeStruct((b, n_tiles * TILE, d), F32),
        compiler_params=_cparams(2),
        name="merge",
    )(*([x_lat] * sub), x_ctx, *([mod] * sub), g1, ya, yb, yc, *wts)


def _ffn_kernel(*refs, sub):
    x_ref, refs = refs[0], refs[1:]
    mod_refs, refs = refs[:sub], refs[sub:]
    g2_ref, wi_ref, wo_ref, o_ref = refs
    for j in range(sub):
        rows = slice(j * TILE, (j + 1) * TILE)
        x = x_ref[0, rows]
        m = mod_refs[j][0, 0]
        hb = _modulated(x, g2_ref[0], m[3:4], m[4:5])
        gu = _dot(hb, wi_ref[0])
        hid = gu.shape[-1] // 2
        gate, up = gu[:, :hid], gu[:, hid:]
        act = (gate * jax.nn.sigmoid(gate) * up).astype(BF16)
        o_ref[0, rows] = x + m[5:6] * _dot(act, wo_ref[0])


def _ffn(xm, mod, l, n_lat_tiles, n_tiles, g2, wi, wo):
    b, _, d = xm.shape
    sub = _sub_tiles(n_tiles)
    row = lambda bi, si: (bi, si, 0)
    once = lambda a: pl.BlockSpec((1,) + a.shape[1:], lambda *_: (l,) + (0,) * (a.ndim - 1),
                                  pipeline_mode=pl.Buffered(1))
    return pl.pallas_call(
        functools.partial(_ffn_kernel, sub=sub),
        grid=(b, n_tiles // sub),
        in_specs=[pl.BlockSpec((1, sub * TILE, d), row)] + _mod_specs(mod, l, b, n_lat_tiles, sub)
        + [_layer_spec(g2, l), once(wi), once(wo)],
        out_specs=pl.BlockSpec((1, sub * TILE, d), row),
        out_shape=jax.ShapeDtypeStruct((b, n_tiles * TILE, d), F32),
        compiler_params=_cparams(2),
        name="ffn",
    )(xm, *([mod] * sub), g2, wi, wo)


def _partner(seg):
    lane = np.arange(LANES)
    return np.where((lane % (2 * seg)) < seg, lane + seg, lane - seg)


def _rope_tables(n_lat, n_ctx):
    rows = n_lat // GRID_W
    row_id = np.repeat(np.arange(rows), GRID_W).astype(np.float64)
    col_id = np.tile(np.arange(GRID_W), rows).astype(np.float64)

    def angles(dim):
        half = dim // 2
        freqs = ROPE_THETA ** (-np.arange(0, half, 2, dtype=np.float64) / half)
        ax = lambda pos: np.concatenate([pos[:, None] * freqs[None, :]] * 2, axis=-1)
        return np.concatenate([ax(row_id), ax(col_id)], axis=-1)

    def signed(sin, dim):
        sign = np.where((np.arange(dim) % (dim // 2)) < dim // 4, -1.0, 1.0)
        return sin * sign[None, :]

    a64, a32 = angles(GQA_HEAD_DIM), angles(MLA_ROPE_DIM)
    t = n_lat + n_ctx
    cosa, sina = np.ones((t, LANES)), np.zeros((t, LANES))
    cosa[:n_lat] = np.tile(np.cos(a64), (1, 2))
    sina[:n_lat] = np.tile(signed(np.sin(a64), GQA_HEAD_DIM), (1, 2))
    cosb, sinb = np.ones((t, LANES)), np.zeros((t, LANES))
    cosb[:n_lat, 64:96] = np.cos(a32)
    sinb[:n_lat, 64:96] = signed(np.sin(a32), MLA_ROPE_DIM)
    return jnp.asarray(np.stack([cosa, cosa, cosb, cosb, sina, sina, sinb, sinb]), F32)


TAB_Q, TAB_K, TAB_QB, TAB_KR, N_TABLES = 0, 1, 2, 3, 4


def _table_gains(gains, scales):
    g = jnp.stack(gains, axis=1) * jnp.asarray(scales, F32)[None, :, None]
    partner = np.stack([_partner(16), _partner(16), _partner(8), _partner(8)])
    return jnp.concatenate([g, jnp.take_along_axis(g, jnp.asarray(partner)[None], axis=-1)], axis=1)


def _seg_matrices():
    def blockdiag(segs):
        m = np.zeros((2 * LANES, 2 * LANES))
        for start, n in segs:
            m[start:start + n, start:start + n] = 1.0 / n
        return m

    gqa = blockdiag([(s0, 64) for s0 in range(0, 256, 64)])
    qb = blockdiag([(0, 64), (64, 32), (128, 64), (192, 32)])
    kn = blockdiag([(0, 64), (128, 64)])
    kk = blockdiag([(0, 64), (64, 64), (192, 32)])
    return [jnp.asarray(a, F32).astype(BF16) for a in (gqa, qb, kn, kk)]


def _score_bounds(g_q_gqa, g_k_gqa, g_q_nope, g_k_nope, g_q_rope, g_k_rope):
    top = lambda g: jnp.max(g * g, axis=-1)
    gqa = jnp.sqrt(GQA_HEAD_DIM * top(g_q_gqa) * GQA_HEAD_DIM * top(g_k_gqa)) * (GQA_SCALE * LOG2E)
    mla = jnp.sqrt((MLA_NOPE_DIM * top(g_q_nope) + MLA_ROPE_DIM * top(g_q_rope))
                   * (MLA_NOPE_DIM * top(g_k_nope) + MLA_ROPE_DIM * top(g_k_rope))) * (MLA_SCALE * LOG2E)
    return jnp.stack([gqa, mla], axis=-1) * BOUND_SLACK


def _dft_tables(n_lat, n_ctx):
    def cs(n):
        j = np.arange(n)
        ang = 2.0 * np.pi * ((j[:, None] * j[None, :]) % n) / n
        return np.cos(ang) / np.sqrt(n), np.sin(ang) / np.sqrt(n)

    cn, sn = cs(n_lat)
    cc, sc = cs(n_ctx)
    cg, sg = cs(FOURIER_GROUP_DIM)
    eye = np.eye(FOURIER_GROUPS)
    dc = np.concatenate([cc, -sc], axis=1)
    chan = np.concatenate([np.kron(eye, cg), np.kron(eye, sg)], axis=1)
    flip = np.zeros((TILE, TILE))
    flip[np.arange(1, TILE), TILE - np.arange(1, TILE)] = 1.0
    alt = np.tile(np.where(np.arange(n_lat) % 2 == 0, 1.0, -1.0)[None, :], (8, 1))
    return [jnp.asarray(a, F32).astype(BF16) for a in (cn[:n_lat // 2], sn[:n_lat // 2], dc, chan, flip, alt)]


_SRC_CKV0, _SRC_KR0, _SRC_F0, _GATE0, _W_IN_COLS = 1152, 1408, 1440, 1952, 5024


def _split_kernel(wt_ref, wa_ref, wg_ref):
    d = wt_ref.shape[2]
    piece = 256

    def emit(out_ref, col0, row0, n):
        for c in range(0, n, piece):
            w = min(piece, n - c)
            out_ref[0, :, col0 + c:col0 + c + w] = wt_ref[0, row0 + c:row0 + c + w, :].T.astype(BF16)

    emit(wa_ref, _Q0, 0, _KR0)
    kr_rows = jnp.concatenate([jnp.zeros((64, d), F32), wt_ref[0, _SRC_KR0:_SRC_F0, :], jnp.zeros((32, d), F32)],
                              axis=0)
    wa_ref[0, :, _KR0:_CKV0] = kr_rows.T.astype(BF16)
    emit(wa_ref, _CKV0, _SRC_CKV0, _F0 - _CKV0)
    emit(wa_ref, _F0, _SRC_F0, _WA - _F0)
    emit(wg_ref, 0, _GATE0, _W_IN_COLS - _GATE0)


def _split_w_in(w_in):
    depth, d, cols = w_in.shape
    assert cols == _W_IN_COLS
    w_t = jnp.swapaxes(w_in, 1, 2)
    return pl.pallas_call(
        _split_kernel,
        grid=(depth,),
        in_specs=[pl.BlockSpec((1, cols, d), lambda l: (l, 0, 0), pipeline_mode=pl.Buffered(1))],
        out_specs=[pl.BlockSpec((1, d, _WA), lambda l: (l, 0, 0)),
                   pl.BlockSpec((1, d, cols - _GATE0), lambda l: (l, 0, 0))],
        out_shape=[jax.ShapeDtypeStruct((depth, d, _WA), BF16), jax.ShapeDtypeStruct((depth, d, cols - _GATE0), BF16)],
        compiler_params=_cparams(1),
        name="split_w_in",
    )(w_t)


def _inproj_weights(w_in, g_q_gqa, g_k_gqa, g_cq, g_ckv, w_uq, w_ukv, g_q_nope, g_k_nope, g_q_rope, g_k_rope):
    depth = w_in.shape[0]
    wa, gates = _split_w_in(w_in)
    zeros = lambda *s: jnp.zeros((depth,) + s, F32)
    uq = w_uq.reshape(depth, MLA_Q_RANK, MLA_HEADS, MLA_NOPE_DIM + MLA_ROPE_DIM)
    wuq = jnp.concatenate([uq, zeros(MLA_Q_RANK, MLA_HEADS, 32)], axis=-1).reshape(depth, MLA_Q_RANK, -1)
    ukv = w_ukv.reshape(depth, MLA_KV_RANK, MLA_HEADS, MLA_NOPE_DIM + MLA_V_DIM)
    kn, vv = ukv[..., :MLA_NOPE_DIM], ukv[..., MLA_NOPE_DIM:]
    z64 = zeros(MLA_KV_RANK, MLA_HEADS, 64)
    wkn = jnp.concatenate([kn, z64], axis=-1).reshape(depth, MLA_KV_RANK, -1)
    even = (jnp.arange(MLA_HEADS) % 2 == 0)[None, None, :, None]
    wv = jnp.where(even, jnp.concatenate([vv, z64], axis=-1), jnp.concatenate([z64, vv], axis=-1))
    wvt = jnp.swapaxes(wv.reshape(depth, MLA_KV_RANK, -1), 1, 2)
    gkn = jnp.concatenate([g_k_nope, zeros(64)], axis=-1)[:, None, :]
    z32, z64v = zeros(32), zeros(64)
    tgains = _table_gains([jnp.tile(g_q_gqa, (1, 2)), jnp.tile(g_k_gqa, (1, 2)),
                           jnp.concatenate([g_q_nope, g_q_rope, z32], axis=-1),
                           jnp.concatenate([z64v, g_k_rope, z32], axis=-1)],
                          [GQA_SCALE * LOG2E, 1.0, MLA_SCALE * LOG2E, 1.0])
    wts = [wa, wuq.astype(BF16), wkn.astype(BF16), wvt.astype(BF16), g_cq[:, None, :], g_ckv[:, None, :], gkn,
           tgains]
    return wts, gates


def kernel(x, c, ctx, c_ctx, w_mod, b_mod, g_norm1, g_norm2, w_in, g_q_gqa, g_k_gqa, g_cq, g_ckv, w_uq, w_ukv,
           g_q_nope, g_k_nope, g_q_rope, g_k_rope, b_gate, w_br_a, w_br_b, w_br_c, w_out, w_ffn_in, w_ffn_out):
    b, n, d = x.shape
    nc = ctx.shape[1]
    depth = w_mod.shape[0]
    assert n % TILE == 0 and nc == TILE and b + 1 <= MOD_ROWS and n % GRID_W == 0
    n_lat_tiles, n_tiles = n // TILE, (n + nc) // TILE

    cc = jnp.concatenate([c, c_ctx[None], jnp.zeros((MOD_ROWS - b - 1, d), F32)], axis=0)
    mod = _modulation(cc, w_mod, b_mod).reshape(depth, MOD_ROWS, 6, d)
    dft = _dft_tables(n, nc)
    segs = _seg_matrices()
    bounds = _score_bounds(g_q_gqa, g_k_gqa, g_q_nope, g_k_nope, g_q_rope, g_k_rope)
    tabs = _rope_tables(n, nc)
    wts, wg = _inproj_weights(w_in, g_q_gqa, g_k_gqa, g_cq, g_ckv, w_uq, w_ukv, g_q_nope, g_k_nope, g_q_rope,
                              g_k_rope)
    g1, g2 = g_norm1[:, None, :], g_norm2[:, None, :]
    mw = [wg, b_gate[:, None, :], w_br_a.astype(BF16), w_br_b.astype(BF16), w_br_c.astype(BF16), w_out.astype(BF16)]
    wi, wo = w_ffn_in.astype(BF16), w_ffn_out.astype(BF16)
    gqa_q = (0, 0, 128, 128, 256, 256, 384, 384)
    gqa_k = (0, 128, 0, 128, 256, 384, 256, 384)
    gqa_v = (0, 1, 0, 1, 2, 3, 2, 3)
    mla_o = tuple(range(0, 1024, 128))
    mla_h = tuple(range(MLA_HEADS))

    x_lat, x_ctx, ctx_tile = x, ctx, 0
    for l in range(depth):
        n_out = n_lat_tiles if l == depth - 1 else n_tiles
        qa, ka, vat, qb, kb, vbt, pf = _inproj(x_lat, x_ctx, ctx_tile, mod, l, n_lat_tiles, g1, wts, tabs, segs)
        ya = _attention(bounds[l, 0:1], qa, ka, vat, n_lat_tiles, n_out, gqa_q, gqa_k, gqa_v, "attn_gqa")
        yb = _attention(bounds[l, 1:2], qb, kb, vbt, n_lat_tiles, n_out, mla_o, mla_o, mla_h, "attn_mla")
        yc = _fourier(pf, dft, n_lat_tiles, n_out)
        xm = _merge(x_lat, x_ctx, ctx_tile, mod, l, n_lat_tiles, n_out, g1, ya, yb, yc, mw)
        xall = _ffn(xm, mod, l, n_lat_tiles, n_out, g2, wi, wo)
        x_lat, x_ctx, ctx_tile = xall, xall, n_lat_tiles
    return xall
```

```python
import functools
import math

import numpy as np
import jax
import jax.numpy as jnp
from jax import lax
from jax.experimental import pallas as pl
from jax.experimental.pallas import tpu as pltpu

GRID_W = 64
ROPE_THETA = 10000.0
EPS = 1e-6
GQA_HEADS = 8
GQA_KV_HEADS = 2
GQA_HEAD_DIM = 64
GQA_SCALE = GQA_HEAD_DIM ** -0.5
MLA_HEADS = 8
MLA_Q_RANK = 384
MLA_KV_RANK = 256
MLA_NOPE_DIM = 64
MLA_ROPE_DIM = 32
MLA_V_DIM = 64
MLA_SCALE = (MLA_NOPE_DIM + MLA_ROPE_DIM) ** -0.5
FOURIER_GROUPS = 4
FOURIER_GROUP_DIM = 128
FOURIER_WIDTH = FOURIER_GROUPS * FOURIER_GROUP_DIM
LOG2E = math.log2(math.e)

LANES = 128
TILE = 256
MOD_ROWS = 16
VMEM_LIMIT = 56 * 1024 * 1024

BF16 = jnp.bfloat16
F32 = jnp.float32

_ONES_ROW_EVEN, _ONES_ROW_ODD = LANES - 1, 0

_Q0, _K0, _V0, _CQ0, _KR0, _CKV0, _F0, _WA = 0, 512, 640, 768, 1152, 1280, 1536, 2048


def _cparams(n_axes):
    return pltpu.CompilerParams(dimension_semantics=("arbitrary",) * n_axes, vmem_limit_bytes=VMEM_LIMIT)


def _dot(a, b):
    return jnp.dot(a, b, preferred_element_type=F32)


def _rms_rows(xv, g):
    return xv * lax.rsqrt(jnp.mean(xv * xv, axis=-1, keepdims=True) + EPS) * g


def _modulated(x, g, shift, scale):
    r = lax.rsqrt(jnp.mean(x * x, axis=-1, keepdims=True) + EPS)
    return (x * r * (g * (1.0 + scale)) + shift).astype(BF16)


def _layer_spec(a, l):
    return pl.BlockSpec((1,) + a.shape[1:], lambda *_: (l,) + (0,) * (a.ndim - 1))


def _const_spec(a):
    return pl.BlockSpec(a.shape, lambda *_: (0,) * a.ndim)


def _sub_tiles(n_tiles):
    return next(s for s in (3, 2, 1) if n_tiles % s == 0)


def _row_specs(x_lat, x_ctx, ctx_tile, n_lat_tiles, sub):
    d = x_lat.shape[-1]
    lat = lambda j: pl.BlockSpec((1, TILE, d), lambda bi, si: (bi, jnp.minimum(si * sub + j, n_lat_tiles - 1), 0))
    return [lat(j) for j in range(sub)] + [pl.BlockSpec((1, TILE, d), lambda bi, si: (bi, ctx_tile, 0))]


def _mod_specs(mod, l, n_batch, n_lat_tiles, sub):
    spec = lambda j: pl.BlockSpec((1, 1) + mod.shape[2:],
                                  lambda bi, si: (l, jnp.where(si * sub + j >= n_lat_tiles, n_batch, bi), 0, 0))
    return [spec(j) for j in range(sub)]


def _tile_rows(xl_refs, xc_ref, j, n_lat_tiles):
    return jnp.where(pl.program_id(1) * len(xl_refs) + j < n_lat_tiles, xl_refs[j][0], xc_ref[0])


def _mod_kernel(cc_ref, w_ref, b_ref, o_ref):
    s = cc_ref[...]
    s = s * jax.nn.sigmoid(s)
    o_ref[0] = _dot(s.astype(BF16), w_ref[0].astype(BF16)) + b_ref[0]


def _modulation(cc, w_mod, b_mod):
    depth, d, d6 = w_mod.shape
    bn = 1536
    return pl.pallas_call(
        _mod_kernel,
        grid=(depth, d6 // bn),
        in_specs=[pl.BlockSpec((MOD_ROWS, d), lambda l, j: (0, 0)),
                  pl.BlockSpec((1, d, bn), lambda l, j: (l, 0, j)),
                  pl.BlockSpec((1, 1, bn), lambda l, j: (l, 0, j))],
        out_specs=pl.BlockSpec((1, MOD_ROWS, bn), lambda l, j: (l, 0, j)),
        out_shape=jax.ShapeDtypeStruct((depth, MOD_ROWS, d6), F32),
        compiler_params=_cparams(2),
        name="modulation",
    )(cc, w_mod, b_mod.reshape(depth, 1, d6))


def _rot_half(xv, lane, seg):
    first = (lane & (2 * seg - 1)) < seg
    return jnp.where(first, pltpu.roll(xv, LANES - seg, 1), pltpu.roll(xv, seg, 1))


def _seg_rsqrt(xw, seg_ref):
    return lax.rsqrt(_dot((xw * xw).astype(BF16), seg_ref[...]) + EPS)


def _inproj_kernel(*refs, n_lat_tiles, sub):
    xl_refs, refs = refs[:sub], refs[sub:]
    xc_ref, refs = refs[0], refs[1:]
    mod_refs, refs = refs[:sub], refs[sub:]
    (g1_ref, wa_ref, wuq_ref, wkn_ref, wvt_ref, gcq_ref, gckv_ref, gkn_ref, tg_ref, tab_ref,
     sgqa_ref, sqb_ref, skn_ref, skk_ref, qa_ref, ka_ref, vat_ref, qb_ref, kb_ref, vbt_ref, pf_ref) = refs
    lane = lax.broadcasted_iota(jnp.int32, (1, LANES), 1)
    lo = lane < 64
    hi = lane >= 64
    row = lax.broadcasted_iota(jnp.int32, (LANES, TILE), 0)
    blk = lambda a, j: a[:, j * LANES:(j + 1) * LANES]
    n_qblk = GQA_HEADS * GQA_HEAD_DIM // LANES
    w2 = 2 * LANES

    def project(j):
        rows = slice(j * TILE, (j + 1) * TILE)
        m = mod_refs[j][0, 0]
        hb = _modulated(_tile_rows(xl_refs, xc_ref, j, n_lat_tiles), g1_ref[0], m[0:1], m[1:2])
        p_c = _dot(hb, wa_ref[0, :, _CQ0:_CKV0])
        p_ckv = _dot(hb, wa_ref[0, :, _CKV0:_F0])
        pq = _dot(hb, wa_ref[0, :, _Q0:_K0])
        pkv = _dot(hb, wa_ref[0, :, _K0:_CQ0])
        pf_ref[0, rows] = _dot(hb, wa_ref[0, :, _F0:_WA]).astype(BF16)
        cq = _rms_rows(p_c[:, :MLA_Q_RANK], gcq_ref[0]).astype(BF16)
        ckv = _rms_rows(p_ckv, gckv_ref[0])
        qb = _dot(cq, wuq_ref[0])
        kn_all = _dot(ckv.astype(BF16), wkn_ref[0])
        vbt = _dot(wvt_ref[0], ckv.T.astype(BF16))
        for h in range(MLA_HEADS):
            vh = vbt[h * LANES:(h + 1) * LANES]
            ones_row = _ONES_ROW_EVEN if h % 2 == 0 else _ONES_ROW_ODD
            vbt_ref[0, h, :, rows] = jnp.where(row == ones_row, 1.0, vh).astype(BF16)
        vt = pkv[:, LANES:].T
        for kvh in range(GQA_KV_HEADS):
            own = (row >= 64) == (kvh == 1)
            top = jnp.where(own, vt, 0.0) if kvh == 0 else pltpu.roll(jnp.where(own, vt, 0.0), 64, 0)
            bot = pltpu.roll(top, 64, 0)
            vat_ref[0, 2 * kvh, :, rows] = jnp.where(row == _ONES_ROW_EVEN, 1.0, top).astype(BF16)
            vat_ref[0, 2 * kvh + 1, :, rows] = jnp.where(row == _ONES_ROW_ODD, 1.0, bot).astype(BF16)
        return pq, pkv[:, :LANES], qb, kn_all, p_c[:, MLA_Q_RANK:]

    def finish(j, pq, xk, qb, kn_all, xkr):
        rows = slice(j * TILE, (j + 1) * TILE)
        r_qb = jnp.concatenate([_seg_rsqrt(qb[:, c * w2:(c + 1) * w2], sqb_ref) for c in range(MLA_HEADS // 2)],
                               axis=1)
        r_kn = jnp.concatenate([_seg_rsqrt(kn_all[:, c * w2:(c + 1) * w2], skn_ref) for c in range(MLA_HEADS // 2)],
                               axis=1)
        r_q = jnp.concatenate([_seg_rsqrt(pq[:, c * w2:(c + 1) * w2], sgqa_ref) for c in range(n_qblk // 2)], axis=1)
        r_kk = _seg_rsqrt(jnp.concatenate([xk, xkr], axis=1), skk_ref)
        r_k, r_kr = r_kk[:, :LANES], r_kk[:, LANES:]

        tab = [tab_ref[i, rows] * tg_ref[0, i:i + 1] for i in range(2 * N_TABLES)]

        def roped(xb, table, seg, r):
            return (xb * tab[table] + _rot_half(xb, lane, seg) * tab[N_TABLES + table]) * r

        for h in range(MLA_HEADS):
            qb_ref[0, rows, h * LANES:(h + 1) * LANES] = roped(blk(qb, h), TAB_QB, 8, blk(r_qb, h)).astype(BF16)
        kr = roped(xkr, TAB_KR, 8, r_kr)
        for h in range(MLA_HEADS):
            kb_ref[0, rows, h * LANES:(h + 1) * LANES] = (blk(kn_all, h) * blk(r_kn, h) * gkn_ref[0] + kr).astype(BF16)
        for c in range(n_qblk):
            qa_ref[0, rows, c * LANES:(c + 1) * LANES] = roped(blk(pq, c), TAB_Q, 16, blk(r_q, c)).astype(BF16)
        kn = roped(xk, TAB_K, 16, r_k)
        sw = pltpu.roll(kn, 64, 1)
        for c, f in enumerate((jnp.where(lo, kn, 0.0), jnp.where(hi, sw, 0.0), jnp.where(lo, sw, 0.0),
                               jnp.where(hi, kn, 0.0))):
            ka_ref[0, rows, c * LANES:(c + 1) * LANES] = f.astype(BF16)

    pending = project(0)
    for j in range(sub):
        nxt = project(j + 1) if j + 1 < sub else None
        finish(j, *pending)
        pending = nxt


def _inproj(x_lat, x_ctx, ctx_tile, mod, l, n_lat_tiles, g1, wts, tabs, segs):
    b = x_lat.shape[0]
    nt = n_lat_tiles + 1
    sub = _sub_tiles(nt)
    t, step_rows = nt * TILE, sub * TILE
    row = lambda bi, si: (bi, si, 0)
    rowt = lambda bi, si: (bi, 0, 0, si)
    tok = lambda w: (pl.BlockSpec((1, step_rows, w), row), jax.ShapeDtypeStruct((b, t, w), BF16))
    tra = lambda h: (pl.BlockSpec((1, h, LANES, step_rows), rowt), jax.ShapeDtypeStruct((b, h, LANES, t), BF16))
    outs = [tok(512), tok(512), tra(4), tok(1024), tok(1024), tra(MLA_HEADS), tok(512)]
    return pl.pallas_call(
        functools.partial(_inproj_kernel, n_lat_tiles=n_lat_tiles, sub=sub),
        grid=(b, nt // sub),
        in_specs=_row_specs(x_lat, x_ctx, ctx_tile, n_lat_tiles, sub) + _mod_specs(mod, l, b, n_lat_tiles, sub)
        + [_layer_spec(g1, l)] + [_layer_spec(w, l) for w in wts]
        + [pl.BlockSpec((2 * N_TABLES, step_rows, LANES), lambda bi, si: (0, si, 0))]
        + [_const_spec(s) for s in segs],
        out_specs=[o[0] for o in outs],
        out_shape=[o[1] for o in outs],
        compiler_params=_cparams(2),
        name="inproj",
    )(*([x_lat] * sub), x_ctx, *([mod] * sub), g1, *wts, tabs, *segs)


def _reduce_keys(a, op):
    nk, nq = a.shape
    part = op(a.reshape(nk // TILE, TILE, nq), axis=0)
    return op(part, axis=0, keepdims=True)


SCORE_SAFE = 60.0
BOUND_SLACK = 1.0 + 2.0 ** -6


def _attn_kernel(bound_ref, q_ref, k_ref, vt_ref, o_ref, *, n_lat_tiles, with_ctx, qoff, koff, vidx):
    n_keys = k_ref.shape[1]
    n_pairs = len(qoff) // 2
    zero = jnp.zeros((TILE, LANES), BF16)
    bound = bound_ref[0]

    def scores(row0, pair, key0, nk):
        e, o = 2 * pair, 2 * pair + 1
        assert koff[o] == koff[e] + LANES
        q_e = q_ref[0, pl.ds(row0, TILE), qoff[e]:qoff[e] + LANES]
        q_o = q_ref[0, pl.ds(row0, TILE), qoff[o]:qoff[o] + LANES]
        qd = jnp.concatenate([jnp.concatenate([q_e, zero], axis=1), jnp.concatenate([zero, q_o], axis=1)], axis=0)
        k2 = k_ref[0, key0:key0 + nk, koff[e]:koff[e] + 2 * LANES]
        return lax.dot_general(k2, qd, (((1,), (1,)), ((), ())), preferred_element_type=F32)

    def tile(row0, key0, nk, bounded):
        for pair in range(n_pairs):
            e, o = 2 * pair, 2 * pair + 1
            st = scores(row0, pair, key0, nk)
            pb = jnp.exp2(st if bounded else st - _reduce_keys(st, jnp.max)).astype(BF16)
            ot_e = _dot(vt_ref[0, vidx[e], :, key0:key0 + nk], pb[:, :TILE])
            ot_o = _dot(vt_ref[0, vidx[o], :, key0:key0 + nk], pb[:, TILE:])
            ot_e = ot_e * (1.0 / ot_e[_ONES_ROW_EVEN:_ONES_ROW_EVEN + 1])
            ot_o = ot_o * (1.0 / ot_o[_ONES_ROW_ODD:_ONES_ROW_ODD + 1])
            ot = jnp.where(lax.broadcasted_iota(jnp.int32, (LANES, TILE), 0) < 64, ot_e, ot_o)
            o_ref[0, pl.ds(row0, TILE), pair * LANES:(pair + 1) * LANES] = ot.T.astype(BF16)

    def all_tiles(bounded):
        def body(t, carry):
            tile(pl.multiple_of(t * TILE, TILE), 0, n_keys, bounded)
            return carry
        lax.fori_loop(0, n_lat_tiles, body, 0, unroll=2 if bounded else 1)
        if with_ctx:
            lat = n_lat_tiles * TILE
            tile(lat, lat, n_keys - lat, bounded)

    lax.cond(bound < SCORE_SAFE, lambda: all_tiles(True), lambda: all_tiles(False))


def _attention(shift, q, k, vt, n_lat_tiles, n_q_tiles, qoff, koff, vidx, name):
    b, t, wq = q.shape
    wk = k.shape[2]
    nh = vt.shape[1]
    kern = functools.partial(_attn_kernel, n_lat_tiles=n_lat_tiles, with_ctx=n_q_tiles > n_lat_tiles,
                             qoff=qoff, koff=koff, vidx=vidx)
    return pl.pallas_call(
        kern,
        grid=(b,),
        in_specs=[pl.BlockSpec(memory_space=pltpu.SMEM),
                  pl.BlockSpec((1, t, wq), lambda bi: (bi, 0, 0)),
                  pl.BlockSpec((1, t, wk), lambda bi: (bi, 0, 0)),
                  pl.BlockSpec((1, nh, LANES, t), lambda bi: (bi, 0, 0, 0))],
        out_specs=pl.BlockSpec((1, n_q_tiles * TILE, 512), lambda bi: (bi, 0, 0)),
        out_shape=jax.ShapeDtypeStruct((b, n_q_tiles * TILE, 512), BF16),
        compiler_params=_cparams(1),
        name=name,
    )(shift, q, k, vt)


def _fourier_kernel(pf_ref, cn_ref, sn_ref, dc_ref, cs_ref, flip_ref, alt_ref, o_ref, a_ref, b_ref, *, n_lat_tiles,
                    with_ctx):
    n_lat = n_lat_tiles * TILE
    half_tiles = n_lat_tiles // 2
    w = FOURIER_WIDTH
    for c in range(n_lat_tiles):
        ab = _dot(pf_ref[0, c * TILE:(c + 1) * TILE, :], cs_ref[...])
        a_ref[c * TILE:(c + 1) * TILE, :] = ab[:, :w].astype(BF16)
        b_ref[c * TILE:(c + 1) * TILE, :] = ab[:, w:].astype(BF16)
    row0 = lax.broadcasted_iota(jnp.int32, (TILE, w), 0) == 0
    carry = _dot(alt_ref[...], a_ref[...])[0:1] * (1.0 / math.sqrt(n_lat))
    for m in reversed(range(half_tiles)):
        g = _dot(cn_ref[m * TILE:(m + 1) * TILE, :], a_ref[...])
        h = _dot(sn_ref[m * TILE:(m + 1) * TILE, :], b_ref[...])
        o_ref[0, m * TILE:(m + 1) * TILE, :] = (g - h).astype(BF16)
        mirrored = (g + h).astype(BF16)
        tile_rev = _dot(flip_ref[...], mirrored)
        bot = n_lat_tiles - 1 - m
        o_ref[0, bot * TILE:(bot + 1) * TILE, :] = jnp.where(row0, carry, tile_rev).astype(BF16)
        carry = mirrored[0:1].astype(F32)
    if with_ctx:
        ab = _dot(pf_ref[0, n_lat:n_lat + TILE, :], cs_ref[...])
        abc = jnp.concatenate([ab[:, :w], ab[:, w:]], axis=0).astype(BF16)
        o_ref[0, n_lat:n_lat + TILE, :] = _dot(dc_ref[...], abc).astype(BF16)


def _fourier(pf, tables, n_lat_tiles, n_tiles):
    b, t, w = pf.shape
    n_lat = n_lat_tiles * TILE
    kern = functools.partial(_fourier_kernel, n_lat_tiles=n_lat_tiles, with_ctx=n_tiles > n_lat_tiles)
    return pl.pallas_call(
        kern,
        grid=(b,),
        in_specs=[pl.BlockSpec((1, t, w), lambda bi: (bi, 0, 0))] + [_const_spec(a) for a in tables],
        out_specs=pl.BlockSpec((1, n_tiles * TILE, w), lambda bi: (bi, 0, 0)),
        out_shape=jax.ShapeDtypeStruct((b, n_tiles * TILE, w), BF16),
        scratch_shapes=[pltpu.VMEM((n_lat, w), BF16), pltpu.VMEM((n_lat, w), BF16)],
        compiler_params=_cparams(1),
        name="fourier",
    )(pf, *tables)


def _merge_kernel(*refs, n_lat_tiles, sub):
    xl_refs, refs = refs[:sub], refs[sub:]
    xc_ref, refs = refs[0], refs[1:]
    mod_refs, refs = refs[:sub], refs[sub:]
    g1_ref, ya_ref, yb_ref, yc_ref, wg_ref, bg_ref, wa_ref, wb_ref, wc_ref, wo_ref, o_ref = refs
    for j in range(sub):
        rows = slice(j * TILE, (j + 1) * TILE)
        x = _tile_rows(xl_refs, xc_ref, j, n_lat_tiles)
        m = mod_refs[j][0, 0]
        d = x.shape[-1]
        hb = _modulated(x, g1_ref[0], m[0:1], m[1:2])
        acc = None
        for i, (y_ref, w_ref) in enumerate(((ya_ref, wa_ref), (yb_ref, wb_ref), (yc_ref, wc_ref))):
            gate = jax.nn.sigmoid(_dot(hb, wg_ref[0, :, i * d:(i + 1) * d]) + bg_ref[0, :, i * d:(i + 1) * d])
            term = gate * _dot(y_ref[0, rows], w_ref[0])
            acc = term if acc is None else acc + term
        o_ref[0, rows] = x + m[2:3] * _dot(acc.astype(BF16), wo_ref[0])


def _merge(x_lat, x_ctx, ctx_tile, mod, l, n_lat_tiles, n_tiles, g1, ya, yb, yc, wts):
    b, _, d = x_lat.shape
    sub = _sub_tiles(n_tiles)
    row = lambda bi, si: (bi, si, 0)
    return pl.pallas_call(
        functools.partial(_merge_kernel, n_lat_tiles=n_lat_tiles, sub=sub),
        grid=(b, n_tiles // sub),
        in_specs=_row_specs(x_lat, x_ctx, ctx_tile, n_lat_tiles, sub) + _mod_specs(mod, l, b, n_lat_tiles, sub)
        + [_layer_spec(g1, l)] + [pl.BlockSpec((1, sub * TILE, 512), row)] * 3 + [_layer_spec(w, l) for w in wts],
        out_specs=pl.BlockSpec((1, sub * TILE, d), row),
        out_shape=jax.ShapeDtypeStruct((b, n_tiles * TILE, d), F32),
        compiler_params=_cparams(2),
        name="merge",
    )(*([x_lat] * sub), x_ctx, *([mod] * sub), g1, ya, yb, yc, *wts)


def _ffn_kernel(*refs, sub):
    x_ref, refs = refs[0], refs[1:]
    mod_refs, refs = refs[:sub], refs[sub:]
    g2_ref, wi_ref, wo_ref, o_ref = refs
    for j in range(sub):
        rows = slice(j * TILE, (j + 1) * TILE)
        x = x_ref[0, rows]
        m = mod_refs[j][0, 0]
        hb = _modulated(x, g2_ref[0], m[3:4], m[4:5])
        gu = _dot(hb, wi_ref[0])
        hid = gu.shape[-1] // 2
        gate, up = gu[:, :hid], gu[:, hid:]
        act = (gate * jax.nn.sigmoid(gate) * up).astype(BF16)
        o_ref[0, rows] = x + m[5:6] * _dot(act, wo_ref[0])


def _ffn(xm, mod, l, n_lat_tiles, n_tiles, g2, wi, wo):
    b, _, d = xm.shape
    sub = _sub_tiles(n_tiles)
    row = lambda bi, si: (bi, si, 0)
    once = lambda a: pl.BlockSpec((1,) + a.shape[1:], lambda *_: (l,) + (0,) * (a.ndim - 1),
                                  pipeline_mode=pl.Buffered(1))
    return pl.pallas_call(
        functools.partial(_ffn_kernel, sub=sub),
        grid=(b, n_tiles // sub),
        in_specs=[pl.BlockSpec((1, sub * TILE, d), row)] + _mod_specs(mod, l, b, n_lat_tiles, sub)
        + [_layer_spec(g2, l), once(wi), once(wo)],
        out_specs=pl.BlockSpec((1, sub * TILE, d), row),
        out_shape=jax.ShapeDtypeStruct((b, n_tiles * TILE, d), F32),
        compiler_params=_cparams(2),
        name="ffn",
    )(xm, *([mod] * sub), g2, wi, wo)


def _partner(seg):
    lane = np.arange(LANES)
    return np.where((lane % (2 * seg)) < seg, lane + seg, lane - seg)


def _rope_tables(n_lat, n_ctx):
    rows = n_lat // GRID_W
    row_id = np.repeat(np.arange(rows), GRID_W).astype(np.float64)
    col_id = np.tile(np.arange(GRID_W), rows).astype(np.float64)

    def angles(dim):
        half = dim // 2
        freqs = ROPE_THETA ** (-np.arange(0, half, 2, dtype=np.float64) / half)
        ax = lambda pos: np.concatenate([pos[:, None] * freqs[None, :]] * 2, axis=-1)
        return np.concatenate([ax(row_id), ax(col_id)], axis=-1)

    def signed(sin, dim):
        sign = np.where((np.arange(dim) % (dim // 2)) < dim // 4, -1.0, 1.0)
        return sin * sign[None, :]

    a64, a32 = angles(GQA_HEAD_DIM), angles(MLA_ROPE_DIM)
    t = n_lat + n_ctx
    cosa, sina = np.ones((t, LANES)), np.zeros((t, LANES))
    cosa[:n_lat] = np.tile(np.cos(a64), (1, 2))
    sina[:n_lat] = np.tile(signed(np.sin(a64), GQA_HEAD_DIM), (1, 2))
    cosb, sinb = np.ones((t, LANES)), np.zeros((t, LANES))
    cosb[:n_lat, 64:96] = np.cos(a32)
    sinb[:n_lat, 64:96] = signed(np.sin(a32), MLA_ROPE_DIM)
    return jnp.asarray(np.stack([cosa, cosa, cosb, cosb, sina, sina, sinb, sinb]), F32)


TAB_Q, TAB_K, TAB_QB, TAB_KR, N_TABLES = 0, 1, 2, 3, 4


def _table_gains(gains, scales):
    g = jnp.stack(gains, axis=1) * jnp.asarray(scales, F32)[None, :, None]
    partner = np.stack([_partner(16), _partner(16), _partner(8), _partner(8)])
    return jnp.concatenate([g, jnp.take_along_axis(g, jnp.asarray(partner)[None], axis=-1)], axis=1)


def _seg_matrices():
    def blockdiag(segs):
        m = np.zeros((2 * LANES, 2 * LANES))
        for start, n in segs:
            m[start:start + n, start:start + n] = 1.0 / n
        return m

    gqa = blockdiag([(s0, 64) for s0 in range(0, 256, 64)])
    qb = blockdiag([(0, 64), (64, 32), (128, 64), (192, 32)])
    kn = blockdiag([(0, 64), (128, 64)])
    kk = blockdiag([(0, 64), (64, 64), (192, 32)])
    return [jnp.asarray(a, F32).astype(BF16) for a in (gqa, qb, kn, kk)]


def _score_bounds(g_q_gqa, g_k_gqa, g_q_nope, g_k_nope, g_q_rope, g_k_rope):
    top = lambda g: jnp.max(g * g, axis=-1)
    gqa = jnp.sqrt(GQA_HEAD_DIM * top(g_q_gqa) * GQA_HEAD_DIM * top(g_k_gqa)) * (GQA_SCALE * LOG2E)
    mla = jnp.sqrt((MLA_NOPE_DIM * top(g_q_nope) + MLA_ROPE_DIM * top(g_q_rope))
                   * (MLA_NOPE_DIM * top(g_k_nope) + MLA_ROPE_DIM * top(g_k_rope))) * (MLA_SCALE * LOG2E)
    return jnp.stack([gqa, mla], axis=-1) * BOUND_SLACK


def _dft_tables(n_lat, n_ctx):
    def cs(n):
        j = np.arange(n)
        ang = 2.0 * np.pi * ((j[:, None] * j[None, :]) % n) / n
        return np.cos(ang) / np.sqrt(n), np.sin(ang) / np.sqrt(n)

    cn, sn = cs(n_lat)
    cc, sc = cs(n_ctx)
    cg, sg = cs(FOURIER_GROUP_DIM)
    eye = np.eye(FOURIER_GROUPS)
    dc = np.concatenate([cc, -sc], axis=1)
    chan = np.concatenate([np.kron(eye, cg), np.kron(eye, sg)], axis=1)
    flip = np.zeros((TILE, TILE))
    flip[np.arange(1, TILE), TILE - np.arange(1, TILE)] = 1.0
    alt = np.tile(np.where(np.arange(n_lat) % 2 == 0, 1.0, -1.0)[None, :], (8, 1))
    return [jnp.asarray(a, F32).astype(BF16) for a in (cn[:n_lat // 2], sn[:n_lat // 2], dc, chan, flip, alt)]


_SRC_CKV0, _SRC_KR0, _SRC_F0, _GATE0, _W_IN_COLS = 1152, 1408, 1440, 1952, 5024


def _split_kernel(wt_ref, wa_ref, wg_ref):
    d = wt_ref.shape[2]
    piece = 256

    def emit(out_ref, col0, row0, n):
        for c in range(0, n, piece):
            w = min(piece, n - c)
            out_ref[0, :, col0 + c:col0 + c + w] = wt_ref[0, row0 + c:row0 + c + w, :].T.astype(BF16)

    emit(wa_ref, _Q0, 0, _KR0)
    kr_rows = jnp.concatenate([jnp.zeros((64, d), F32), wt_ref[0, _SRC_KR0:_SRC_F0, :], jnp.zeros((32, d), F32)],
                              axis=0)
    wa_ref[0, :, _KR0:_CKV0] = kr_rows.T.astype(BF16)
    emit(wa_ref, _CKV0, _SRC_CKV0, _F0 - _CKV0)
    emit(wa_ref, _F0, _SRC_F0, _WA - _F0)
    emit(wg_ref, 0, _GATE0, _W_IN_COLS - _GATE0)


def _cast_kernel(w_ref, o_ref):
    o_ref[...] = w_ref[...].astype(BF16)


def _cast_bf16(w):
    depth, rows, cols = w.shape
    block = (1, rows // 4, cols)
    spec = pl.BlockSpec(block, lambda l, r: (l, r, 0))
    return pl.pallas_call(
        _cast_kernel,
        grid=(depth, 4),
        in_specs=[spec],
        out_specs=spec,
        out_shape=jax.ShapeDtypeStruct(w.shape, BF16),
        compiler_params=_cparams(2),
        name="cast_bf16",
    )(w)


def _split_w_in(w_in):
    depth, d, cols = w_in.shape
    assert cols == _W_IN_COLS
    w_t = jnp.swapaxes(w_in, 1, 2)
    return pl.pallas_call(
        _split_kernel,
        grid=(depth,),
        in_specs=[pl.BlockSpec((1, cols, d), lambda l: (l, 0, 0), pipeline_mode=pl.Buffered(1))],
        out_specs=[pl.BlockSpec((1, d, _WA), lambda l: (l, 0, 0)),
                   pl.BlockSpec((1, d, cols - _GATE0), lambda l: (l, 0, 0))],
        out_shape=[jax.ShapeDtypeStruct((depth, d, _WA), BF16), jax.ShapeDtypeStruct((depth, d, cols - _GATE0), BF16)],
        compiler_params=_cparams(1),
        name="split_w_in",
    )(w_t)


def _inproj_weights(w_in, g_q_gqa, g_k_gqa, g_cq, g_ckv, w_uq, w_ukv, g_q_nope, g_k_nope, g_q_rope, g_k_rope):
    depth = w_in.shape[0]
    wa, gates = _split_w_in(w_in)
    zeros = lambda *s: jnp.zeros((depth,) + s, F32)
    uq = w_uq.reshape(depth, MLA_Q_RANK, MLA_HEADS, MLA_NOPE_DIM + MLA_ROPE_DIM)
    wuq = jnp.concatenate([uq, zeros(MLA_Q_RANK, MLA_HEADS, 32)], axis=-1).reshape(depth, MLA_Q_RANK, -1)
    ukv = w_ukv.reshape(depth, MLA_KV_RANK, MLA_HEADS, MLA_NOPE_DIM + MLA_V_DIM)
    kn, vv = ukv[..., :MLA_NOPE_DIM], ukv[..., MLA_NOPE_DIM:]
    z64 = zeros(MLA_KV_RANK, MLA_HEADS, 64)
    wkn = jnp.concatenate([kn, z64], axis=-1).reshape(depth, MLA_KV_RANK, -1)
    even = (jnp.arange(MLA_HEADS) % 2 == 0)[None, None, :, None]
    wv = jnp.where(even, jnp.concatenate([vv, z64], axis=-1), jnp.concatenate([z64, vv], axis=-1))
    wvt = jnp.swapaxes(wv.reshape(depth, MLA_KV_RANK, -1), 1, 2)
    gkn = jnp.concatenate([g_k_nope, zeros(64)], axis=-1)[:, None, :]
    z32, z64v = zeros(32), zeros(64)
    tgains = _table_gains([jnp.tile(g_q_gqa, (1, 2)), jnp.tile(g_k_gqa, (1, 2)),
                           jnp.concatenate([g_q_nope, g_q_rope, z32], axis=-1),
                           jnp.concatenate([z64v, g_k_rope, z32], axis=-1)],
                          [GQA_SCALE * LOG2E, 1.0, MLA_SCALE * LOG2E, 1.0])
    wts = [wa, wuq.astype(BF16), wkn.astype(BF16), wvt.astype(BF16), g_cq[:, None, :], g_ckv[:, None, :], gkn,
           tgains]
    return wts, gates


def kernel(x, c, ctx, c_ctx, w_mod, b_mod, g_norm1, g_norm2, w_in, g_q_gqa, g_k_gqa, g_cq, g_ckv, w_uq, w_ukv,
           g_q_nope, g_k_nope, g_q_rope, g_k_rope, b_gate, w_br_a, w_br_b, w_br_c, w_out, w_ffn_in, w_ffn_out):
    b, n, d = x.shape
    nc = ctx.shape[1]
    depth = w_mod.shape[0]
    assert n % TILE == 0 and nc == TILE and b + 1 <= MOD_ROWS and n % GRID_W == 0
    n_lat_tiles, n_tiles = n // TILE, (n + nc) // TILE

    cc = jnp.concatenate([c, c_ctx[None], jnp.zeros((MOD_ROWS - b - 1, d), F32)], axis=0)
    mod = _modulation(cc, w_mod, b_mod).reshape(depth, MOD_ROWS, 6, d)
    dft = _dft_tables(n, nc)
    segs = _seg_matrices()
    bounds = _score_bounds(g_q_gqa, g_k_gqa, g_q_nope, g_k_nope, g_q_rope, g_k_rope)
    tabs = _rope_tables(n, nc)
    wts, wg = _inproj_weights(w_in, g_q_gqa, g_k_gqa, g_cq, g_ckv, w_uq, w_ukv, g_q_nope, g_k_nope, g_q_rope,
                              g_k_rope)
    g1, g2 = g_norm1[:, None, :], g_norm2[:, None, :]
    mw = [wg, b_gate[:, None, :], w_br_a.astype(BF16), w_br_b.astype(BF16), w_br_c.astype(BF16), w_out.astype(BF16)]
    wi, wo = _cast_bf16(w_ffn_in), _cast_bf16(w_ffn_out)
    gqa_q = (0, 0, 128, 128, 256, 256, 384, 384)
    gqa_k = (0, 128, 0, 128, 256, 384, 256, 384)
    gqa_v = (0, 1, 0, 1, 2, 3, 2, 3)
    mla_o = tuple(range(0, 1024, 128))
    mla_h = tuple(range(MLA_HEADS))

    x_lat, x_ctx, ctx_tile = x, ctx, 0
    for l in range(depth):
        n_out = n_lat_tiles if l == depth - 1 else n_tiles
        qa, ka, vat, qb, kb, vbt, pf = _inproj(x_lat, x_ctx, ctx_tile, mod, l, n_lat_tiles, g1, wts, tabs, segs)
        ya = _attention(bounds[l, 0:1], qa, ka, vat, n_lat_tiles, n_out, gqa_q, gqa_k, gqa_v, "attn_gqa")
        yb = _attention(bounds[l, 1:2], qb, kb, vbt, n_lat_tiles, n_out, mla_o, mla_o, mla_h, "attn_mla")
        yc = _fourier(pf, dft, n_lat_tiles, n_out)
        xm = _merge(x_lat, x_ctx, ctx_tile, mod, l, n_lat_tiles, n_out, g1, ya, yb, yc, mw)
        xall = _ffn(xm, mod, l, n_lat_tiles, n_out, g2, wi, wo)
        x_lat, x_ctx, ctx_tile = xall, xall, n_lat_tiles
    return xall
```
